```python
import jax, jax.numpy as jnp
from jax import lax
import numpy as np

D_MODEL = 1024
BATCH = 2
SEQ = 8192
DEPTH = 4

GRID_W = 64
CTX_LEN = 256

DEEPNORM_ALPHA = (2 * DEPTH) ** 0.25
DEEPNORM_BETA = (8 * DEPTH) ** -0.25
LN_EPS = 1e-6

N_EVEN = (DEPTH + 1) // 2
N_ODD = DEPTH // 2

GLA_HEADS = 4
GLA_DK = 64
GLA_DV = 128
GLA_QK_W = GLA_HEADS * GLA_DK
GLA_V_W = GLA_HEADS * GLA_DV
GLA_GATE_RANK = 16
GLA_TAU = 16.0
GLA_CHUNK = 64

MLA_HEADS = 8
MLA_NOPE = 64
MLA_ROPE = 32
MLA_V = 64
MLA_Q_RANK = 256
MLA_KV_RANK = 128
MLA_V_W = MLA_HEADS * MLA_V
MLA_SCALE = (MLA_NOPE + MLA_ROPE) ** -0.5
ROPE_BASE = 10000.0
Q_BLOCK = 128

FNET_GROUPS = 4
FNET_GROUP_CH = 128
FNET_W = FNET_GROUPS * FNET_GROUP_CH

SGU_GROUPS = 4
SGU_GROUP_CH = 128
SGU_W = SGU_GROUPS * SGU_GROUP_CH
SGU_CHUNK = 128

EVEN_IN_SIZES = (GLA_QK_W, GLA_QK_W, GLA_V_W, 2 * GLA_GATE_RANK, GLA_V_W,
                 MLA_Q_RANK, MLA_KV_RANK, MLA_ROPE, MLA_V_W)
ODD_IN_SIZES = (FNET_W, FNET_W, SGU_W, SGU_W, SGU_W)
EVEN_OUT_IN = GLA_V_W + MLA_V_W
ODD_OUT_IN = FNET_W + SGU_W

kernel_name = "hybrid_gla_mla_fnet_sgu_prefix_dit"


def split_cols(z, sizes):
    idx = np.cumsum(sizes)[:-1].tolist()
    return jnp.split(z, idx, axis=-1)


def layer_norm(x, g=None, b=None):
    xf = x.astype(jnp.float32)
    xc = xf - jnp.mean(xf, -1, keepdims=True)
    y = xc * lax.rsqrt(jnp.mean(xc * xc, -1, keepdims=True) + LN_EPS)
    if g is not None:
        y = y * g.astype(jnp.float32) + b.astype(jnp.float32)
    return y.astype(x.dtype)


def rms_norm(x, g):
    xf = x.astype(jnp.float32)
    y = xf * lax.rsqrt(jnp.mean(xf * xf, -1, keepdims=True) + LN_EPS)
    return (y * g.astype(jnp.float32)).astype(x.dtype)


def ada_mod(cond, w, b):
    m = jnp.dot(jax.nn.silu(cond), w) + b
    return jnp.split(m, 3, axis=-1)


def modulate(x, shift, scale):
    return layer_norm(x) * (1 + scale[..., None, :]) + shift[..., None, :]


def axial_rope_angles(n):
    rows = n // GRID_W
    row = jnp.repeat(jnp.arange(rows, dtype=jnp.float32), GRID_W)
    col = (jnp.arange(n) % GRID_W).astype(jnp.float32)
    n_freq = MLA_ROPE // 4
    inv = ROPE_BASE ** (-jnp.arange(n_freq, dtype=jnp.float32) / n_freq)
    ang = jnp.concatenate([row[:, None] * inv, col[:, None] * inv], -1)
    return jnp.cos(ang), jnp.sin(ang)


def apply_rope(x, cos, sin):
    xf = x.astype(jnp.float32)
    x1, x2 = jnp.split(xf, 2, axis=-1)
    return jnp.concatenate([x1 * cos - x2 * sin, x1 * sin + x2 * cos], -1).astype(x.dtype)


def gla_q(q):
    B, n, _ = q.shape
    return q.astype(jnp.float32).reshape(B, n, GLA_HEADS, GLA_DK) * GLA_DK ** -0.5


def gla_kvg(k, v, lr, w2, b):
    B, n, _ = k.shape
    heads = lambda a, d: a.astype(jnp.float32).reshape(B, n, GLA_HEADS, d)
    lr_dirs = jnp.split(lr.astype(jnp.float32), 2, axis=-1)
    g = tuple(
        heads(jax.nn.log_sigmoid(jnp.einsum('bnr,re->bne', lr_dirs[d], w2[d].astype(jnp.float32))
                                 + b[d].astype(jnp.float32)) / GLA_TAU, GLA_DK)
        for d in range(2))
    return heads(k, GLA_DK), heads(v, GLA_DV), g


def gla_chunk_scan(q, k, v, g, s0):
    B, n, H, _ = q.shape
    nc = n // GLA_CHUNK
    chunk = lambda a: a.reshape(B, nc, GLA_CHUNK, H, a.shape[-1])
    q, k, v, g = chunk(q), chunk(k), chunk(v), chunk(g)
    b = jnp.cumsum(g, axis=2)
    b_mid = b[:, :, GLA_CHUNK // 2 - 1:GLA_CHUNK // 2]
    b_end = b[:, :, -1:]
    att = jnp.einsum('bcthk,bcshk->bchts', q * jnp.exp(b - b_mid), k * jnp.exp(b_mid - b))
    tri = jnp.tril(jnp.ones((GLA_CHUNK, GLA_CHUNK), dtype=bool))
    o = jnp.einsum('bchts,bcshv->bcthv', jnp.where(tri, att, 0.0), v)
    ds = jnp.einsum('bcshk,bcshv->bchkv', k * jnp.exp(b_end - b), v)
    decay = jnp.exp(b_end[:, :, 0])

    def step(s, inp):
        d, dsc = inp
        return d[..., None] * s + dsc, s

    s_last, s_start = lax.scan(step, s0, (jnp.moveaxis(decay, 1, 0), jnp.moveaxis(ds, 1, 0)))
    o = o + jnp.einsum('bcthk,cbhkv->bcthv', q * jnp.exp(b), s_start)
    return o.reshape(B, n, H, v.shape[-1]), s_last


def gla_final_state(k, v, g):
    bc = jnp.cumsum(g, axis=1)
    return jnp.einsum('bshk,bshv->bhkv', k * jnp.exp(bc[:, -1:] - bc), v)


def gla_bidir(q_l, k_l, v_l, g_l, k_c, v_c, g_c, q_c):
    B, _, H, _ = k_c.shape
    ident = lambda a: a
    rev = lambda a: a[:, ::-1]
    o_l, o_c = 0.0, 0.0
    for d, order in ((0, ident), (1, rev)):
        if q_c is None:
            s_c = gla_final_state(order(k_c), order(v_c), order(g_c[d]))
        else:
            s0 = jnp.zeros((B, H, GLA_DK, GLA_DV), jnp.float32)
            oc, s_c = gla_chunk_scan(order(q_c), order(k_c), order(v_c), order(g_c[d]), s0)
            o_c = o_c + order(oc)
        ol, _ = gla_chunk_scan(order(q_l), order(k_l), order(v_l), order(g_l[d]), s_c)
        o_l = o_l + order(ol)
    return o_l, (o_c if q_c is not None else None)


def mla_q(cq, q_norm_g, w_uq, rope):
    B, n, _ = cq.shape
    q = jnp.einsum('bnr,re->bne', rms_norm(cq, q_norm_g), w_uq).reshape(B, n, MLA_HEADS, MLA_NOPE + MLA_ROPE)
    q_nope, q_rope = q[..., :MLA_NOPE], q[..., MLA_NOPE:]
    if rope is not None:
        q_rope = apply_rope(q_rope, rope[0][:, None, :], rope[1][:, None, :])
    return q_nope, q_rope


def mla_kv(ckv, kr, kv_norm_g, w_ukv, rope):
    B, n, _ = ckv.shape
    kv = jnp.einsum('bnr,re->bne', rms_norm(ckv, kv_norm_g), w_ukv).reshape(B, n, MLA_HEADS, MLA_NOPE + MLA_V)
    if rope is not None:
        kr = apply_rope(kr, rope[0], rope[1])
    return kv[..., :MLA_NOPE], kr, kv[..., MLA_NOPE:]


def mla_attend(q_nope, q_rope, k_nope, k_rope, v):
    s = (jnp.einsum('bqhd,bkhd->bhqk', q_nope, k_nope, preferred_element_type=jnp.float32)
         + jnp.einsum('bqhr,bkr->bhqk', q_rope, k_rope, preferred_element_type=jnp.float32)) * MLA_SCALE
    p = jax.nn.softmax(s, axis=-1).astype(v.dtype)
    return jnp.einsum('bhqk,bkhd->bqhd', p, v)


def mla_latent_blocks(q_nope, q_rope, k_nope, k_rope, v):
    B, n = q_nope.shape[:2]
    nb = n // Q_BLOCK
    blocks = lambda a: jnp.moveaxis(a.reshape(B, nb, Q_BLOCK, *a.shape[2:]), 1, 0)
    o = lax.map(lambda qs: mla_attend(qs[0], qs[1], k_nope, k_rope, v), (blocks(q_nope), blocks(q_rope)))
    return jnp.moveaxis(o, 0, 1).reshape(B, n, MLA_V_W)


def even_mixer(h_lat, h_ctx, need_ctx_out, rope, w_in, gla_w2, gla_b, gla_norm_g,
               q_norm_g, w_uq, kv_norm_g, w_ukv, w_out):
    B, n, _ = h_lat.shape
    L = h_ctx.shape[1]
    proj = lambda h: split_cols(jnp.einsum('bnd,de->bne', h, w_in), EVEN_IN_SIZES)
    gq_l, gk_l, gv_l, glr_l, gg_l, cq_l, ckv_l, kr_l, mg_l = proj(h_lat)
    gq_c, gk_c, gv_c, glr_c, gg_c, cq_c, ckv_c, kr_c, mg_c = proj(h_ctx)

    k_l, v_l, g_l = gla_kvg(gk_l, gv_l, glr_l, gla_w2, gla_b)
    k_c, v_c, g_c = gla_kvg(gk_c, gv_c, glr_c, gla_w2, gla_b)
    o_gl, o_gc = gla_bidir(gla_q(gq_l), k_l, v_l, g_l, k_c, v_c, g_c,
                           gla_q(gq_c) if need_ctx_out else None)
    y_gla = rms_norm(o_gl, gla_norm_g).reshape(B, n, GLA_V_W).astype(h_lat.dtype) * jax.nn.silu(gg_l)

    kn_c, krot_c, vm_c = mla_kv(ckv_c, kr_c, kv_norm_g, w_ukv, None)
    kn_l, krot_l, vm_l = mla_kv(ckv_l, kr_l, kv_norm_g, w_ukv, rope)
    qn_l, qr_l = mla_q(cq_l, q_norm_g, w_uq, rope)
    o_ml = mla_latent_blocks(qn_l, qr_l,
                             jnp.concatenate([kn_c, kn_l], 1),
                             jnp.concatenate([krot_c, krot_l], 1),
                             jnp.concatenate([vm_c, vm_l], 1))
    y_mla = o_ml * jax.nn.silu(mg_l)
    y_lat = jnp.einsum('bne,ed->bnd', jnp.concatenate([y_gla, y_mla], -1), w_out)

    y_ctx = None
    if need_ctx_out:
        yg_c = rms_norm(o_gc, gla_norm_g).reshape(B, L, GLA_V_W).astype(h_ctx.dtype) * jax.nn.silu(gg_c)
        qn_c, qr_c = mla_q(cq_c, q_norm_g, w_uq, None)
        ym_c = mla_attend(qn_c, qr_c, kn_c, krot_c, vm_c).reshape(B, L, MLA_V_W) * jax.nn.silu(mg_c)
        y_ctx = jnp.einsum('bne,ed->bnd', jnp.concatenate([yg_c, ym_c], -1), w_out)
    return y_lat, y_ctx


def odd_mixer(h, w_in, sgu_w, sgu_b, w_out):
    B, n, _ = h.shape
    f, f_gate, u, v, s_gate = split_cols(jnp.einsum('bnd,de->bne', h, w_in), ODD_IN_SIZES)
    fg = f.astype(jnp.float32).reshape(B, n, FNET_GROUPS, FNET_GROUP_CH)
    fr = jnp.fft.fft2(fg, axes=(1, 3), norm='ortho').real.reshape(B, n, FNET_W).astype(h.dtype)
    y_f = fr * jax.nn.silu(f_gate)
    u = jax.nn.gelu(u, approximate=False)
    vg = layer_norm(jax.nn.gelu(v, approximate=False).reshape(B, n // SGU_CHUNK, SGU_CHUNK, SGU_GROUPS, SGU_GROUP_CH))
    sv = jnp.einsum('gts,bcsgd->bctgd', sgu_w, vg) + sgu_b.T[:, :, None]
    y_s = u * sv.reshape(B, n, SGU_W) * jax.nn.silu(s_gate)
    return jnp.einsum('bne,ed->bnd', jnp.concatenate([y_f, y_s], -1), w_out)


def setup_inputs(seed: int = 0) -> dict:
    key = jax.random.key(seed)
    ks = iter(jax.random.split(key, 24))
    nrm = lambda shape, scale: jax.random.normal(next(ks), shape, jnp.float32) * scale
    D = D_MODEL
    return {
        "x": nrm((BATCH, SEQ, D), 1.0),
        "c": nrm((BATCH, D), 1.0),
        "ctx": nrm((BATCH, CTX_LEN, D), 1.0),
        "c_ctx": nrm((D,), 1.0),
        "ada_w": nrm((DEPTH, D, 3 * D), D ** -0.5),
        "ada_b": nrm((DEPTH, 3 * D), 0.02),
        "post_ln_g": 1.0 + nrm((DEPTH, D), 0.02),
        "post_ln_b": nrm((DEPTH, D), 0.02),
        "even_w_in": nrm((N_EVEN, D, sum(EVEN_IN_SIZES)), D ** -0.5),
        "gla_w2": nrm((N_EVEN, 2, GLA_GATE_RANK, GLA_QK_W), GLA_GATE_RANK ** -0.5),
        "gla_b": nrm((N_EVEN, 2, GLA_QK_W), 0.1),
        "gla_norm_g": 1.0 + nrm((N_EVEN, GLA_DV), 0.02),
        "mla_q_norm_g": 1.0 + nrm((N_EVEN, MLA_Q_RANK), 0.02),
        "mla_w_uq": nrm((N_EVEN, MLA_Q_RANK, MLA_HEADS * (MLA_NOPE + MLA_ROPE)), MLA_Q_RANK ** -0.5),
        "mla_kv_norm_g": 1.0 + nrm((N_EVEN, MLA_KV_RANK), 0.02),
        "mla_w_ukv": nrm((N_EVEN, MLA_KV_RANK, MLA_HEADS * (MLA_NOPE + MLA_V)), MLA_KV_RANK ** -0.5),
        "even_w_out": nrm((N_EVEN, EVEN_OUT_IN, D), EVEN_OUT_IN ** -0.5 * DEEPNORM_BETA),
        "odd_w_in": nrm((N_ODD, D, sum(ODD_IN_SIZES)), D ** -0.5),
        "sgu_w": nrm((N_ODD, SGU_GROUPS, SGU_CHUNK, SGU_CHUNK), SGU_CHUNK ** -0.5),
        "sgu_b": 1.0 + nrm((N_ODD, SGU_GROUPS, SGU_CHUNK), 0.1),
        "odd_w_out": nrm((N_ODD, ODD_OUT_IN, D), ODD_OUT_IN ** -0.5 * DEEPNORM_BETA),
    }


def reference(x, c, ctx, c_ctx, ada_w, ada_b, post_ln_g, post_ln_b, even_w_in, gla_w2, gla_b, gla_norm_g,
              mla_q_norm_g, mla_w_uq, mla_kv_norm_g, mla_w_ukv, even_w_out, odd_w_in, sgu_w, sgu_b, odd_w_out):
    rope = axial_rope_angles(x.shape[1])
    for l in range(DEPTH):
        need_ctx_out = any(j % 2 == 0 for j in range(l + 1, DEPTH))
        i = l // 2
        shift, scale, gate = ada_mod(c, ada_w[l], ada_b[l])
        h_lat = modulate(x, shift, scale)
        if l % 2 == 0 or need_ctx_out:
            shift_c, scale_c, gate_c = ada_mod(c_ctx, ada_w[l], ada_b[l])
            h_ctx = modulate(ctx, shift_c, scale_c)
        if l % 2 == 0:
            y_lat, y_ctx = even_mixer(h_lat, h_ctx, need_ctx_out, rope, even_w_in[i], gla_w2[i], gla_b[i],
                                      gla_norm_g[i], mla_q_norm_g[i], mla_w_uq[i], mla_kv_norm_g[i],
                                      mla_w_ukv[i], even_w_out[i])
        else:
            y_lat = odd_mixer(h_lat, odd_w_in[i], sgu_w[i], sgu_b[i], odd_w_out[i])
            y_ctx = odd_mixer(h_ctx, odd_w_in[i], sgu_w[i], sgu_b[i], odd_w_out[i]) if need_ctx_out else None
        x = layer_norm(DEEPNORM_ALPHA * x + gate[:, None, :] * y_lat, post_ln_g[l], post_ln_b[l])
        if need_ctx_out:
            ctx = layer_norm(DEEPNORM_ALPHA * ctx + gate_c[None, :] * y_ctx, post_ln_g[l], post_ln_b[l])
    return x
```

```python
import functools
import math

import numpy as np
import jax
import jax.numpy as jnp
from jax import lax
from jax.experimental import pallas as pl
from jax.experimental.pallas import tpu as pltpu

F32 = jnp.float32
BF16 = jnp.bfloat16

DEPTH = 4
GRID_W = 64
DEEPNORM_ALPHA = (2 * DEPTH) ** 0.25
LN_EPS = 1e-6

GLA_HEADS = 4
GLA_DK = 64
GLA_DV = 128
GLA_QK_W = GLA_HEADS * GLA_DK
GLA_V_W = GLA_HEADS * GLA_DV
GLA_GATE_RANK = 16
GLA_TAU = 16.0
GLA_CHUNK = 64

MLA_HEADS = 8
MLA_NOPE = 64
MLA_ROPE = 32
MLA_V = 64
MLA_Q_RANK = 256
MLA_KV_RANK = 128
MLA_V_W = MLA_HEADS * MLA_V
MLA_SCALE = (MLA_NOPE + MLA_ROPE) ** -0.5
ROPE_BASE = 10000.0
MLA_HEAD_PAD = 128

FNET_GROUPS = 4
FNET_GROUP_CH = 128
FNET_W = FNET_GROUPS * FNET_GROUP_CH
FFT_N1 = 64

SGU_GROUPS = 4
SGU_GROUP_CH = 128
SGU_W = SGU_GROUPS * SGU_GROUP_CH
SGU_CHUNK = 128

EVEN_IN_SIZES = (GLA_QK_W, GLA_QK_W, GLA_V_W, 2 * GLA_GATE_RANK, GLA_V_W,
                 MLA_Q_RANK, MLA_KV_RANK, MLA_ROPE, MLA_V_W)

E_GQ, E_GK, E_GV, E_GG, E_CQ, E_CKV, E_MG, E_LR, E_KRA, E_KRB, E_END = (
    0, 256, 512, 1024, 1536, 1792, 1920, 2432, 2560, 2688, 2816)

VMEM_LIMIT_BYTES = 56 * 1024 * 1024
LOG2E = math.log2(math.e)
NEG_BIG = -1e30


def _cparams(*sem):
    return pltpu.CompilerParams(dimension_semantics=sem, vmem_limit_bytes=VMEM_LIMIT_BYTES)


def _dot(a, b):
    return jnp.dot(a, b, preferred_element_type=F32)


def _dot_nt(a, b):
    return lax.dot_general(a, b, (((1,), (1,)), ((), ())), preferred_element_type=F32)


def _dot_tn(a, b):
    return lax.dot_general(a, b, (((0,), (0,)), ((), ())), preferred_element_type=F32)


def _ln(x):
    xc = x - jnp.mean(x, -1, keepdims=True)
    return xc * lax.rsqrt(jnp.mean(xc * xc, -1, keepdims=True) + LN_EPS)


def _rms(x):
    return x * lax.rsqrt(jnp.mean(x * x, -1, keepdims=True) + LN_EPS)


def _silu(x):
    return x / (1.0 + jnp.exp(-x))


def _gelu(x):
    return 0.5 * x * (1.0 + lax.erf(x * (2.0 ** -0.5)))


def _tile_lanes(x, reps):
    return jnp.concatenate([x] * reps, axis=1)


def _full(shape):
    n = len(shape)
    return pl.BlockSpec(shape, lambda *_: (0,) * n)


def _mods_kernel(cond_ref, w_ref, b_ref, o_ref):
    s = _silu(cond_ref[...])
    o_ref[0] = jnp.dot(s, w_ref[0], preferred_element_type=F32,
                       precision=lax.Precision.HIGHEST) + b_ref[0]


def _mods(cond, ada_w, ada_b):
    depth, d, d3 = ada_w.shape
    nb = d3 // d
    return pl.pallas_call(
        _mods_kernel,
        grid=(depth, nb),
        in_specs=[pl.BlockSpec((8, d), lambda l, j: (0, 0)),
                  pl.BlockSpec((1, d, d), lambda l, j: (l, 0, j)),
                  pl.BlockSpec((1, 1, d), lambda l, j: (l, 0, j))],
        out_specs=pl.BlockSpec((1, 8, d), lambda l, j: (l, 0, j)),
        out_shape=jax.ShapeDtypeStruct((depth, 8, d3), F32),
        compiler_params=_cparams("arbitrary", "arbitrary"),
        name="ada_mod",
    )(cond, ada_w, ada_b.reshape(depth, 1, d3))


def _even_pre_kernel(x_ref, shift_ref, scale_ref, w_ref, w2_ref, gb_ref, qg_ref, wqa_ref, wqb_ref,
                     kvg_ref, wk_ref, wv_ref, cq_ref, sq_ref, ck_ref, sk_ref,
                     q_ref, k_ref, v_ref, g_ref, sgg_ref, smg_ref, qt_ref, kk_ref, vt_ref):
    h = _ln(x_ref[0]) * (1.0 + scale_ref[0]) + shift_ref[0]
    z = _dot(h.astype(BF16), w_ref[...])
    q_ref[0] = z[:, E_GQ:E_GK].astype(BF16)
    k_ref[0] = z[:, E_GK:E_GV].astype(BF16)
    v_ref[0] = z[:, E_GV:E_GG].astype(BF16)
    sgg_ref[0] = _silu(z[:, E_GG:E_CQ]).astype(BF16)
    smg_ref[0] = _silu(z[:, E_MG:E_LR]).astype(BF16)
    pre = _dot(z[:, E_LR:E_KRA].astype(BF16), w2_ref[...]) + gb_ref[...]
    g_ref[0] = jax.nn.log_sigmoid(pre) * (1.0 / GLA_TAU)
    cqn = (_rms(z[:, E_CQ:E_CKV]) * qg_ref[...]).astype(BF16)
    qfull = (_dot(cqn, wqa_ref[...]) * _tile_lanes(cq_ref[...], MLA_HEADS)
             + _dot(cqn, wqb_ref[...]) * _tile_lanes(sq_ref[...], MLA_HEADS))
    qt_ref[0] = qfull.T.astype(BF16)
    ckvn = (_rms(z[:, E_CKV:E_MG]) * kvg_ref[...]).astype(BF16)
    kr = z[:, E_KRA:E_KRB] * ck_ref[...] + z[:, E_KRB:E_END] * sk_ref[...]
    kk_ref[0] = (_dot(ckvn, wk_ref[...]) + _tile_lanes(kr, MLA_HEADS)).astype(BF16)
    vt_ref[0] = _dot(ckvn, wv_ref[...]).T.astype(BF16)


def _even_pre(x, shift, scale, wts, tabs, tm):
    bsz, t, d = x.shape
    hq = MLA_HEADS * MLA_HEAD_PAD
    tok = lambda w: pl.BlockSpec((1, tm, w), lambda b, i: (b, i, 0))
    tab = pl.BlockSpec((tm, MLA_HEAD_PAD), lambda b, i: (i, 0))
    mod = pl.BlockSpec((1, 1, d), lambda b, i: (b, 0, 0))
    out_shape = (
        jax.ShapeDtypeStruct((bsz, t, GLA_QK_W), BF16),
        jax.ShapeDtypeStruct((bsz, t, GLA_QK_W), BF16),
        jax.ShapeDtypeStruct((bsz, t, GLA_V_W), BF16),
        jax.ShapeDtypeStruct((bsz, t, 2 * GLA_QK_W), F32),
        jax.ShapeDtypeStruct((bsz, t, GLA_V_W), BF16),
        jax.ShapeDtypeStruct((bsz, t, MLA_V_W), BF16),
        jax.ShapeDtypeStruct((bsz, hq, t), BF16),
        jax.ShapeDtypeStruct((bsz, t, hq), BF16),
        jax.ShapeDtypeStruct((bsz, MLA_V_W, t), BF16),
    )
    out_specs = (tok(GLA_QK_W), tok(GLA_QK_W), tok(GLA_V_W), tok(2 * GLA_QK_W), tok(GLA_V_W), tok(MLA_V_W),
                 pl.BlockSpec((1, hq, tm), lambda b, i: (b, 0, i)),
                 tok(hq),
                 pl.BlockSpec((1, MLA_V_W, tm), lambda b, i: (b, 0, i)))
    in_specs = [tok(d), mod, mod] + [_full(w.shape) for w in wts] + [tab] * 4
    return pl.pallas_call(
        _even_pre_kernel, grid=(bsz, t // tm), in_specs=in_specs, out_specs=out_specs, out_shape=out_shape,
        compiler_params=_cparams("arbitrary", "arbitrary"), name="even_pre",
    )(x, shift, scale, *wts, *tabs)


def _gla_chunk(q, k, v, g, s, tmat, mid, end, tri, head_masks, ones):
    g_hi = g.astype(BF16)
    g_lo = (g - g_hi.astype(F32)).astype(BF16)
    b = _dot(tmat, g_hi) + _dot(tmat, g_lo)
    b_mid = b[mid:mid + 1]
    b_end = b[end:end + 1]
    qe = q * jnp.exp(b - b_mid)
    ke = (k * jnp.exp(b_mid - b)).astype(BF16)
    kd = (k * jnp.exp(b_end - b)).astype(BF16)
    qb = q * jnp.exp(b)
    dec = jnp.exp(_dot_tn(g_hi, ones) + _dot_tn(g_lo, ones))
    stack = lambda a: jnp.concatenate([jnp.where(m, a, 0.0) for m in head_masks], axis=0).astype(BF16)
    att = _dot_nt(stack(qe), ke)
    o_inter = _dot(stack(qb), s.astype(BF16))
    outs, ds = [], []
    c = q.shape[0]
    for h in range(GLA_HEADS):
        a_h = jnp.where(tri, att[h * c:(h + 1) * c], 0.0).astype(BF16)
        v_h = v[:, h * GLA_DV:(h + 1) * GLA_DV]
        outs.append(_dot(a_h, v_h) + o_inter[h * c:(h + 1) * c])
        ds.append(_dot_tn(kd, v_h)[h * GLA_DK:(h + 1) * GLA_DK])
    return outs, dec * s + jnp.concatenate(ds, axis=0)


def _gla_kernel(qf_ref, kf_ref, vf_ref, gf_ref, qb_ref, kb_ref, vb_ref, gb_ref, s0_ref,
                of_ref, ob_ref, sfin_ref, s_scr, *, nchunk):
    i = pl.program_id(1)

    @pl.when(i == 0)
    def _():
        s_scr[...] = s0_ref[0]

    c = GLA_CHUNK
    row = lax.broadcasted_iota(jnp.int32, (c, c), 0)
    col = lax.broadcasted_iota(jnp.int32, (c, c), 1)
    lower = col <= row
    upper = col >= row
    lmat = jnp.where(lower, 1.0, 0.0).astype(BF16)
    umat = jnp.where(upper, 1.0, 0.0).astype(BF16)
    ones = jnp.ones((c, GLA_DV), BF16)
    lane_head = lax.broadcasted_iota(jnp.int32, (c, GLA_QK_W), 1) // GLA_DK
    head_masks = [lane_head == h for h in range(GLA_HEADS)]

    s_f = s_scr[0]
    for j in range(nchunk):
        r = slice(j * c, (j + 1) * c)
        outs, s_f = _gla_chunk(qf_ref[0, r, :].astype(F32), kf_ref[0, r, :].astype(F32), vf_ref[0, r, :],
                               gf_ref[0, r, :], s_f, lmat, c // 2 - 1, c - 1, lower, head_masks, ones)
        for h in range(GLA_HEADS):
            of_ref[0, r, h * GLA_DV:(h + 1) * GLA_DV] = outs[h]
    s_scr[0] = s_f

    s_b = s_scr[1]
    for j in reversed(range(nchunk)):
        r = slice(j * c, (j + 1) * c)
        outs, s_b = _gla_chunk(qb_ref[0, r, :].astype(F32), kb_ref[0, r, :].astype(F32), vb_ref[0, r, :],
                               gb_ref[0, r, :], s_b, umat, c // 2, 0, upper, head_masks, ones)
        for h in range(GLA_HEADS):
            ob_ref[0, r, h * GLA_DV:(h + 1) * GLA_DV] = outs[h]
    s_scr[1] = s_b

    @pl.when(i == pl.num_programs(1) - 1)
    def _():
        sfin_ref[0] = s_scr[...]


def _gla(q, k, v, g, s0, tb):
    bsz, t, _ = q.shape
    nblk = t // tb
    fwd = lambda w: pl.BlockSpec((1, tb, w), lambda b, i: (b, i, 0))
    bwd = lambda w: pl.BlockSpec((1, tb, w), lambda b, i: (b, nblk - 1 - i, 0))
    st = pl.BlockSpec((1, 2, GLA_QK_W, GLA_DV), lambda b, i: (b, 0, 0, 0))
    return pl.pallas_call(
        functools.partial(_gla_kernel, nchunk=tb // GLA_CHUNK),
        grid=(bsz, nblk),
        in_specs=[fwd(GLA_QK_W), fwd(GLA_QK_W), fwd(GLA_V_W),
                  pl.BlockSpec((1, tb, GLA_QK_W), lambda b, i: (b, i, 0)),
                  bwd(GLA_QK_W), bwd(GLA_QK_W), bwd(GLA_V_W),
                  pl.BlockSpec((1, tb, GLA_QK_W), lambda b, i: (b, nblk - 1 - i, 1)),
                  st],
        out_specs=(fwd(GLA_V_W), bwd(GLA_V_W), st),
        out_shape=(jax.ShapeDtypeStruct((bsz, t, GLA_V_W), F32),
                   jax.ShapeDtypeStruct((bsz, t, GLA_V_W), F32),
                   jax.ShapeDtypeStruct((bsz, 2, GLA_QK_W, GLA_DV), F32)),
        scratch_shapes=[pltpu.VMEM((2, GLA_QK_W, GLA_DV), F32)],
        compiler_params=_cparams("arbitrary", "arbitrary"), name="gla_scan",
    )(q, k, v, g, q, k, v, g, s0)


def _mla_kernel(qt_ref, kc_ref, vtc_ref, *rest, n_lat, tk):
    if n_lat:
        kl_ref, vtl_ref, o_ref = rest
    else:
        (o_ref,) = rest
    q_t = qt_ref[0]
    tq = q_t.shape[1]

    def step(kblk, vtblk, carry):
        m, l, acc = carry
        s_t = _dot(kblk, q_t)
        m_new = jnp.maximum(m, jnp.max(s_t, axis=0, keepdims=True))
        alpha = jnp.exp2(m - m_new)
        p = jnp.exp2(s_t - m_new)
        l = alpha * l + jnp.sum(p, axis=0, keepdims=True)
        acc = alpha * acc + _dot(vtblk, p.astype(BF16))
        return m_new, l, acc

    carry = (jnp.full((1, tq), NEG_BIG, F32), jnp.zeros((1, tq), F32), jnp.zeros((MLA_V, tq), F32))
    carry = step(kc_ref[0], vtc_ref[0], carry)
    if n_lat:
        def body(j, carry):
            off = pl.multiple_of(j * tk, tk)
            return step(kl_ref[0, pl.ds(off, tk), :], vtl_ref[0, :, pl.ds(off, tk)], carry)
        carry = lax.fori_loop(0, n_lat // tk, body, carry)
    _, l, acc = carry
    o_ref[0] = acc / l


def _mla(qt, kc, vtc, kl, vtl, tq, tk):
    bsz, _, t = qt.shape
    lc = kc.shape[1]
    n_lat = 0 if kl is None else kl.shape[1]
    in_specs = [pl.BlockSpec((1, MLA_HEAD_PAD, tq), lambda b, h, i: (b, h, i)),
                pl.BlockSpec((1, lc, MLA_HEAD_PAD), lambda b, h, i: (b, 0, h)),
                pl.BlockSpec((1, MLA_V, lc), lambda b, h, i: (b, h, 0))]
    args = [qt, kc, vtc]
    if n_lat:
        in_specs += [pl.BlockSpec((1, n_lat, MLA_HEAD_PAD), lambda b, h, i: (b, 0, h)),
                     pl.BlockSpec((1, MLA_V, n_lat), lambda b, h, i: (b, h, 0))]
        args += [kl, vtl]
    return pl.pallas_call(
        functools.partial(_mla_kernel, n_lat=n_lat, tk=tk),
        grid=(bsz, MLA_HEADS, t // tq),
        in_specs=in_specs,
        out_specs=pl.BlockSpec((1, MLA_V, tq), lambda b, h, i: (b, h, i)),
        out_shape=jax.ShapeDtypeStruct((bsz, MLA_V_W, t), F32),
        compiler_params=_cparams("arbitrary", "arbitrary", "arbitrary"), name="mla_attn",
    )(*args)


def _post_tail(x, y, gate, lng, lnb):
    return _ln(DEEPNORM_ALPHA * x + gate * y) * lng + lnb


def _even_post_kernel(x_ref, of_ref, ob_ref, sgg_ref, ot_ref, smg_ref, gate_ref, wo_ref, gng_ref,
                      lng_ref, lnb_ref, o_ref):
    o = of_ref[0] + ob_ref[0]
    parts = [_rms(o[:, h * GLA_DV:(h + 1) * GLA_DV]) * gng_ref[...] for h in range(GLA_HEADS)]
    yg = jnp.concatenate(parts, axis=1) * sgg_ref[0].astype(F32)
    ym = ot_ref[0].T * smg_ref[0].astype(F32)
    y = _dot(yg.astype(BF16), wo_ref[0:GLA_V_W, :]) + _dot(ym.astype(BF16), wo_ref[GLA_V_W:, :])
    o_ref[0] = _post_tail(x_ref[0], y, gate_ref[0], lng_ref[...], lnb_ref[...])


def _even_post(x, o_f, o_b, sgg, o_t, smg, gate, wo, gng, lng, lnb, tm):
    bsz, t, d = x.shape
    tok = lambda w: pl.BlockSpec((1, tm, w), lambda b, i: (b, i, 0))
    return pl.pallas_call(
        _even_post_kernel, grid=(bsz, t // tm),
        in_specs=[tok(d), tok(GLA_V_W), tok(GLA_V_W), tok(GLA_V_W),
                  pl.BlockSpec((1, MLA_V_W, tm), lambda b, i: (b, 0, i)), tok(MLA_V_W),
                  pl.BlockSpec((1, 1, d), lambda b, i: (b, 0, 0)),
                  _full(wo.shape), _full(gng.shape), _full(lng.shape), _full(lnb.shape)],
        out_specs=tok(d), out_shape=jax.ShapeDtypeStruct((bsz, t, d), F32),
        compiler_params=_cparams("arbitrary", "arbitrary"), name="even_post",
    )(x, o_f, o_b, sgg, o_t, smg, gate, wo, gng, lng, lnb)


def _odd_pre_kernel(x_ref, shift_ref, scale_ref, w_ref, dft_ref, sw_ref, sb_ref, u_ref, sf_ref, ys_ref):
    h = _ln(x_ref[0]) * (1.0 + scale_ref[0]) + shift_ref[0]
    z = _dot(h.astype(BF16), w_ref[...])
    u_ref[0] = _dot(z[:, 0:FNET_W].astype(BF16), dft_ref[...]).astype(BF16)
    sf_ref[0] = _silu(z[:, FNET_W:2 * FNET_W]).astype(BF16)
    o0 = 2 * FNET_W
    tm = z.shape[0]
    for g in range(SGU_GROUPS):
        cs = slice(g * SGU_GROUP_CH, (g + 1) * SGU_GROUP_CH)
        ug = _gelu(z[:, o0 + g * SGU_GROUP_CH:o0 + (g + 1) * SGU_GROUP_CH])
        vg = _ln(_gelu(z[:, o0 + SGU_W + g * SGU_GROUP_CH:o0 + SGU_W + (g + 1) * SGU_GROUP_CH])).astype(BF16)
        sg = _silu(z[:, o0 + 2 * SGU_W + g * SGU_GROUP_CH:o0 + 2 * SGU_W + (g + 1) * SGU_GROUP_CH])
        for c in range(tm // SGU_CHUNK):
            rs = slice(c * SGU_CHUNK, (c + 1) * SGU_CHUNK)
            sv = _dot(sw_ref[g], vg[rs]) + sb_ref[g]
            ys_ref[0, rs, cs] = (ug[rs] * sv * sg[rs]).astype(BF16)


def _odd_pre(x, shift, scale, w, dft, sw, sb, tm):
    bsz, t, d = x.shape
    tok = lambda w_: pl.BlockSpec((1, tm, w_), lambda b, i: (b, i, 0))
    mod = pl.BlockSpec((1, 1, d), lambda b, i: (b, 0, 0))
    return pl.pallas_call(
        _odd_pre_kernel, grid=(bsz, t // tm),
        in_specs=[tok(d), mod, mod, _full(w.shape), _full(dft.shape), _full(sw.shape), _full(sb.shape)],
        out_specs=(tok(2 * FNET_W), tok(FNET_W), tok(SGU_W)),
        out_shape=(jax.ShapeDtypeStruct((bsz, t, 2 * FNET_W), BF16),
                   jax.ShapeDtypeStruct((bsz, t, FNET_W), BF16),
                   jax.ShapeDtypeStruct((bsz, t, SGU_W), BF16)),
        compiler_params=_cparams("arbitrary", "arbitrary"), name="odd_pre",
    )(x, shift, scale, w, dft, sw, sb)


FFT_T2_BLK = 8
FFT_P1_BLK = 8


def _fft1_kernel(u_ref, w1_ref, tc_ref, ts_ref, z_ref):
    pq = _dot(w1_ref[...], u_ref[0])
    n1 = FFT_N1
    w2 = 2 * FNET_W
    for j in range(FFT_T2_BLK):
        a_c = pq[0:n1, j * w2:j * w2 + FNET_W]
        b_c = pq[0:n1, j * w2 + FNET_W:(j + 1) * w2]
        a_s = pq[n1:, j * w2:j * w2 + FNET_W]
        b_s = pq[n1:, j * w2 + FNET_W:(j + 1) * w2]
        zr = a_c - b_s
        zi = -b_c - a_s
        tc = _tile_lanes(tc_ref[:, j * 128:(j + 1) * 128], FNET_W // 128)
        ts = _tile_lanes(ts_ref[:, j * 128:(j + 1) * 128], FNET_W // 128)
        z_ref[0, :, j * w2:j * w2 + FNET_W] = (zr * tc + zi * ts).astype(BF16)
        z_ref[0, :, j * w2 + FNET_W:(j + 1) * w2] = (zi * tc - zr * ts).astype(BF16)


def _fft2_kernel(z_ref, c2_ref, s2_ref, y_ref):
    for j in range(FFT_P1_BLK):
        zp = z_ref[0, j]
        y_ref[0, :, j, :] = _dot(c2_ref[...], zp[:, 0:FNET_W]) + _dot(s2_ref[...], zp[:, FNET_W:])


def _dft_tables(t):
    n1, n2 = FFT_N1, t // FFT_N1
    p1 = np.arange(n1, dtype=np.float64)
    a1 = 2.0 * np.pi * np.outer(p1, p1) / n1
    w1 = np.concatenate([np.cos(a1), np.sin(a1)], axis=0)
    at = 2.0 * np.pi * np.outer(p1, np.arange(n2, dtype=np.float64)) / t
    tc = np.repeat(np.cos(at)[:, :, None], 128, axis=2).reshape(n1, n2 * 128)
    ts = np.repeat(np.sin(at)[:, :, None], 128, axis=2).reshape(n1, n2 * 128)
    p2 = np.arange(n2, dtype=np.float64)
    a2 = 2.0 * np.pi * np.outer(p2, p2) / n2
    norm = 1.0 / math.sqrt(t * FNET_GROUP_CH)
    return (jnp.asarray(w1, F32).astype(BF16), jnp.asarray(tc, F32), jnp.asarray(ts, F32),
            jnp.asarray(np.cos(a2) * norm, F32).astype(BF16), jnp.asarray(np.sin(a2) * norm, F32).astype(BF16))


def _channel_dft_matrix():
    d = np.arange(FNET_GROUP_CH, dtype=np.float64)
    a = 2.0 * np.pi * np.outer(d, d) / FNET_GROUP_CH
    m = np.zeros((FNET_W, 2 * FNET_W))
    for g in range(FNET_GROUPS):
        r = slice(g * FNET_GROUP_CH, (g + 1) * FNET_GROUP_CH)
        m[r, g * FNET_GROUP_CH:(g + 1) * FNET_GROUP_CH] = np.cos(a)
        m[r, FNET_W + g * FNET_GROUP_CH:FNET_W + (g + 1) * FNET_GROUP_CH] = np.sin(a)
    return jnp.asarray(m, F32).astype(BF16)


def _fnet_long(u):
    bsz, t, w2 = u.shape
    n1, n2 = FFT_N1, t // FFT_N1
    w1, tc, ts, c2, s2 = _dft_tables(t)
    cols = FFT_T2_BLK * w2
    z = pl.pallas_call(
        _fft1_kernel, grid=(bsz, n2 // FFT_T2_BLK),
        in_specs=[pl.BlockSpec((1, n1, cols), lambda b, i: (b, 0, i)), _full(w1.shape),
                  pl.BlockSpec((n1, FFT_T2_BLK * 128), lambda b, i: (0, i)),
                  pl.BlockSpec((n1, FFT_T2_BLK * 128), lambda b, i: (0, i))],
        out_specs=pl.BlockSpec((1, n1, cols), lambda b, i: (b, 0, i)),
        out_shape=jax.ShapeDtypeStruct((bsz, n1, n2 * w2), BF16),
        compiler_params=_cparams("arbitrary", "arbitrary"), name="fnet_stage1",
    )(u.reshape(bsz, n1, n2 * w2), w1, tc, ts)
    y = pl.pallas_call(
        _fft2_kernel, grid=(bsz, n1 // FFT_P1_BLK),
        in_specs=[pl.BlockSpec((1, FFT_P1_BLK, n2, w2), lambda b, i: (b, i, 0, 0)),
                  _full(c2.shape), _full(s2.shape)],
        out_specs=pl.BlockSpec((1, n2, FFT_P1_BLK, FNET_W), lambda b, i: (b, 0, i, 0)),
        out_shape=jax.ShapeDtypeStruct((bsz, n2, n1, FNET_W), F32),
        compiler_params=_cparams("arbitrary", "arbitrary"), name="fnet_stage2",
    )(z.reshape(bsz, n1, n2, w2), c2, s2)
    return y.reshape(bsz, t, FNET_W)


def _fnet_short_kernel(u_ref, c_ref, s_ref, y_ref):
    u = u_ref[0]
    y_ref[0] = _dot(c_ref[...], u[:, 0:FNET_W]) - _dot(s_ref[...], u[:, FNET_W:])


def _fnet_short(u):
    bsz, t, w2 = u.shape
    p = np.arange(t, dtype=np.float64)
    a = 2.0 * np.pi * np.outer(p, p) / t
    norm = 1.0 / math.sqrt(t * FNET_GROUP_CH)
    c, s = jnp.asarray(np.cos(a) * norm, F32).astype(BF16), jnp.asarray(np.sin(a) * norm, F32).astype(BF16)
    return pl.pallas_call(
        _fnet_short_kernel, grid=(bsz,),
        in_specs=[pl.BlockSpec((1, t, w2), lambda b: (b, 0, 0)), _full(c.shape), _full(s.shape)],
        out_specs=pl.BlockSpec((1, t, FNET_W), lambda b: (b, 0, 0)),
        out_shape=jax.ShapeDtypeStruct((bsz, t, FNET_W), F32),
        compiler_params=_cparams("arbitrary"), name="fnet_short",
    )(u, c, s)


def _odd_post_kernel(x_ref, fr_ref, sf_ref, ys_ref, gate_ref, wo_ref, lng_ref, lnb_ref, o_ref):
    yf = (fr_ref[0] * sf_ref[0].astype(F32)).astype(BF16)
    y = _dot(yf, wo_ref[0:FNET_W, :]) + _dot(ys_ref[0], wo_ref[FNET_W:, :])
    o_ref[0] = _post_tail(x_ref[0], y, gate_ref[0], lng_ref[...], lnb_ref[...])


def _odd_post(x, fr, sf, ys, gate, wo, lng, lnb, tm):
    bsz, t, d = x.shape
    tok = lambda w: pl.BlockSpec((1, tm, w), lambda b, i: (b, i, 0))
    return pl.pallas_call(
        _odd_post_kernel, grid=(bsz, t // tm),
        in_specs=[tok(d), tok(FNET_W), tok(FNET_W), tok(SGU_W),
                  pl.BlockSpec((1, 1, d), lambda b, i: (b, 0, 0)),
                  _full(wo.shape), _full(lng.shape), _full(lnb.shape)],
        out_specs=tok(d), out_shape=jax.ShapeDtypeStruct((bsz, t, d), F32),
        compiler_params=_cparams("arbitrary", "arbitrary"), name="odd_post",
    )(x, fr, sf, ys, gate, wo, lng, lnb)


def _even_weights(w_in, gla_w2, gla_b, q_norm_g, w_uq, kv_norm_g, w_ukv):
    d = w_in.shape[0]
    idx = np.cumsum(EVEN_IN_SIZES)[:-1].tolist()
    gq, gk, gv, glr, gg, cq, ckv, kr, mg = jnp.split(w_in, idx, axis=1)
    half = MLA_ROPE // 2
    z = lambda n: jnp.zeros((d, n), w_in.dtype)
    lr_pad = jnp.concatenate([glr, z(128 - 2 * GLA_GATE_RANK)], axis=1)
    kra = jnp.concatenate([z(MLA_NOPE), kr, z(MLA_HEAD_PAD - MLA_NOPE - MLA_ROPE)], axis=1)
    krb = jnp.concatenate([z(MLA_NOPE), kr[:, half:], kr[:, :half], z(MLA_HEAD_PAD - MLA_NOPE - MLA_ROPE)], axis=1)
    w = jnp.concatenate([gq * GLA_DK ** -0.5, gk, gv, gg, cq, ckv, mg, lr_pad, kra, krb], axis=1).astype(BF16)
    w2 = jnp.zeros((128, 2 * GLA_QK_W), F32)
    w2 = w2.at[0:GLA_GATE_RANK, 0:GLA_QK_W].set(gla_w2[0])
    w2 = w2.at[GLA_GATE_RANK:2 * GLA_GATE_RANK, GLA_QK_W:].set(gla_w2[1])
    gb = jnp.concatenate([gla_b[0], gla_b[1]])[None, :]
    uq = w_uq.reshape(MLA_Q_RANK, MLA_HEADS, MLA_NOPE + MLA_ROPE)
    pad = MLA_HEAD_PAD - MLA_NOPE - MLA_ROPE
    zq = lambda n: jnp.zeros((MLA_Q_RANK, MLA_HEADS, n), w_uq.dtype)
    wqa = jnp.concatenate([uq, zq(pad)], axis=2).reshape(MLA_Q_RANK, -1)
    wqb = jnp.concatenate([zq(MLA_NOPE), uq[:, :, MLA_NOPE + half:], uq[:, :, MLA_NOPE:MLA_NOPE + half], zq(pad)],
                          axis=2).reshape(MLA_Q_RANK, -1)
    ukv = w_ukv.reshape(MLA_KV_RANK, MLA_HEADS, MLA_NOPE + MLA_V)
    wk = jnp.concatenate([ukv[:, :, :MLA_NOPE], jnp.zeros((MLA_KV_RANK, MLA_HEADS, MLA_HEAD_PAD - MLA_NOPE), w_ukv.dtype)],
                         axis=2).reshape(MLA_KV_RANK, -1)
    wv = ukv[:, :, MLA_NOPE:].reshape(MLA_KV_RANK, -1)
    return (w, w2.astype(BF16), gb, q_norm_g[None, :], wqa.astype(BF16), wqb.astype(BF16),
            kv_norm_g[None, :], wk.astype(BF16), wv.astype(BF16))


def _rope_tables(n):
    row = jnp.repeat(jnp.arange(n // GRID_W, dtype=F32), GRID_W)
    col = (jnp.arange(n) % GRID_W).astype(F32)
    n_freq = MLA_ROPE // 4
    inv = ROPE_BASE ** (-jnp.arange(n_freq, dtype=F32) / n_freq)
    ang = jnp.concatenate([row[:, None] * inv, col[:, None] * inv], -1)
    cos, sin = jnp.cos(ang), jnp.sin(ang)
    pad = jnp.zeros((n, MLA_HEAD_PAD - MLA_NOPE - MLA_ROPE), F32)
    zn = jnp.zeros((n, MLA_NOPE), F32)
    qs = MLA_SCALE * LOG2E
    cq = jnp.concatenate([jnp.full((n, MLA_NOPE), qs, F32), cos * qs, cos * qs, pad], axis=1)
    sq = jnp.concatenate([zn, -sin * qs, sin * qs, pad], axis=1)
    ck = jnp.concatenate([zn, cos, cos, pad], axis=1)
    sk = jnp.concatenate([zn, -sin, sin, pad], axis=1)
    return cq, sq, ck, sk


def _plain_tables(n):
    pad = jnp.zeros((n, MLA_HEAD_PAD - MLA_NOPE - MLA_ROPE), F32)
    zn = jnp.zeros((n, MLA_NOPE), F32)
    zero = jnp.zeros((n, MLA_HEAD_PAD), F32)
    qs = MLA_SCALE * LOG2E
    cq = jnp.concatenate([jnp.full((n, MLA_NOPE + MLA_ROPE), qs, F32), pad], axis=1)
    ck = jnp.concatenate([zn, jnp.ones((n, MLA_ROPE), F32), pad], axis=1)
    return cq, zero, ck, zero


def _pick(t, pref):
    return pref if t % pref == 0 else t


def kernel(x, c, ctx, c_ctx, ada_w, ada_b, post_ln_g, post_ln_b, even_w_in, gla_w2, gla_b, gla_norm_g,
           mla_q_norm_g, mla_w_uq, mla_kv_norm_g, mla_w_ukv, even_w_out, odd_w_in, sgu_w, sgu_b, odd_w_out):
    bsz, n, d = x.shape
    lc = ctx.shape[1]
    depth = ada_w.shape[0]
    assert bsz + 1 <= 8 and n % (FFT_N1 * FFT_T2_BLK) == 0 and n % 512 == 0 and lc % 128 == 0

    cond = jnp.concatenate([c, c_ctx[None, :], jnp.zeros((8 - bsz - 1, d), F32)], axis=0)
    mods = _mods(cond, ada_w, ada_b)

    def lat_mod(l, j):
        return mods[l, :bsz, j * d:(j + 1) * d][:, None, :]

    def ctx_mod(l, j):
        return jnp.broadcast_to(mods[l, bsz, j * d:(j + 1) * d][None, None, :], (bsz, 1, d))

    rope_tabs = _rope_tables(n)
    ctx_tabs = _plain_tables(lc)
    dft_c = _channel_dft_matrix()
    tm_lat, tm_ctx = _pick(n, 512), _pick(lc, 256)
    zero_state = jnp.zeros((bsz, 2, GLA_QK_W, GLA_DV), F32)

    for l in range(depth):
        need_ctx_out = any(j % 2 == 0 for j in range(l + 1, depth))
        i = l // 2
        lng, lnb = post_ln_g[l][None, :], post_ln_b[l][None, :]
        if l % 2 == 0:
            wts = _even_weights(even_w_in[i], gla_w2[i], gla_b[i], mla_q_norm_g[i], mla_w_uq[i],
                                mla_kv_norm_g[i], mla_w_ukv[i])
            wo = even_w_out[i].astype(BF16)
            gng = gla_norm_g[i][None, :]
            pc = _even_pre(ctx, ctx_mod(l, 0), ctx_mod(l, 1), wts, ctx_tabs, tm_ctx)
            pz = _even_pre(x, lat_mod(l, 0), lat_mod(l, 1), wts, rope_tabs, tm_lat)
            q_c, k_c, v_c, g_c, sgg_c, smg_c, qt_c, kk_c, vt_c = pc
            q_l, k_l, v_l, g_l, sgg_l, smg_l, qt_l, kk_l, vt_l = pz
            of_c, ob_c, s_c = _gla(q_c, k_c, v_c, g_c, zero_state, _pick(lc, 256))
            of_l, ob_l, _ = _gla(q_l, k_l, v_l, g_l, s_c, _pick(n, 256))
            ot_l = _mla(qt_l, kk_c, vt_c, kk_l, vt_l, _pick(n, 512), _pick(n, 512))
            x_new = _even_post(x, of_l, ob_l, sgg_l, ot_l, smg_l, lat_mod(l, 2), wo, gng, lng, lnb, tm_lat)
            if need_ctx_out:
                ot_c = _mla(qt_c, kk_c, vt_c, None, None, lc, lc)
                ctx = _even_post(ctx, of_c, ob_c, sgg_c, ot_c, smg_c, ctx_mod(l, 2), wo, gng, lng, lnb, tm_ctx)
            x = x_new
        else:
            w = odd_w_in[i].astype(BF16)
            wo = odd_w_out[i].astype(BF16)
            sw = sgu_w[i].astype(BF16)
            sb = jnp.broadcast_to(sgu_b[i][:, :, None], (SGU_GROUPS, SGU_CHUNK, SGU_GROUP_CH))
            u, sf, ys = _odd_pre(x, lat_mod(l, 0), lat_mod(l, 1), w, dft_c, sw, sb, tm_lat)
            x_new = _odd_post(x, _fnet_long(u), sf, ys, lat_mod(l, 2), wo, lng, lnb, tm_lat)
            if need_ctx_out:
                u, sf, ys = _odd_pre(ctx, ctx_mod(l, 0), ctx_mod(l, 1), w, dft_c, sw, sb, tm_ctx)
                ctx = _odd_post(ctx, _fnet_short(u), sf, ys, ctx_mod(l, 2), wo, lng, lnb, tm_ctx)
            x = x_new
    return x
```

```python
import functools
import math

import numpy as np
import jax
import jax.numpy as jnp
from jax import lax
from jax.experimental import pallas as pl
from jax.experimental.pallas import tpu as pltpu

F32 = jnp.float32
BF16 = jnp.bfloat16

DEPTH = 4
GRID_W = 64
DEEPNORM_ALPHA = (2 * DEPTH) ** 0.25
LN_EPS = 1e-6

GLA_HEADS = 4
GLA_DK = 64
GLA_DV = 128
GLA_QK_W = GLA_HEADS * GLA_DK
GLA_V_W = GLA_HEADS * GLA_DV
GLA_GATE_RANK = 16
GLA_TAU = 16.0
GLA_CHUNK = 64

MLA_HEADS = 8
MLA_NOPE = 64
MLA_ROPE = 32
MLA_V = 64
MLA_Q_RANK = 256
MLA_KV_RANK = 128
MLA_V_W = MLA_HEADS * MLA_V
MLA_SCALE = (MLA_NOPE + MLA_ROPE) ** -0.5
ROPE_BASE = 10000.0
MLA_HEAD_PAD = 128
MLA_VT_ROWS = MLA_V + 16
MLA_VT_W = MLA_HEADS * MLA_VT_ROWS

FNET_GROUPS = 4
FNET_GROUP_CH = 128
FNET_W = FNET_GROUPS * FNET_GROUP_CH
FFT_N1 = 64

SGU_GROUPS = 4
SGU_GROUP_CH = 128
SGU_W = SGU_GROUPS * SGU_GROUP_CH
SGU_CHUNK = 128

EVEN_IN_SIZES = (GLA_QK_W, GLA_QK_W, GLA_V_W, 2 * GLA_GATE_RANK, GLA_V_W,
                 MLA_Q_RANK, MLA_KV_RANK, MLA_ROPE, MLA_V_W)

E_GQ, E_GK, E_GV, E_GG, E_CQ, E_CKV, E_MG, E_LR, E_KRA, E_KRB, E_END = (
    0, 256, 512, 1024, 1536, 1792, 1920, 2432, 2560, 2688, 2816)

VMEM_LIMIT_BYTES = 56 * 1024 * 1024
LOG2E = math.log2(math.e)
NEG_BIG = -1e30


def _cparams(*sem):
    return pltpu.CompilerParams(dimension_semantics=sem, vmem_limit_bytes=VMEM_LIMIT_BYTES)


def _dot(a, b):
    return jnp.dot(a, b, preferred_element_type=F32)


def _dot_nt(a, b):
    return lax.dot_general(a, b, (((1,), (1,)), ((), ())), preferred_element_type=F32)


def _dot_tn(a, b):
    return lax.dot_general(a, b, (((0,), (0,)), ((), ())), preferred_element_type=F32)


def _ln(x):
    xc = x - jnp.mean(x, -1, keepdims=True)
    return xc * lax.rsqrt(jnp.mean(xc * xc, -1, keepdims=True) + LN_EPS)


def _rms(x):
    return x * lax.rsqrt(jnp.mean(x * x, -1, keepdims=True) + LN_EPS)


def _silu(x):
    return x / (1.0 + jnp.exp(-x))


def _gelu(x):
    return 0.5 * x * (1.0 + lax.erf(x * (2.0 ** -0.5)))


def _tile_lanes(x, reps):
    return jnp.concatenate([x] * reps, axis=1)


def _full(shape):
    n = len(shape)
    return pl.BlockSpec(shape, lambda *_: (0,) * n)


def _mods_kernel(cond_ref, w_ref, b_ref, o_ref):
    s = _silu(cond_ref[...])
    o_ref[0] = jnp.dot(s, w_ref[0], preferred_element_type=F32,
                       precision=lax.Precision.HIGHEST) + b_ref[0]


def _mods(cond, ada_w, ada_b):
    depth, d, d3 = ada_w.shape
    nb = d3 // d
    return pl.pallas_call(
        _mods_kernel,
        grid=(depth, nb),
        in_specs=[pl.BlockSpec((8, d), lambda l, j: (0, 0)),
                  pl.BlockSpec((1, d, d), lambda l, j: (l, 0, j)),
                  pl.BlockSpec((1, 1, d), lambda l, j: (l, 0, j))],
        out_specs=pl.BlockSpec((1, 8, d), lambda l, j: (l, 0, j)),
        out_shape=jax.ShapeDtypeStruct((depth, 8, d3), F32),
        compiler_params=_cparams("arbitrary", "arbitrary"),
        name="ada_mod",
    )(cond, ada_w, ada_b.reshape(depth, 1, d3))


def _even_pre_kernel(x_ref, shift_ref, scale_ref, w_ref, w2_ref, gb_ref, qg_ref, wqa_ref, wqb_ref,
                     kvg_ref, wk_ref, wv_ref, vb_ref, cq_ref, sq_ref, ck_ref, sk_ref,
                     q_ref, k_ref, v_ref, g_ref, sgg_ref, smg_ref, qt_ref, kk_ref, vt_ref):
    h = _ln(x_ref[0]) * (1.0 + scale_ref[0]) + shift_ref[0]
    z = _dot(h.astype(BF16), w_ref[...])
    q_ref[0] = z[:, E_GQ:E_GK].astype(BF16)
    k_ref[0] = z[:, E_GK:E_GV].astype(BF16)
    v_ref[0] = z[:, E_GV:E_GG].astype(BF16)
    sgg_ref[0] = _silu(z[:, E_GG:E_CQ]).astype(BF16)
    smg_ref[0] = _silu(z[:, E_MG:E_LR]).astype(BF16)
    pre = _dot(z[:, E_LR:E_KRA].astype(BF16), w2_ref[...]) + gb_ref[...]
    g_ref[0] = jax.nn.log_sigmoid(pre) * (1.0 / GLA_TAU)
    cqn = (_rms(z[:, E_CQ:E_CKV]) * qg_ref[...]).astype(BF16)
    qfull = (_dot(cqn, wqa_ref[...]) * _tile_lanes(cq_ref[...], MLA_HEADS)
             + _dot(cqn, wqb_ref[...]) * _tile_lanes(sq_ref[...], MLA_HEADS))
    qt_ref[0] = qfull.T.astype(BF16)
    ckvn = (_rms(z[:, E_CKV:E_MG]) * kvg_ref[...]).astype(BF16)
    kr = z[:, E_KRA:E_KRB] * ck_ref[...] + z[:, E_KRB:E_END] * sk_ref[...]
    kk_ref[0] = (_dot(ckvn, wk_ref[...]) + _tile_lanes(kr, MLA_HEADS)).astype(BF16)
    vt_ref[0] = (_dot(ckvn, wv_ref[...]) + vb_ref[...]).T.astype(BF16)


def _even_pre(x, shift, scale, wts, tabs, tm):
    bsz, t, d = x.shape
    hq = MLA_HEADS * MLA_HEAD_PAD
    tok = lambda w: pl.BlockSpec((1, tm, w), lambda b, i: (b, i, 0))
    tab = pl.BlockSpec((tm, MLA_HEAD_PAD), lambda b, i: (i, 0))
    mod = pl.BlockSpec((1, 1, d), lambda b, i: (b, 0, 0))
    out_shape = (
        jax.ShapeDtypeStruct((bsz, t, GLA_QK_W), BF16),
        jax.ShapeDtypeStruct((bsz, t, GLA_QK_W), BF16),
        jax.ShapeDtypeStruct((bsz, t, GLA_V_W), BF16),
        jax.ShapeDtypeStruct((bsz, t, 2 * GLA_QK_W), F32),
        jax.ShapeDtypeStruct((bsz, t, GLA_V_W), BF16),
        jax.ShapeDtypeStruct((bsz, t, MLA_V_W), BF16),
        jax.ShapeDtypeStruct((bsz, hq, t), BF16),
        jax.ShapeDtypeStruct((bsz, t, hq), BF16),
        jax.ShapeDtypeStruct((bsz, MLA_VT_W, t), BF16),
    )
    out_specs = (tok(GLA_QK_W), tok(GLA_QK_W), tok(GLA_V_W), tok(2 * GLA_QK_W), tok(GLA_V_W), tok(MLA_V_W),
                 pl.BlockSpec((1, hq, tm), lambda b, i: (b, 0, i)),
                 tok(hq),
                 pl.BlockSpec((1, MLA_VT_W, tm), lambda b, i: (b, 0, i)))
    in_specs = [tok(d), mod, mod] + [_full(w.shape) for w in wts] + [tab] * 4
    return pl.pallas_call(
        _even_pre_kernel, grid=(bsz, t // tm), in_specs=in_specs, out_specs=out_specs, out_shape=out_shape,
        compiler_params=_cparams("arbitrary", "arbitrary"), name="even_pre",
    )(x, shift, scale, *wts, *tabs)


def _gla_chunk(q, k, v, g, s, tmat, mid, end, tri, head_masks, ones):
    g_hi = g.astype(BF16)
    g_lo = (g - g_hi.astype(F32)).astype(BF16)
    b = _dot(tmat, g_hi) + _dot(tmat, g_lo)
    b_mid = b[mid:mid + 1]
    b_end = b[end:end + 1]
    qe = q * jnp.exp(b - b_mid)
    ke = (k * jnp.exp(b_mid - b)).astype(BF16)
    kd = (k * jnp.exp(b_end - b)).astype(BF16)
    qb = q * jnp.exp(b)
    dec = jnp.exp(_dot_tn(g_hi, ones) + _dot_tn(g_lo, ones))
    stack = lambda a: jnp.concatenate([jnp.where(m, a, 0.0) for m in head_masks], axis=0).astype(BF16)
    att = _dot_nt(stack(qe), ke)
    o_inter = _dot(stack(qb), s.astype(BF16))
    outs, ds = [], []
    c = q.shape[0]
    for h in range(GLA_HEADS):
        a_h = jnp.where(tri, att[h * c:(h + 1) * c], 0.0).astype(BF16)
        v_h = v[:, h * GLA_DV:(h + 1) * GLA_DV]
        outs.append(_dot(a_h, v_h) + o_inter[h * c:(h + 1) * c])
        ds.append(_dot_tn(kd, v_h)[h * GLA_DK:(h + 1) * GLA_DK])
    return outs, dec * s + jnp.concatenate(ds, axis=0)


def _gla_kernel(qf_ref, kf_ref, vf_ref, gf_ref, qb_ref, kb_ref, vb_ref, gb_ref, s0_ref,
                of_ref, ob_ref, sfin_ref, s_scr, *, nchunk):
    i = pl.program_id(1)

    @pl.when(i == 0)
    def _():
        s_scr[...] = s0_ref[0]

    c = GLA_CHUNK
    row = lax.broadcasted_iota(jnp.int32, (c, c), 0)
    col = lax.broadcasted_iota(jnp.int32, (c, c), 1)
    lower = col <= row
    upper = col >= row
    lmat = jnp.where(lower, 1.0, 0.0).astype(BF16)
    umat = jnp.where(upper, 1.0, 0.0).astype(BF16)
    ones = jnp.ones((c, GLA_DV), BF16)
    lane_head = lax.broadcasted_iota(jnp.int32, (c, GLA_QK_W), 1) // GLA_DK
    head_masks = [lane_head == h for h in range(GLA_HEADS)]

    s_f = s_scr[0]
    for j in range(nchunk):
        r = slice(j * c, (j + 1) * c)
        outs, s_f = _gla_chunk(qf_ref[0, r, :].astype(F32), kf_ref[0, r, :].astype(F32), vf_ref[0, r, :],
                               gf_ref[0, r, :], s_f, lmat, c // 2 - 1, c - 1, lower, head_masks, ones)
        for h in range(GLA_HEADS):
            of_ref[0, r, h * GLA_DV:(h + 1) * GLA_DV] = outs[h]
    s_scr[0] = s_f

    s_b = s_scr[1]
    for j in reversed(range(nchunk)):
        r = slice(j * c, (j + 1) * c)
        outs, s_b = _gla_chunk(qb_ref[0, r, :].astype(F32), kb_ref[0, r, :].astype(F32), vb_ref[0, r, :],
                               gb_ref[0, r, :], s_b, umat, c // 2, 0, upper, head_masks, ones)
        for h in range(GLA_HEADS):
            ob_ref[0, r, h * GLA_DV:(h + 1) * GLA_DV] = outs[h]
    s_scr[1] = s_b

    @pl.when(i == pl.num_programs(1) - 1)
    def _():
        sfin_ref[0] = s_scr[...]


def _gla(q, k, v, g, s0, tb):
    bsz, t, _ = q.shape
    nblk = t // tb
    fwd = lambda w: pl.BlockSpec((1, tb, w), lambda b, i: (b, i, 0))
    bwd = lambda w: pl.BlockSpec((1, tb, w), lambda b, i: (b, nblk - 1 - i, 0))
    st = pl.BlockSpec((1, 2, GLA_QK_W, GLA_DV), lambda b, i: (b, 0, 0, 0))
    return pl.pallas_call(
        functools.partial(_gla_kernel, nchunk=tb // GLA_CHUNK),
        grid=(bsz, nblk),
        in_specs=[fwd(GLA_QK_W), fwd(GLA_QK_W), fwd(GLA_V_W),
                  pl.BlockSpec((1, tb, GLA_QK_W), lambda b, i: (b, i, 0)),
                  bwd(GLA_QK_W), bwd(GLA_QK_W), bwd(GLA_V_W),
                  pl.BlockSpec((1, tb, GLA_QK_W), lambda b, i: (b, nblk - 1 - i, 1)),
                  st],
        out_specs=(fwd(GLA_V_W), bwd(GLA_V_W), st),
        out_shape=(jax.ShapeDtypeStruct((bsz, t, GLA_V_W), F32),
                   jax.ShapeDtypeStruct((bsz, t, GLA_V_W), F32),
                   jax.ShapeDtypeStruct((bsz, 2, GLA_QK_W, GLA_DV), F32)),
        scratch_shapes=[pltpu.VMEM((2, GLA_QK_W, GLA_DV), F32)],
        compiler_params=_cparams("arbitrary", "arbitrary"), name="gla_scan",
    )(q, k, v, g, q, k, v, g, s0)


MLA_SLOTS = 3
MLA_TK = 512


def _mla_kernel(qt_ref, k_ref, vt_ref, o_ref, *scratch, n_chunks, tk):
    q_t = qt_ref[0]
    tq = q_t.shape[1]
    s_bufs, p_bufs = scratch[:MLA_SLOTS], scratch[MLA_SLOTS:]

    def stage(t, slot, carry):
        t_static = t if isinstance(t, int) else None
        do_qk = t_static is None or t_static < n_chunks
        do_exp = t_static is None or 1 <= t_static <= n_chunks
        do_pv = t_static is None or t_static >= 2
        cmax, m, alpha_prev, acc = carry
        m_new, alpha_new = m, alpha_prev
        if do_exp:
            e = (slot - 1) % MLA_SLOTS
            m_new = jnp.maximum(m, jnp.max(cmax, axis=0, keepdims=True))
            alpha_new = jnp.exp2(m - m_new)
            p_bufs[e][...] = jnp.exp2(s_bufs[e][...] - m_new).astype(BF16)
        if do_qk:
            s_t = _dot(k_ref[0, pl.ds(pl.multiple_of(t * tk, tk), tk), :], q_t)
            s_bufs[slot][...] = s_t
            cmax = jnp.max(s_t.reshape(tk // 8, 8, tq), axis=0)
        if do_pv:
            v = (slot - 2) % MLA_SLOTS
            vt = vt_ref[0, :, pl.ds(pl.multiple_of((t - 2) * tk, tk), tk)]
            acc = alpha_prev * acc + _dot(vt, p_bufs[v][...])
        return cmax, m_new, alpha_new, acc

    carry = (jnp.full((8, tq), NEG_BIG, F32), jnp.full((1, tq), NEG_BIG, F32),
             jnp.ones((1, tq), F32), jnp.zeros((MLA_VT_ROWS, tq), F32))
    t0 = 0
    while t0 < min(2, n_chunks) or (t0 < n_chunks and (n_chunks - t0) % MLA_SLOTS):
        carry = stage(t0, t0 % MLA_SLOTS, carry)
        t0 += 1
    if t0 < n_chunks:
        def trip(i, c):
            for u in range(MLA_SLOTS):
                c = stage(t0 + MLA_SLOTS * i + u, (t0 + u) % MLA_SLOTS, c)
            return c
        carry = lax.fori_loop(0, (n_chunks - t0) // MLA_SLOTS, trip, carry)
    for t in range(max(t0, n_chunks), n_chunks + 2):
        carry = stage(t, t % MLA_SLOTS, carry)
    acc = carry[3]
    o_ref[0] = acc[0:MLA_V] / acc[MLA_V:MLA_V + 1]


def _mla(qt, k, vt, tq, tk):
    bsz, _, t = qt.shape
    s = k.shape[1]
    assert s % tk == 0 and t % tq == 0
    return pl.pallas_call(
        functools.partial(_mla_kernel, n_chunks=s // tk, tk=tk),
        grid=(bsz, MLA_HEADS, t // tq),
        in_specs=[pl.BlockSpec((1, MLA_HEAD_PAD, tq), lambda b, h, i: (b, h, i)),
                  pl.BlockSpec((1, s, MLA_HEAD_PAD), lambda b, h, i: (b, 0, h)),
                  pl.BlockSpec((1, MLA_VT_ROWS, s), lambda b, h, i: (b, h, 0))],
        out_specs=pl.BlockSpec((1, MLA_V, tq), lambda b, h, i: (b, h, i)),
        out_shape=jax.ShapeDtypeStruct((bsz, MLA_V_W, t), F32),
        scratch_shapes=[pltpu.VMEM((tk, tq), F32)] * MLA_SLOTS + [pltpu.VMEM((tk, tq), BF16)] * MLA_SLOTS,
        compiler_params=_cparams("arbitrary", "arbitrary", "arbitrary"), name="mla_attn",
    )(qt, k, vt)


def _keys_with_context(kk_c, vt_c, kk_l, vt_l, tk):
    bsz, lc, _ = kk_c.shape
    npad = (-lc) % tk
    pad_k = jnp.tile(kk_c, (1, -(-npad // lc), 1))[:, :npad]
    k_all = jnp.concatenate([kk_c, pad_k, kk_l], axis=1)
    vt_all = jnp.concatenate([vt_c, jnp.zeros((bsz, MLA_VT_W, npad), vt_c.dtype), vt_l], axis=2)
    return k_all, vt_all


def _post_tail(x, y, gate, lng, lnb):
    return _ln(DEEPNORM_ALPHA * x + gate * y) * lng + lnb


def _even_post_kernel(x_ref, of_ref, ob_ref, sgg_ref, ot_ref, smg_ref, gate_ref, wo_ref, gng_ref,
                      lng_ref, lnb_ref, o_ref):
    o = of_ref[0] + ob_ref[0]
    parts = [_rms(o[:, h * GLA_DV:(h + 1) * GLA_DV]) * gng_ref[...] for h in range(GLA_HEADS)]
    yg = jnp.concatenate(parts, axis=1) * sgg_ref[0].astype(F32)
    ym = ot_ref[0].T * smg_ref[0].astype(F32)
    y = _dot(yg.astype(BF16), wo_ref[0:GLA_V_W, :]) + _dot(ym.astype(BF16), wo_ref[GLA_V_W:, :])
    o_ref[0] = _post_tail(x_ref[0], y, gate_ref[0], lng_ref[...], lnb_ref[...])


def _even_post(x, o_f, o_b, sgg, o_t, smg, gate, wo, gng, lng, lnb, tm):
    bsz, t, d = x.shape
    tok = lambda w: pl.BlockSpec((1, tm, w), lambda b, i: (b, i, 0))
    return pl.pallas_call(
        _even_post_kernel, grid=(bsz, t // tm),
        in_specs=[tok(d), tok(GLA_V_W), tok(GLA_V_W), tok(GLA_V_W),
                  pl.BlockSpec((1, MLA_V_W, tm), lambda b, i: (b, 0, i)), tok(MLA_V_W),
                  pl.BlockSpec((1, 1, d), lambda b, i: (b, 0, 0)),
                  _full(wo.shape), _full(gng.shape), _full(lng.shape), _full(lnb.shape)],
        out_specs=tok(d), out_shape=jax.ShapeDtypeStruct((bsz, t, d), F32),
        compiler_params=_cparams("arbitrary", "arbitrary"), name="even_post",
    )(x, o_f, o_b, sgg, o_t, smg, gate, wo, gng, lng, lnb)


def _odd_pre_kernel(x_ref, shift_ref, scale_ref, w_ref, dft_ref, sw_ref, sb_ref, u_ref, sf_ref, ys_ref):
    h = _ln(x_ref[0]) * (1.0 + scale_ref[0]) + shift_ref[0]
    z = _dot(h.astype(BF16), w_ref[...])
    u_ref[0] = _dot(z[:, 0:FNET_W].astype(BF16), dft_ref[...]).astype(BF16)
    sf_ref[0] = _silu(z[:, FNET_W:2 * FNET_W]).astype(BF16)
    o0 = 2 * FNET_W
    tm = z.shape[0]
    for g in range(SGU_GROUPS):
        cs = slice(g * SGU_GROUP_CH, (g + 1) * SGU_GROUP_CH)
        ug = _gelu(z[:, o0 + g * SGU_GROUP_CH:o0 + (g + 1) * SGU_GROUP_CH])
        vg = _ln(_gelu(z[:, o0 + SGU_W + g * SGU_GROUP_CH:o0 + SGU_W + (g + 1) * SGU_GROUP_CH])).astype(BF16)
        sg = _silu(z[:, o0 + 2 * SGU_W + g * SGU_GROUP_CH:o0 + 2 * SGU_W + (g + 1) * SGU_GROUP_CH])
        for c in range(tm // SGU_CHUNK):
            rs = slice(c * SGU_CHUNK, (c + 1) * SGU_CHUNK)
            sv = _dot(sw_ref[g], vg[rs]) + sb_ref[g]
            ys_ref[0, rs, cs] = (ug[rs] * sv * sg[rs]).astype(BF16)


def _odd_pre(x, shift, scale, w, dft, sw, sb, tm):
    bsz, t, d = x.shape
    tok = lambda w_: pl.BlockSpec((1, tm, w_), lambda b, i: (b, i, 0))
    mod = pl.BlockSpec((1, 1, d), lambda b, i: (b, 0, 0))
    return pl.pallas_call(
        _odd_pre_kernel, grid=(bsz, t // tm),
        in_specs=[tok(d), mod, mod, _full(w.shape), _full(dft.shape), _full(sw.shape), _full(sb.shape)],
        out_specs=(tok(2 * FNET_W), tok(FNET_W), tok(SGU_W)),
        out_shape=(jax.ShapeDtypeStruct((bsz, t, 2 * FNET_W), BF16),
                   jax.ShapeDtypeStruct((bsz, t, FNET_W), BF16),
                   jax.ShapeDtypeStruct((bsz, t, SGU_W), BF16)),
        compiler_params=_cparams("arbitrary", "arbitrary"), name="odd_pre",
    )(x, shift, scale, w, dft, sw, sb)


FFT_T2_BLK = 8
FFT_P1_BLK = 8


def _fft1_kernel(u_ref, w1_ref, tc_ref, ts_ref, z_ref):
    pq = _dot(w1_ref[...], u_ref[0])
    n1 = FFT_N1
    w2 = 2 * FNET_W
    for j in range(FFT_T2_BLK):
        a_c = pq[0:n1, j * w2:j * w2 + FNET_W]
        b_c = pq[0:n1, j * w2 + FNET_W:(j + 1) * w2]
        a_s = pq[n1:, j * w2:j * w2 + FNET_W]
        b_s = pq[n1:, j * w2 + FNET_W:(j + 1) * w2]
        zr = a_c - b_s
        zi = -b_c - a_s
        tc = _tile_lanes(tc_ref[:, j * 128:(j + 1) * 128], FNET_W // 128)
        ts = _tile_lanes(ts_ref[:, j * 128:(j + 1) * 128], FNET_W // 128)
        z_ref[0, :, j * w2:j * w2 + FNET_W] = (zr * tc + zi * ts).astype(BF16)
        z_ref[0, :, j * w2 + FNET_W:(j + 1) * w2] = (zi * tc - zr * ts).astype(BF16)


def _fft2_kernel(z_ref, c2_ref, s2_ref, y_ref):
    for j in range(FFT_P1_BLK):
        zp = z_ref[0, j]
        y_ref[0, :, j, :] = _dot(c2_ref[...], zp[:, 0:FNET_W]) + _dot(s2_ref[...], zp[:, FNET_W:])


def _dft_tables(t):
    n1, n2 = FFT_N1, t // FFT_N1
    p1 = np.arange(n1, dtype=np.float64)
    a1 = 2.0 * np.pi * np.outer(p1, p1) / n1
    w1 = np.concatenate([np.cos(a1), np.sin(a1)], axis=0)
    at = 2.0 * np.pi * np.outer(p1, np.arange(n2, dtype=np.float64)) / t
    tc = np.repeat(np.cos(at)[:, :, None], 128, axis=2).reshape(n1, n2 * 128)
    ts = np.repeat(np.sin(at)[:, :, None], 128, axis=2).reshape(n1, n2 * 128)
    p2 = np.arange(n2, dtype=np.float64)
    a2 = 2.0 * np.pi * np.outer(p2, p2) / n2
    norm = 1.0 / math.sqrt(t * FNET_GROUP_CH)
    return (jnp.asarray(w1, F32).astype(BF16), jnp.asarray(tc, F32), jnp.asarray(ts, F32),
            jnp.asarray(np.cos(a2) * norm, F32).astype(BF16), jnp.asarray(np.sin(a2) * norm, F32).astype(BF16))


def _channel_dft_matrix():
    d = np.arange(FNET_GROUP_CH, dtype=np.float64)
    a = 2.0 * np.pi * np.outer(d, d) / FNET_GROUP_CH
    m = np.zeros((FNET_W, 2 * FNET_W))
    for g in range(FNET_GROUPS):
        r = slice(g * FNET_GROUP_CH, (g + 1) * FNET_GROUP_CH)
        m[r, g * FNET_GROUP_CH:(g + 1) * FNET_GROUP_CH] = np.cos(a)
        m[r, FNET_W + g * FNET_GROUP_CH:FNET_W + (g + 1) * FNET_GROUP_CH] = np.sin(a)
    return jnp.asarray(m, F32).astype(BF16)


def _fnet_long(u):
    bsz, t, w2 = u.shape
    n1, n2 = FFT_N1, t // FFT_N1
    w1, tc, ts, c2, s2 = _dft_tables(t)
    cols = FFT_T2_BLK * w2
    z = pl.pallas_call(
        _fft1_kernel, grid=(bsz, n2 // FFT_T2_BLK),
        in_specs=[pl.BlockSpec((1, n1, cols), lambda b, i: (b, 0, i)), _full(w1.shape),
                  pl.BlockSpec((n1, FFT_T2_BLK * 128), lambda b, i: (0, i)),
                  pl.BlockSpec((n1, FFT_T2_BLK * 128), lambda b, i: (0, i))],
        out_specs=pl.BlockSpec((1, n1, cols), lambda b, i: (b, 0, i)),
        out_shape=jax.ShapeDtypeStruct((bsz, n1, n2 * w2), BF16),
        compiler_params=_cparams("arbitrary", "arbitrary"), name="fnet_stage1",
    )(u.reshape(bsz, n1, n2 * w2), w1, tc, ts)
    y = pl.pallas_call(
        _fft2_kernel, grid=(bsz, n1 // FFT_P1_BLK),
        in_specs=[pl.BlockSpec((1, FFT_P1_BLK, n2, w2), lambda b, i: (b, i, 0, 0)),
                  _full(c2.shape), _full(s2.shape)],
        out_specs=pl.BlockSpec((1, n2, FFT_P1_BLK, FNET_W), lambda b, i: (b, 0, i, 0)),
        out_shape=jax.ShapeDtypeStruct((bsz, n2, n1, FNET_W), F32),
        compiler_params=_cparams("arbitrary", "arbitrary"), name="fnet_stage2",
    )(z.reshape(bsz, n1, n2, w2), c2, s2)
    return y.reshape(bsz, t, FNET_W)


def _fnet_short_kernel(u_ref, c_ref, s_ref, y_ref):
    u = u_ref[0]
    y_ref[0] = _dot(c_ref[...], u[:, 0:FNET_W]) - _dot(s_ref[...], u[:, FNET_W:])


def _fnet_short(u):
    bsz, t, w2 = u.shape
    p = np.arange(t, dtype=np.float64)
    a = 2.0 * np.pi * np.outer(p, p) / t
    norm = 1.0 / math.sqrt(t * FNET_GROUP_CH)
    c, s = jnp.asarray(np.cos(a) * norm, F32).astype(BF16), jnp.asarray(np.sin(a) * norm, F32).astype(BF16)
    return pl.pallas_call(
        _fnet_short_kernel, grid=(bsz,),
        in_specs=[pl.BlockSpec((1, t, w2), lambda b: (b, 0, 0)), _full(c.shape), _full(s.shape)],
        out_specs=pl.BlockSpec((1, t, FNET_W), lambda b: (b, 0, 0)),
        out_shape=jax.ShapeDtypeStruct((bsz, t, FNET_W), F32),
        compiler_params=_cparams("arbitrary"), name="fnet_short",
    )(u, c, s)


def _odd_post_kernel(x_ref, fr_ref, sf_ref, ys_ref, gate_ref, wo_ref, lng_ref, lnb_ref, o_ref):
    yf = (fr_ref[0] * sf_ref[0].astype(F32)).astype(BF16)
    y = _dot(yf, wo_ref[0:FNET_W, :]) + _dot(ys_ref[0], wo_ref[FNET_W:, :])
    o_ref[0] = _post_tail(x_ref[0], y, gate_ref[0], lng_ref[...], lnb_ref[...])


def _odd_post(x, fr, sf, ys, gate, wo, lng, lnb, tm):
    bsz, t, d = x.shape
    tok = lambda w: pl.BlockSpec((1, tm, w), lambda b, i: (b, i, 0))
    return pl.pallas_call(
        _odd_post_kernel, grid=(bsz, t // tm),
        in_specs=[tok(d), tok(FNET_W), tok(FNET_W), tok(SGU_W),
                  pl.BlockSpec((1, 1, d), lambda b, i: (b, 0, 0)),
                  _full(wo.shape), _full(lng.shape), _full(lnb.shape)],
        out_specs=tok(d), out_shape=jax.ShapeDtypeStruct((bsz, t, d), F32),
        compiler_params=_cparams("arbitrary", "arbitrary"), name="odd_post",
    )(x, fr, sf, ys, gate, wo, lng, lnb)


def _even_weights(w_in, gla_w2, gla_b, q_norm_g, w_uq, kv_norm_g, w_ukv):
    d = w_in.shape[0]
    idx = np.cumsum(EVEN_IN_SIZES)[:-1].tolist()
    gq, gk, gv, glr, gg, cq, ckv, kr, mg = jnp.split(w_in, idx, axis=1)
    half = MLA_ROPE // 2
    z = lambda n: jnp.zeros((d, n), w_in.dtype)
    lr_pad = jnp.concatenate([glr, z(128 - 2 * GLA_GATE_RANK)], axis=1)
    kra = jnp.concatenate([z(MLA_NOPE), kr, z(MLA_HEAD_PAD - MLA_NOPE - MLA_ROPE)], axis=1)
    krb = jnp.concatenate([z(MLA_NOPE), kr[:, half:], kr[:, :half], z(MLA_HEAD_PAD - MLA_NOPE - MLA_ROPE)], axis=1)
    w = jnp.concatenate([gq * GLA_DK ** -0.5, gk, gv, gg, cq, ckv, mg, lr_pad, kra, krb], axis=1).astype(BF16)
    w2 = jnp.zeros((128, 2 * GLA_QK_W), F32)
    w2 = w2.at[0:GLA_GATE_RANK, 0:GLA_QK_W].set(gla_w2[0])
    w2 = w2.at[GLA_GATE_RANK:2 * GLA_GATE_RANK, GLA_QK_W:].set(gla_w2[1])
    gb = jnp.concatenate([gla_b[0], gla_b[1]])[None, :]
    uq = w_uq.reshape(MLA_Q_RANK, MLA_HEADS, MLA_NOPE + MLA_ROPE)
    pad = MLA_HEAD_PAD - MLA_NOPE - MLA_ROPE
    zq = lambda n: jnp.zeros((MLA_Q_RANK, MLA_HEADS, n), w_uq.dtype)
    wqa = jnp.concatenate([uq, zq(pad)], axis=2).reshape(MLA_Q_RANK, -1)
    wqb = jnp.concatenate([zq(MLA_NOPE), uq[:, :, MLA_NOPE + half:], uq[:, :, MLA_NOPE:MLA_NOPE + half], zq(pad)],
                          axis=2).reshape(MLA_Q_RANK, -1)
    ukv = w_ukv.reshape(MLA_KV_RANK, MLA_HEADS, MLA_NOPE + MLA_V)
    wk = jnp.concatenate([ukv[:, :, :MLA_NOPE], jnp.zeros((MLA_KV_RANK, MLA_HEADS, MLA_HEAD_PAD - MLA_NOPE), w_ukv.dtype)],
                         axis=2).reshape(MLA_KV_RANK, -1)
    vpad = MLA_VT_ROWS - MLA_V
    wv = jnp.concatenate([ukv[:, :, MLA_NOPE:], jnp.zeros((MLA_KV_RANK, MLA_HEADS, vpad), w_ukv.dtype)],
                         axis=2).reshape(MLA_KV_RANK, -1)
    vbias = np.zeros((MLA_HEADS, MLA_VT_ROWS), np.float32)
    vbias[:, MLA_V] = 1.0
    return (w, w2.astype(BF16), gb, q_norm_g[None, :], wqa.astype(BF16), wqb.astype(BF16),
            kv_norm_g[None, :], wk.astype(BF16), wv.astype(BF16), jnp.asarray(vbias.reshape(1, -1)))


def _rope_tables(n):
    row = jnp.repeat(jnp.arange(n // GRID_W, dtype=F32), GRID_W)
    col = (jnp.arange(n) % GRID_W).astype(F32)
    n_freq = MLA_ROPE // 4
    inv = ROPE_BASE ** (-jnp.arange(n_freq, dtype=F32) / n_freq)
    ang = jnp.concatenate([row[:, None] * inv, col[:, None] * inv], -1)
    cos, sin = jnp.cos(ang), jnp.sin(ang)
    pad = jnp.zeros((n, MLA_HEAD_PAD - MLA_NOPE - MLA_ROPE), F32)
    zn = jnp.zeros((n, MLA_NOPE), F32)
    qs = MLA_SCALE * LOG2E
    cq = jnp.concatenate([jnp.full((n, MLA_NOPE), qs, F32), cos * qs, cos * qs, pad], axis=1)
    sq = jnp.concatenate([zn, -sin * qs, sin * qs, pad], axis=1)
    ck = jnp.concatenate([zn, cos, cos, pad], axis=1)
    sk = jnp.concatenate([zn, -sin, sin, pad], axis=1)
    return cq, sq, ck, sk


def _plain_tables(n):
    pad = jnp.zeros((n, MLA_HEAD_PAD - MLA_NOPE - MLA_ROPE), F32)
    zn = jnp.zeros((n, MLA_NOPE), F32)
    zero = jnp.zeros((n, MLA_HEAD_PAD), F32)
    qs = MLA_SCALE * LOG2E
    cq = jnp.concatenate([jnp.full((n, MLA_NOPE + MLA_ROPE), qs, F32), pad], axis=1)
    ck = jnp.concatenate([zn, jnp.ones((n, MLA_ROPE), F32), pad], axis=1)
    return cq, zero, ck, zero


def _pick(t, pref):
    return pref if t % pref == 0 else t


def kernel(x, c, ctx, c_ctx, ada_w, ada_b, post_ln_g, post_ln_b, even_w_in, gla_w2, gla_b, gla_norm_g,
           mla_q_norm_g, mla_w_uq, mla_kv_norm_g, mla_w_ukv, even_w_out, odd_w_in, sgu_w, sgu_b, odd_w_out):
    bsz, n, d = x.shape
    lc = ctx.shape[1]
    depth = ada_w.shape[0]
    assert bsz + 1 <= 8 and n % (FFT_N1 * FFT_T2_BLK) == 0 and n % 512 == 0 and lc % 128 == 0

    cond = jnp.concatenate([c, c_ctx[None, :], jnp.zeros((8 - bsz - 1, d), F32)], axis=0)
    mods = _mods(cond, ada_w, ada_b)

    def lat_mod(l, j):
        return mods[l, :bsz, j * d:(j + 1) * d][:, None, :]

    def ctx_mod(l, j):
        return jnp.broadcast_to(mods[l, bsz, j * d:(j + 1) * d][None, None, :], (bsz, 1, d))

    rope_tabs = _rope_tables(n)
    ctx_tabs = _plain_tables(lc)
    dft_c = _channel_dft_matrix()
    tm_lat, tm_ctx = _pick(n, 512), _pick(lc, 256)
    zero_state = jnp.zeros((bsz, 2, GLA_QK_W, GLA_DV), F32)

    for l in range(depth):
        need_ctx_out = any(j % 2 == 0 for j in range(l + 1, depth))
        i = l // 2
        lng, lnb = post_ln_g[l][None, :], post_ln_b[l][None, :]
        if l % 2 == 0:
            wts = _even_weights(even_w_in[i], gla_w2[i], gla_b[i], mla_q_norm_g[i], mla_w_uq[i],
                                mla_kv_norm_g[i], mla_w_ukv[i])
            wo = even_w_out[i].astype(BF16)
            gng = gla_norm_g[i][None, :]
            pc = _even_pre(ctx, ctx_mod(l, 0), ctx_mod(l, 1), wts, ctx_tabs, tm_ctx)
            pz = _even_pre(x, lat_mod(l, 0), lat_mod(l, 1), wts, rope_tabs, tm_lat)
            q_c, k_c, v_c, g_c, sgg_c, smg_c, qt_c, kk_c, vt_c = pc
            q_l, k_l, v_l, g_l, sgg_l, smg_l, qt_l, kk_l, vt_l = pz
            of_c, ob_c, s_c = _gla(q_c, k_c, v_c, g_c, zero_state, _pick(lc, 256))
            of_l, ob_l, _ = _gla(q_l, k_l, v_l, g_l, s_c, _pick(n, 256))
            k_all, vt_all = _keys_with_context(kk_c, vt_c, kk_l, vt_l, MLA_TK)
            ot_l = _mla(qt_l, k_all, vt_all, _pick(n, 512), MLA_TK)
            x_new = _even_post(x, of_l, ob_l, sgg_l, ot_l, smg_l, lat_mod(l, 2), wo, gng, lng, lnb, tm_lat)
            if need_ctx_out:
                ot_c = _mla(qt_c, kk_c, vt_c, lc, lc)
                ctx = _even_post(ctx, of_c, ob_c, sgg_c, ot_c, smg_c, ctx_mod(l, 2), wo, gng, lng, lnb, tm_ctx)
            x = x_new
        else:
            w = odd_w_in[i].astype(BF16)
            wo = odd_w_out[i].astype(BF16)
            sw = sgu_w[i].astype(BF16)
            sb = jnp.broadcast_to(sgu_b[i][:, :, None], (SGU_GROUPS, SGU_CHUNK, SGU_GROUP_CH))
            u, sf, ys = _odd_pre(x, lat_mod(l, 0), lat_mod(l, 1), w, dft_c, sw, sb, tm_lat)
            x_new = _odd_post(x, _fnet_long(u), sf, ys, lat_mod(l, 2), wo, lng, lnb, tm_lat)
            if need_ctx_out:
                u, sf, ys = _odd_pre(ctx, ctx_mod(l, 0), ctx_mod(l, 1), w, dft_c, sw, sb, tm_ctx)
                ctx = _odd_post(ctx, _fnet_short(u), sf, ys, ctx_mod(l, 2), wo, lng, lnb, tm_ctx)
            x = x_new
    return x
```

```python
import functools
import math

import numpy as np
import jax
import jax.numpy as jnp
from jax import lax
from jax.experimental import pallas as pl
from jax.experimental.pallas import tpu as pltpu

F32 = jnp.float32
BF16 = jnp.bfloat16

DEPTH = 4
GRID_W = 64
DEEPNORM_ALPHA = (2 * DEPTH) ** 0.25
LN_EPS = 1e-6

GLA_HEADS = 4
GLA_DK = 64
GLA_DV = 128
GLA_QK_W = GLA_HEADS * GLA_DK
GLA_V_W = GLA_HEADS * GLA_DV
GLA_GATE_RANK = 16
GLA_TAU = 16.0
GLA_CHUNK = 64

MLA_HEADS = 8
MLA_NOPE = 64
MLA_ROPE = 32
MLA_V = 64
MLA_Q_RANK = 256
MLA_KV_RANK = 128
MLA_V_W = MLA_HEADS * MLA_V
MLA_SCALE = (MLA_NOPE + MLA_ROPE) ** -0.5
ROPE_BASE = 10000.0
MLA_HEAD_PAD = 128
MLA_VT_ROWS = MLA_V + 16
MLA_VT_W = MLA_HEADS * MLA_VT_ROWS

FNET_GROUPS = 4
FNET_GROUP_CH = 128
FNET_W = FNET_GROUPS * FNET_GROUP_CH
FFT_N1 = 64

SGU_GROUPS = 4
SGU_GROUP_CH = 128
SGU_W = SGU_GROUPS * SGU_GROUP_CH
SGU_CHUNK = 128

EVEN_IN_SIZES = (GLA_QK_W, GLA_QK_W, GLA_V_W, 2 * GLA_GATE_RANK, GLA_V_W,
                 MLA_Q_RANK, MLA_KV_RANK, MLA_ROPE, MLA_V_W)

E_GQ, E_GK, E_GV, E_GG, E_CQ, E_CKV, E_MG, E_LR, E_KRA, E_KRB, E_END = (
    0, 256, 512, 1024, 1536, 1792, 1920, 2432, 2560, 2688, 2816)

VMEM_LIMIT_BYTES = 56 * 1024 * 1024
LOG2E = math.log2(math.e)
NEG_BIG = -1e30


def _cparams(*sem):
    return pltpu.CompilerParams(dimension_semantics=sem, vmem_limit_bytes=VMEM_LIMIT_BYTES)


def _dot(a, b):
    return jnp.dot(a, b, preferred_element_type=F32)


def _dot_nt(a, b):
    return lax.dot_general(a, b, (((1,), (1,)), ((), ())), preferred_element_type=F32)


def _dot_tn(a, b):
    return lax.dot_general(a, b, (((0,), (0,)), ((), ())), preferred_element_type=F32)


def _ln(x):
    xc = x - jnp.mean(x, -1, keepdims=True)
    return xc * lax.rsqrt(jnp.mean(xc * xc, -1, keepdims=True) + LN_EPS)


def _rms(x):
    return x * lax.rsqrt(jnp.mean(x * x, -1, keepdims=True) + LN_EPS)


def _silu(x):
    return x / (1.0 + jnp.exp(-x))


def _gelu(x):
    return 0.5 * x * (1.0 + lax.erf(x * (2.0 ** -0.5)))


def _tile_lanes(x, reps):
    return jnp.concatenate([x] * reps, axis=1)


def _full(shape):
    n = len(shape)
    return pl.BlockSpec(shape, lambda *_: (0,) * n)


def _mods_kernel(cond_ref, w_ref, b_ref, o_ref):
    s = _silu(cond_ref[...])
    o_ref[0] = jnp.dot(s, w_ref[0], preferred_element_type=F32,
                       precision=lax.Precision.HIGHEST) + b_ref[0]


def _mods(cond, ada_w, ada_b):
    depth, d, d3 = ada_w.shape
    nb = d3 // d
    return pl.pallas_call(
        _mods_kernel,
        grid=(depth, nb),
        in_specs=[pl.BlockSpec((8, d), lambda l, j: (0, 0)),
                  pl.BlockSpec((1, d, d), lambda l, j: (l, 0, j)),
                  pl.BlockSpec((1, 1, d), lambda l, j: (l, 0, j))],
        out_specs=pl.BlockSpec((1, 8, d), lambda l, j: (l, 0, j)),
        out_shape=jax.ShapeDtypeStruct((depth, 8, d3), F32),
        compiler_params=_cparams("arbitrary", "arbitrary"),
        name="ada_mod",
    )(cond, ada_w, ada_b.reshape(depth, 1, d3))


def _even_pre_kernel(x_ref, shift_ref, scale_ref, w_ref, w2_ref, gb_ref, qg_ref, wqa_ref, wqb_ref,
                     kvg_ref, wk_ref, wv_ref, vb_ref, cq_ref, sq_ref, ck_ref, sk_ref, *rest):
    q_ref, k_ref, v_ref, g_ref, sgg_ref, smg_ref, qt_ref, kk_ref, vt_ref = rest[-9:]
    h = _ln(x_ref[0]) * (1.0 + scale_ref[0]) + shift_ref[0]
    z = _dot(h.astype(BF16), w_ref[...])
    q_ref[0] = z[:, E_GQ:E_GK].astype(BF16)
    k_ref[0] = z[:, E_GK:E_GV].astype(BF16)
    v_ref[0] = z[:, E_GV:E_GG].astype(BF16)
    sgg_ref[0] = _silu(z[:, E_GG:E_CQ]).astype(BF16)
    smg_ref[0] = _silu(z[:, E_MG:E_LR]).astype(BF16)
    pre = _dot(z[:, E_LR:E_KRA].astype(BF16), w2_ref[...]) + gb_ref[...]
    g_ref[0] = jax.nn.log_sigmoid(pre) * (1.0 / GLA_TAU)
    cqn = (_rms(z[:, E_CQ:E_CKV]) * qg_ref[...]).astype(BF16)
    qfull = (_dot(cqn, wqa_ref[...]) * _tile_lanes(cq_ref[...], MLA_HEADS)
             + _dot(cqn, wqb_ref[...]) * _tile_lanes(sq_ref[...], MLA_HEADS))
    qt_ref[0] = qfull.T.astype(BF16)
    ckvn = (_rms(z[:, E_CKV:E_MG]) * kvg_ref[...]).astype(BF16)
    kr = z[:, E_KRA:E_KRB] * ck_ref[...] + z[:, E_KRB:E_END] * sk_ref[...]
    kk_ref[0] = (_dot(ckvn, wk_ref[...]) + _tile_lanes(kr, MLA_HEADS)).astype(BF16)
    vt_ref[0] = (_dot(ckvn, wv_ref[...]) + vb_ref[...]).T.astype(BF16)


def _even_pre(x, shift, scale, wts, tabs, tm, n_keys, key_lo, kv_bufs=None):
    bsz, t, d = x.shape
    hq = MLA_HEADS * MLA_HEAD_PAD
    assert key_lo % tm == 0
    ko = key_lo // tm
    tok = lambda w: pl.BlockSpec((1, tm, w), lambda b, i: (b, i, 0))
    tab = pl.BlockSpec((tm, MLA_HEAD_PAD), lambda b, i: (i, 0))
    mod = pl.BlockSpec((1, 1, d), lambda b, i: (b, 0, 0))
    out_shape = (
        jax.ShapeDtypeStruct((bsz, t, GLA_QK_W), BF16),
        jax.ShapeDtypeStruct((bsz, t, GLA_QK_W), BF16),
        jax.ShapeDtypeStruct((bsz, t, GLA_V_W), BF16),
        jax.ShapeDtypeStruct((bsz, t, 2 * GLA_QK_W), F32),
        jax.ShapeDtypeStruct((bsz, t, GLA_V_W), BF16),
        jax.ShapeDtypeStruct((bsz, t, MLA_V_W), BF16),
        jax.ShapeDtypeStruct((bsz, hq, t), BF16),
        jax.ShapeDtypeStruct((bsz, n_keys, hq), BF16),
        jax.ShapeDtypeStruct((bsz, MLA_VT_W, n_keys), BF16),
    )
    out_specs = (tok(GLA_QK_W), tok(GLA_QK_W), tok(GLA_V_W), tok(2 * GLA_QK_W), tok(GLA_V_W), tok(MLA_V_W),
                 pl.BlockSpec((1, hq, tm), lambda b, i: (b, 0, i)),
                 pl.BlockSpec((1, tm, hq), lambda b, i: (b, ko + i, 0)),
                 pl.BlockSpec((1, MLA_VT_W, tm), lambda b, i: (b, 0, ko + i)))
    in_specs = [tok(d), mod, mod] + [_full(w.shape) for w in wts] + [tab] * 4
    args = [x, shift, scale, *wts, *tabs]
    aliases = {}
    if kv_bufs is not None:
        aliases = {len(args): 7, len(args) + 1: 8}
        in_specs += [pl.BlockSpec(memory_space=pl.ANY)] * 2
        args += list(kv_bufs)
    return pl.pallas_call(
        _even_pre_kernel, grid=(bsz, t // tm), in_specs=in_specs, out_specs=out_specs, out_shape=out_shape,
        input_output_aliases=aliases,
        compiler_params=_cparams("arbitrary", "arbitrary"), name="even_pre",
    )(*args)


def _gla_chunk(q, k, v, g, s, tmat, mid, end, tri, head_masks, ones):
    g_hi = g.astype(BF16)
    g_lo = (g - g_hi.astype(F32)).astype(BF16)
    b = _dot(tmat, g_hi) + _dot(tmat, g_lo)
    b_mid = b[mid:mid + 1]
    b_end = b[end:end + 1]
    qe = q * jnp.exp(b - b_mid)
    ke = (k * jnp.exp(b_mid - b)).astype(BF16)
    kd = (k * jnp.exp(b_end - b)).astype(BF16)
    qb = q * jnp.exp(b)
    dec = jnp.exp(_dot_tn(g_hi, ones) + _dot_tn(g_lo, ones))
    stack = lambda a: jnp.concatenate([jnp.where(m, a, 0.0) for m in head_masks], axis=0).astype(BF16)
    att = _dot_nt(stack(qe), ke)
    o_inter = _dot(stack(qb), s.astype(BF16))
    outs, ds = [], []
    c = q.shape[0]
    for h in range(GLA_HEADS):
        a_h = jnp.where(tri, att[h * c:(h + 1) * c], 0.0).astype(BF16)
        v_h = v[:, h * GLA_DV:(h + 1) * GLA_DV]
        outs.append(_dot(a_h, v_h) + o_inter[h * c:(h + 1) * c])
        ds.append(_dot_tn(kd, v_h)[h * GLA_DK:(h + 1) * GLA_DK])
    return outs, dec * s + jnp.concatenate(ds, axis=0)


def _gla_kernel(qf_ref, kf_ref, vf_ref, gf_ref, qb_ref, kb_ref, vb_ref, gb_ref, s0_ref,
                of_ref, ob_ref, sfin_ref, s_scr, *, nchunk):
    i = pl.program_id(1)

    @pl.when(i == 0)
    def _():
        s_scr[...] = s0_ref[0]

    c = GLA_CHUNK
    row = lax.broadcasted_iota(jnp.int32, (c, c), 0)
    col = lax.broadcasted_iota(jnp.int32, (c, c), 1)
    lower = col <= row
    upper = col >= row
    lmat = jnp.where(lower, 1.0, 0.0).astype(BF16)
    umat = jnp.where(upper, 1.0, 0.0).astype(BF16)
    ones = jnp.ones((c, GLA_DV), BF16)
    lane_head = lax.broadcasted_iota(jnp.int32, (c, GLA_QK_W), 1) // GLA_DK
    head_masks = [lane_head == h for h in range(GLA_HEADS)]

    s_f = s_scr[0]
    for j in range(nchunk):
        r = slice(j * c, (j + 1) * c)
        outs, s_f = _gla_chunk(qf_ref[0, r, :].astype(F32), kf_ref[0, r, :].astype(F32), vf_ref[0, r, :],
                               gf_ref[0, r, :], s_f, lmat, c // 2 - 1, c - 1, lower, head_masks, ones)
        for h in range(GLA_HEADS):
            of_ref[0, r, h * GLA_DV:(h + 1) * GLA_DV] = outs[h]
    s_scr[0] = s_f

    s_b = s_scr[1]
    for j in reversed(range(nchunk)):
        r = slice(j * c, (j + 1) * c)
        outs, s_b = _gla_chunk(qb_ref[0, r, :].astype(F32), kb_ref[0, r, :].astype(F32), vb_ref[0, r, :],
                               gb_ref[0, r, :], s_b, umat, c // 2, 0, upper, head_masks, ones)
        for h in range(GLA_HEADS):
            ob_ref[0, r, h * GLA_DV:(h + 1) * GLA_DV] = outs[h]
    s_scr[1] = s_b

    @pl.when(i == pl.num_programs(1) - 1)
    def _():
        sfin_ref[0] = s_scr[...]


def _gla(q, k, v, g, s0, tb):
    bsz, t, _ = q.shape
    nblk = t // tb
    fwd = lambda w: pl.BlockSpec((1, tb, w), lambda b, i: (b, i, 0))
    bwd = lambda w: pl.BlockSpec((1, tb, w), lambda b, i: (b, nblk - 1 - i, 0))
    st = pl.BlockSpec((1, 2, GLA_QK_W, GLA_DV), lambda b, i: (b, 0, 0, 0))
    return pl.pallas_call(
        functools.partial(_gla_kernel, nchunk=tb // GLA_CHUNK),
        grid=(bsz, nblk),
        in_specs=[fwd(GLA_QK_W), fwd(GLA_QK_W), fwd(GLA_V_W),
                  pl.BlockSpec((1, tb, GLA_QK_W), lambda b, i: (b, i, 0)),
                  bwd(GLA_QK_W), bwd(GLA_QK_W), bwd(GLA_V_W),
                  pl.BlockSpec((1, tb, GLA_QK_W), lambda b, i: (b, nblk - 1 - i, 1)),
                  st],
        out_specs=(fwd(GLA_V_W), bwd(GLA_V_W), st),
        out_shape=(jax.ShapeDtypeStruct((bsz, t, GLA_V_W), F32),
                   jax.ShapeDtypeStruct((bsz, t, GLA_V_W), F32),
                   jax.ShapeDtypeStruct((bsz, 2, GLA_QK_W, GLA_DV), F32)),
        scratch_shapes=[pltpu.VMEM((2, GLA_QK_W, GLA_DV), F32)],
        compiler_params=_cparams("arbitrary", "arbitrary"), name="gla_scan",
    )(q, k, v, g, q, k, v, g, s0)


MLA_SLOTS = 3
MLA_TK = 256


def _mla_kernel(qt_ref, k_ref, vt_ref, o_ref, *scratch, n_chunks, tk):
    q_t = qt_ref[0]
    tq = q_t.shape[1]
    s_bufs, p_bufs = scratch[:MLA_SLOTS], scratch[MLA_SLOTS:]
    cmax = None
    m = jnp.full((1, tq), NEG_BIG, F32)
    alpha = None
    acc = jnp.zeros((MLA_VT_ROWS, tq), F32)
    for t in range(n_chunks + 2):
        alpha_prev = alpha
        if 1 <= t <= n_chunks:
            e = (t - 1) % MLA_SLOTS
            m_new = jnp.maximum(m, jnp.max(cmax, axis=0, keepdims=True))
            alpha = jnp.exp2(m - m_new)
            m = m_new
            p_bufs[e][...] = jnp.exp2(s_bufs[e][...] - m).astype(BF16)
        if t < n_chunks:
            s_t = _dot(k_ref[0, t * tk:(t + 1) * tk, :], q_t)
            s_bufs[t % MLA_SLOTS][...] = s_t
            cmax = jnp.max(s_t.reshape(tk // 8, 8, tq), axis=0)
        if t >= 2:
            vt = vt_ref[0, :, (t - 2) * tk:(t - 1) * tk]
            acc = alpha_prev * acc + _dot(vt, p_bufs[(t - 2) % MLA_SLOTS][...])
    o_ref[0] = acc[0:MLA_V] / acc[MLA_V:MLA_V + 1]


def _mla(qt, k, vt, tq, tk, key_lo, n_keys):
    bsz, _, t = qt.shape
    assert n_keys % tk == 0 and t % tq == 0 and key_lo % n_keys == 0
    kb = key_lo // n_keys
    return pl.pallas_call(
        functools.partial(_mla_kernel, n_chunks=n_keys // tk, tk=tk),
        grid=(bsz, MLA_HEADS, t // tq),
        in_specs=[pl.BlockSpec((1, MLA_HEAD_PAD, tq), lambda b, h, i: (b, h, i)),
                  pl.BlockSpec((1, n_keys, MLA_HEAD_PAD), lambda b, h, i: (b, kb, h)),
                  pl.BlockSpec((1, MLA_VT_ROWS, n_keys), lambda b, h, i: (b, h, kb))],
        out_specs=pl.BlockSpec((1, MLA_V, tq), lambda b, h, i: (b, h, i)),
        out_shape=jax.ShapeDtypeStruct((bsz, MLA_V_W, t), F32),
        scratch_shapes=[pltpu.VMEM((tk, tq), F32)] * MLA_SLOTS + [pltpu.VMEM((tk, tq), BF16)] * MLA_SLOTS,
        compiler_params=_cparams("arbitrary", "arbitrary", "arbitrary"), name="mla_attn",
    )(qt, k, vt)


def _post_tail(x, y, gate, lng, lnb):
    return _ln(DEEPNORM_ALPHA * x + gate * y) * lng + lnb


def _even_post_kernel(x_ref, of_ref, ob_ref, sgg_ref, ot_ref, smg_ref, gate_ref, wo_ref, gng_ref,
                      lng_ref, lnb_ref, o_ref):
    o = of_ref[0] + ob_ref[0]
    parts = [_rms(o[:, h * GLA_DV:(h + 1) * GLA_DV]) * gng_ref[...] for h in range(GLA_HEADS)]
    yg = jnp.concatenate(parts, axis=1) * sgg_ref[0].astype(F32)
    ym = ot_ref[0].T * smg_ref[0].astype(F32)
    y = _dot(yg.astype(BF16), wo_ref[0:GLA_V_W, :]) + _dot(ym.astype(BF16), wo_ref[GLA_V_W:, :])
    o_ref[0] = _post_tail(x_ref[0], y, gate_ref[0], lng_ref[...], lnb_ref[...])


def _even_post(x, o_f, o_b, sgg, o_t, smg, gate, wo, gng, lng, lnb, tm):
    bsz, t, d = x.shape
    tok = lambda w: pl.BlockSpec((1, tm, w), lambda b, i: (b, i, 0))
    return pl.pallas_call(
        _even_post_kernel, grid=(bsz, t // tm),
        in_specs=[tok(d), tok(GLA_V_W), tok(GLA_V_W), tok(GLA_V_W),
                  pl.BlockSpec((1, MLA_V_W, tm), lambda b, i: (b, 0, i)), tok(MLA_V_W),
                  pl.BlockSpec((1, 1, d), lambda b, i: (b, 0, 0)),
                  _full(wo.shape), _full(gng.shape), _full(lng.shape), _full(lnb.shape)],
        out_specs=tok(d), out_shape=jax.ShapeDtypeStruct((bsz, t, d), F32),
        compiler_params=_cparams("arbitrary", "arbitrary"), name="even_post",
    )(x, o_f, o_b, sgg, o_t, smg, gate, wo, gng, lng, lnb)


def _odd_pre_kernel(x_ref, shift_ref, scale_ref, w_ref, dft_ref, sw_ref, sb_ref, u_ref, sf_ref, ys_ref):
    h = _ln(x_ref[0]) * (1.0 + scale_ref[0]) + shift_ref[0]
    z = _dot(h.astype(BF16), w_ref[...])
    u_ref[0] = _dot(z[:, 0:FNET_W].astype(BF16), dft_ref[...]).astype(BF16)
    sf_ref[0] = _silu(z[:, FNET_W:2 * FNET_W]).astype(BF16)
    o0 = 2 * FNET_W
    tm = z.shape[0]
    for g in range(SGU_GROUPS):
        cs = slice(g * SGU_GROUP_CH, (g + 1) * SGU_GROUP_CH)
        ug = _gelu(z[:, o0 + g * SGU_GROUP_CH:o0 + (g + 1) * SGU_GROUP_CH])
        vg = _ln(_gelu(z[:, o0 + SGU_W + g * SGU_GROUP_CH:o0 + SGU_W + (g + 1) * SGU_GROUP_CH])).astype(BF16)
        sg = _silu(z[:, o0 + 2 * SGU_W + g * SGU_GROUP_CH:o0 + 2 * SGU_W + (g + 1) * SGU_GROUP_CH])
        for c in range(tm // SGU_CHUNK):
            rs = slice(c * SGU_CHUNK, (c + 1) * SGU_CHUNK)
            sv = _dot(sw_ref[g], vg[rs]) + sb_ref[g]
            ys_ref[0, rs, cs] = (ug[rs] * sv * sg[rs]).astype(BF16)


def _odd_pre(x, shift, scale, w, dft, sw, sb, tm):
    bsz, t, d = x.shape
    tok = lambda w_: pl.BlockSpec((1, tm, w_), lambda b, i: (b, i, 0))
    mod = pl.BlockSpec((1, 1, d), lambda b, i: (b, 0, 0))
    return pl.pallas_call(
        _odd_pre_kernel, grid=(bsz, t // tm),
        in_specs=[tok(d), mod, mod, _full(w.shape), _full(dft.shape), _full(sw.shape), _full(sb.shape)],
        out_specs=(tok(2 * FNET_W), tok(FNET_W), tok(SGU_W)),
        out_shape=(jax.ShapeDtypeStruct((bsz, t, 2 * FNET_W), BF16),
                   jax.ShapeDtypeStruct((bsz, t, FNET_W), BF16),
                   jax.ShapeDtypeStruct((bsz, t, SGU_W), BF16)),
        compiler_params=_cparams("arbitrary", "arbitrary"), name="odd_pre",
    )(x, shift, scale, w, dft, sw, sb)


FFT_T2_BLK = 8
FFT_P1_BLK = 8


def _fft1_kernel(u_ref, w1_ref, tc_ref, ts_ref, z_ref):
    pq = _dot(w1_ref[...], u_ref[0])
    n1 = FFT_N1
    w2 = 2 * FNET_W
    for j in range(FFT_T2_BLK):
        a_c = pq[0:n1, j * w2:j * w2 + FNET_W]
        b_c = pq[0:n1, j * w2 + FNET_W:(j + 1) * w2]
        a_s = pq[n1:, j * w2:j * w2 + FNET_W]
        b_s = pq[n1:, j * w2 + FNET_W:(j + 1) * w2]
        zr = a_c - b_s
        zi = -b_c - a_s
        tc = _tile_lanes(tc_ref[:, j * 128:(j + 1) * 128], FNET_W // 128)
        ts = _tile_lanes(ts_ref[:, j * 128:(j + 1) * 128], FNET_W // 128)
        z_ref[0, :, j * w2:j * w2 + FNET_W] = (zr * tc + zi * ts).astype(BF16)
        z_ref[0, :, j * w2 + FNET_W:(j + 1) * w2] = (zi * tc - zr * ts).astype(BF16)


def _fft2_kernel(z_ref, c2_ref, s2_ref, y_ref):
    for j in range(FFT_P1_BLK):
        zp = z_ref[0, j]
        y_ref[0, :, j, :] = _dot(c2_ref[...], zp[:, 0:FNET_W]) + _dot(s2_ref[...], zp[:, FNET_W:])


def _dft_tables(t):
    n1, n2 = FFT_N1, t // FFT_N1
    p1 = np.arange(n1, dtype=np.float64)
    a1 = 2.0 * np.pi * np.outer(p1, p1) / n1
    w1 = np.concatenate([np.cos(a1), np.sin(a1)], axis=0)
    at = 2.0 * np.pi * np.outer(p1, np.arange(n2, dtype=np.float64)) / t
    tc = np.repeat(np.cos(at)[:, :, None], 128, axis=2).reshape(n1, n2 * 128)
    ts = np.repeat(np.sin(at)[:, :, None], 128, axis=2).reshape(n1, n2 * 128)
    p2 = np.arange(n2, dtype=np.float64)
    a2 = 2.0 * np.pi * np.outer(p2, p2) / n2
    norm = 1.0 / math.sqrt(t * FNET_GROUP_CH)
    return (jnp.asarray(w1, F32).astype(BF16), jnp.asarray(tc, F32), jnp.asarray(ts, F32),
            jnp.asarray(np.cos(a2) * norm, F32).astype(BF16), jnp.asarray(np.sin(a2) * norm, F32).astype(BF16))


def _channel_dft_matrix():
    d = np.arange(FNET_GROUP_CH, dtype=np.float64)
    a = 2.0 * np.pi * np.outer(d, d) / FNET_GROUP_CH
    m = np.zeros((FNET_W, 2 * FNET_W))
    for g in range(FNET_GROUPS):
        r = slice(g * FNET_GROUP_CH, (g + 1) * FNET_GROUP_CH)
        m[r, g * FNET_GROUP_CH:(g + 1) * FNET_GROUP_CH] = np.cos(a)
        m[r, FNET_W + g * FNET_GROUP_CH:FNET_W + (g + 1) * FNET_GROUP_CH] = np.sin(a)
    return jnp.asarray(m, F32).astype(BF16)


def _fnet_long(u):
    bsz, t, w2 = u.shape
    n1, n2 = FFT_N1, t // FFT_N1
    w1, tc, ts, c2, s2 = _dft_tables(t)
    cols = FFT_T2_BLK * w2
    z = pl.pallas_call(
        _fft1_kernel, grid=(bsz, n2 // FFT_T2_BLK),
        in_specs=[pl.BlockSpec((1, n1, cols), lambda b, i: (b, 0, i)), _full(w1.shape),
                  pl.BlockSpec((n1, FFT_T2_BLK * 128), lambda b, i: (0, i)),
                  pl.BlockSpec((n1, FFT_T2_BLK * 128), lambda b, i: (0, i))],
        out_specs=pl.BlockSpec((1, n1, cols), lambda b, i: (b, 0, i)),
        out_shape=jax.ShapeDtypeStruct((bsz, n1, n2 * w2), BF16),
        compiler_params=_cparams("arbitrary", "arbitrary"), name="fnet_stage1",
    )(u.reshape(bsz, n1, n2 * w2), w1, tc, ts)
    y = pl.pallas_call(
        _fft2_kernel, grid=(bsz, n1 // FFT_P1_BLK),
        in_specs=[pl.BlockSpec((1, FFT_P1_BLK, n2, w2), lambda b, i: (b, i, 0, 0)),
                  _full(c2.shape), _full(s2.shape)],
        out_specs=pl.BlockSpec((1, n2, FFT_P1_BLK, FNET_W), lambda b, i: (b, 0, i, 0)),
        out_shape=jax.ShapeDtypeStruct((bsz, n2, n1, FNET_W), F32),
        compiler_params=_cparams("arbitrary", "arbitrary"), name="fnet_stage2",
    )(z.reshape(bsz, n1, n2, w2), c2, s2)
    return y.reshape(bsz, t, FNET_W)


def _fnet_short_kernel(u_ref, c_ref, s_ref, y_ref):
    u = u_ref[0]
    y_ref[0] = _dot(c_ref[...], u[:, 0:FNET_W]) - _dot(s_ref[...], u[:, FNET_W:])


def _fnet_short(u):
    bsz, t, w2 = u.shape
    p = np.arange(t, dtype=np.float64)
    a = 2.0 * np.pi * np.outer(p, p) / t
    norm = 1.0 / math.sqrt(t * FNET_GROUP_CH)
    c, s = jnp.asarray(np.cos(a) * norm, F32).astype(BF16), jnp.asarray(np.sin(a) * norm, F32).astype(BF16)
    return pl.pallas_call(
        _fnet_short_kernel, grid=(bsz,),
        in_specs=[pl.BlockSpec((1, t, w2), lambda b: (b, 0, 0)), _full(c.shape), _full(s.shape)],
        out_specs=pl.BlockSpec((1, t, FNET_W), lambda b: (b, 0, 0)),
        out_shape=jax.ShapeDtypeStruct((bsz, t, FNET_W), F32),
        compiler_params=_cparams("arbitrary"), name="fnet_short",
    )(u, c, s)


def _odd_post_kernel(x_ref, fr_ref, sf_ref, ys_ref, gate_ref, wo_ref, lng_ref, lnb_ref, o_ref):
    yf = (fr_ref[0] * sf_ref[0].astype(F32)).astype(BF16)
    y = _dot(yf, wo_ref[0:FNET_W, :]) + _dot(ys_ref[0], wo_ref[FNET_W:, :])
    o_ref[0] = _post_tail(x_ref[0], y, gate_ref[0], lng_ref[...], lnb_ref[...])


def _odd_post(x, fr, sf, ys, gate, wo, lng, lnb, tm):
    bsz, t, d = x.shape
    tok = lambda w: pl.BlockSpec((1, tm, w), lambda b, i: (b, i, 0))
    return pl.pallas_call(
        _odd_post_kernel, grid=(bsz, t // tm),
        in_specs=[tok(d), tok(FNET_W), tok(FNET_W), tok(SGU_W),
                  pl.BlockSpec((1, 1, d), lambda b, i: (b, 0, 0)),
                  _full(wo.shape), _full(lng.shape), _full(lnb.shape)],
        out_specs=tok(d), out_shape=jax.ShapeDtypeStruct((bsz, t, d), F32),
        compiler_params=_cparams("arbitrary", "arbitrary"), name="odd_post",
    )(x, fr, sf, ys, gate, wo, lng, lnb)


def _even_weights(w_in, gla_w2, gla_b, q_norm_g, w_uq, kv_norm_g, w_ukv):
    d = w_in.shape[0]
    idx = np.cumsum(EVEN_IN_SIZES)[:-1].tolist()
    gq, gk, gv, glr, gg, cq, ckv, kr, mg = jnp.split(w_in, idx, axis=1)
    half = MLA_ROPE // 2
    z = lambda n: jnp.zeros((d, n), w_in.dtype)
    lr_pad = jnp.concatenate([glr, z(128 - 2 * GLA_GATE_RANK)], axis=1)
    kra = jnp.concatenate([z(MLA_NOPE), kr, z(MLA_HEAD_PAD - MLA_NOPE - MLA_ROPE)], axis=1)
    krb = jnp.concatenate([z(MLA_NOPE), kr[:, half:], kr[:, :half], z(MLA_HEAD_PAD - MLA_NOPE - MLA_ROPE)], axis=1)
    w = jnp.concatenate([gq * GLA_DK ** -0.5, gk, gv, gg, cq, ckv, mg, lr_pad, kra, krb], axis=1).astype(BF16)
    w2 = jnp.zeros((128, 2 * GLA_QK_W), F32)
    w2 = w2.at[0:GLA_GATE_RANK, 0:GLA_QK_W].set(gla_w2[0])
    w2 = w2.at[GLA_GATE_RANK:2 * GLA_GATE_RANK, GLA_QK_W:].set(gla_w2[1])
    gb = jnp.concatenate([gla_b[0], gla_b[1]])[None, :]
    uq = w_uq.reshape(MLA_Q_RANK, MLA_HEADS, MLA_NOPE + MLA_ROPE)
    pad = MLA_HEAD_PAD - MLA_NOPE - MLA_ROPE
    zq = lambda n: jnp.zeros((MLA_Q_RANK, MLA_HEADS, n), w_uq.dtype)
    wqa = jnp.concatenate([uq, zq(pad)], axis=2).reshape(MLA_Q_RANK, -1)
    wqb = jnp.concatenate([zq(MLA_NOPE), uq[:, :, MLA_NOPE + half:], uq[:, :, MLA_NOPE:MLA_NOPE + half], zq(pad)],
                          axis=2).reshape(MLA_Q_RANK, -1)
    ukv = w_ukv.reshape(MLA_KV_RANK, MLA_HEADS, MLA_NOPE + MLA_V)
    wk = jnp.concatenate([ukv[:, :, :MLA_NOPE], jnp.zeros((MLA_KV_RANK, MLA_HEADS, MLA_HEAD_PAD - MLA_NOPE), w_ukv.dtype)],
                         axis=2).reshape(MLA_KV_RANK, -1)
    vpad = MLA_VT_ROWS - MLA_V
    wv = jnp.concatenate([ukv[:, :, MLA_NOPE:], jnp.zeros((MLA_KV_RANK, MLA_HEADS, vpad), w_ukv.dtype)],
                         axis=2).reshape(MLA_KV_RANK, -1)
    vbias = np.zeros((MLA_HEADS, MLA_VT_ROWS), np.float32)
    vbias[:, MLA_V] = 1.0
    return (w, w2.astype(BF16), gb, q_norm_g[None, :], wqa.astype(BF16), wqb.astype(BF16),
            kv_norm_g[None, :], wk.astype(BF16), wv.astype(BF16), jnp.asarray(vbias.reshape(1, -1)))


def _rope_tables(n):
    row = jnp.repeat(jnp.arange(n // GRID_W, dtype=F32), GRID_W)
    col = (jnp.arange(n) % GRID_W).astype(F32)
    n_freq = MLA_ROPE // 4
    inv = ROPE_BASE ** (-jnp.arange(n_freq, dtype=F32) / n_freq)
    ang = jnp.concatenate([row[:, None] * inv, col[:, None] * inv], -1)
    cos, sin = jnp.cos(ang), jnp.sin(ang)
    pad = jnp.zeros((n, MLA_HEAD_PAD - MLA_NOPE - MLA_ROPE), F32)
    zn = jnp.zeros((n, MLA_NOPE), F32)
    qs = MLA_SCALE * LOG2E
    cq = jnp.concatenate([jnp.full((n, MLA_NOPE), qs, F32), cos * qs, cos * qs, pad], axis=1)
    sq = jnp.concatenate([zn, -sin * qs, sin * qs, pad], axis=1)
    ck = jnp.concatenate([zn, cos, cos, pad], axis=1)
    sk = jnp.concatenate([zn, -sin, sin, pad], axis=1)
    return cq, sq, ck, sk


def _plain_tables(n):
    pad = jnp.zeros((n, MLA_HEAD_PAD - MLA_NOPE - MLA_ROPE), F32)
    zn = jnp.zeros((n, MLA_NOPE), F32)
    zero = jnp.zeros((n, MLA_HEAD_PAD), F32)
    qs = MLA_SCALE * LOG2E
    cq = jnp.concatenate([jnp.full((n, MLA_NOPE + MLA_ROPE), qs, F32), pad], axis=1)
    ck = jnp.concatenate([zn, jnp.ones((n, MLA_ROPE), F32), pad], axis=1)
    return cq, zero, ck, zero


def _pick(t, pref):
    return pref if t % pref == 0 else t


def kernel(x, c, ctx, c_ctx, ada_w, ada_b, post_ln_g, post_ln_b, even_w_in, gla_w2, gla_b, gla_norm_g,
           mla_q_norm_g, mla_w_uq, mla_kv_norm_g, mla_w_ukv, even_w_out, odd_w_in, sgu_w, sgu_b, odd_w_out):
    bsz, n, d = x.shape
    lc = ctx.shape[1]
    depth = ada_w.shape[0]
    assert bsz + 1 <= 8 and n % (FFT_N1 * FFT_T2_BLK) == 0 and n % 512 == 0 and lc % MLA_TK == 0

    cond = jnp.concatenate([c, c_ctx[None, :], jnp.zeros((8 - bsz - 1, d), F32)], axis=0)
    mods = _mods(cond, ada_w, ada_b)

    def lat_mod(l, j):
        return mods[l, :bsz, j * d:(j + 1) * d][:, None, :]

    def ctx_mod(l, j):
        return jnp.broadcast_to(mods[l, bsz, j * d:(j + 1) * d][None, None, :], (bsz, 1, d))

    rope_tabs = _rope_tables(n)
    ctx_tabs = _plain_tables(lc)
    dft_c = _channel_dft_matrix()
    tm_lat, tm_ctx = _pick(n, 512), _pick(lc, 256)
    zero_state = jnp.zeros((bsz, 2, GLA_QK_W, GLA_DV), F32)

    for l in range(depth):
        need_ctx_out = any(j % 2 == 0 for j in range(l + 1, depth))
        i = l // 2
        lng, lnb = post_ln_g[l][None, :], post_ln_b[l][None, :]
        if l % 2 == 0:
            wts = _even_weights(even_w_in[i], gla_w2[i], gla_b[i], mla_q_norm_g[i], mla_w_uq[i],
                                mla_kv_norm_g[i], mla_w_ukv[i])
            wo = even_w_out[i].astype(BF16)
            gng = gla_norm_g[i][None, :]
            pc = _even_pre(ctx, ctx_mod(l, 0), ctx_mod(l, 1), wts, ctx_tabs, tm_ctx, n + lc, n)
            q_c, k_c, v_c, g_c, sgg_c, smg_c, qt_c, kk, vt = pc
            pz = _even_pre(x, lat_mod(l, 0), lat_mod(l, 1), wts, rope_tabs, tm_lat, n + lc, 0, (kk, vt))
            q_l, k_l, v_l, g_l, sgg_l, smg_l, qt_l, kk, vt = pz
            of_c, ob_c, s_c = _gla(q_c, k_c, v_c, g_c, zero_state, _pick(lc, 256))
            of_l, ob_l, _ = _gla(q_l, k_l, v_l, g_l, s_c, _pick(n, 256))
            ot_l = _mla(qt_l, kk, vt, _pick(n, 512), MLA_TK, 0, n + lc)
            x_new = _even_post(x, of_l, ob_l, sgg_l, ot_l, smg_l, lat_mod(l, 2), wo, gng, lng, lnb, tm_lat)
            if need_ctx_out:
                ot_c = _mla(qt_c, kk, vt, lc, MLA_TK, n, lc)
                ctx = _even_post(ctx, of_c, ob_c, sgg_c, ot_c, smg_c, ctx_mod(l, 2), wo, gng, lng, lnb, tm_ctx)
            x = x_new
        else:
            w = odd_w_in[i].astype(BF16)
            wo = odd_w_out[i].astype(BF16)
            sw = sgu_w[i].astype(BF16)
            sb = jnp.broadcast_to(sgu_b[i][:, :, None], (SGU_GROUPS, SGU_CHUNK, SGU_GROUP_CH))
            u, sf, ys = _odd_pre(x, lat_mod(l, 0), lat_mod(l, 1), w, dft_c, sw, sb, tm_lat)
            x_new = _odd_post(x, _fnet_long(u), sf, ys, lat_mod(l, 2), wo, lng, lnb, tm_lat)
            if need_ctx_out:
                u, sf, ys = _odd_pre(ctx, ctx_mod(l, 0), ctx_mod(l, 1), w, dft_c, sw, sb, tm_ctx)
                ctx = _odd_post(ctx, _fnet_short(u), sf, ys, ctx_mod(l, 2), wo, lng, lnb, tm_ctx)
            x = x_new
    return x
```

```python
import functools
import math

import numpy as np
import jax
import jax.numpy as jnp
from jax import lax
from jax.experimental import pallas as pl
from jax.experimental.pallas import tpu as pltpu

F32 = jnp.float32
BF16 = jnp.bfloat16

DEPTH = 4
GRID_W = 64
DEEPNORM_ALPHA = (2 * DEPTH) ** 0.25
LN_EPS = 1e-6

GLA_HEADS = 4
GLA_DK = 64
GLA_DV = 128
GLA_QK_W = GLA_HEADS * GLA_DK
GLA_V_W = GLA_HEADS * GLA_DV
GLA_GATE_RANK = 16
GLA_TAU = 16.0
GLA_CHUNK = 64

MLA_HEADS = 8
MLA_NOPE = 64
MLA_ROPE = 32
MLA_V = 64
MLA_Q_RANK = 256
MLA_KV_RANK = 128
MLA_V_W = MLA_HEADS * MLA_V
MLA_SCALE = (MLA_NOPE + MLA_ROPE) ** -0.5
ROPE_BASE = 10000.0
MLA_HEAD_PAD = 128
MLA_VT_ROWS = MLA_V + 16
MLA_VT_W = MLA_HEADS * MLA_VT_ROWS

FNET_GROUPS = 4
FNET_GROUP_CH = 128
FNET_W = FNET_GROUPS * FNET_GROUP_CH
FFT_N1 = 64

SGU_GROUPS = 4
SGU_GROUP_CH = 128
SGU_W = SGU_GROUPS * SGU_GROUP_CH
SGU_CHUNK = 128

EVEN_IN_SIZES = (GLA_QK_W, GLA_QK_W, GLA_V_W, 2 * GLA_GATE_RANK, GLA_V_W,
                 MLA_Q_RANK, MLA_KV_RANK, MLA_ROPE, MLA_V_W)

E_GQ, E_GK, E_GV, E_GG, E_CQ, E_CKV, E_MG, E_LR, E_KRA, E_KRB, E_END = (
    0, 256, 512, 1024, 1536, 1792, 1920, 2432, 2560, 2688, 2816)

VMEM_LIMIT_BYTES = 56 * 1024 * 1024
LOG2E = math.log2(math.e)
NEG_BIG = -1e30


def _cparams(*sem):
    return pltpu.CompilerParams(dimension_semantics=sem, vmem_limit_bytes=VMEM_LIMIT_BYTES)


def _dot(a, b):
    return jnp.dot(a, b, preferred_element_type=F32)


def _dot_nt(a, b):
    return lax.dot_general(a, b, (((1,), (1,)), ((), ())), preferred_element_type=F32)


def _dot_tn(a, b):
    return lax.dot_general(a, b, (((0,), (0,)), ((), ())), preferred_element_type=F32)


def _ln(x):
    xc = x - jnp.mean(x, -1, keepdims=True)
    return xc * lax.rsqrt(jnp.mean(xc * xc, -1, keepdims=True) + LN_EPS)


def _rms(x):
    return x * lax.rsqrt(jnp.mean(x * x, -1, keepdims=True) + LN_EPS)


def _silu(x):
    return x / (1.0 + jnp.exp(-x))


def _gelu(x):
    return 0.5 * x * (1.0 + lax.erf(x * (2.0 ** -0.5)))


def _tile_lanes(x, reps):
    return jnp.concatenate([x] * reps, axis=1)


def _full(shape):
    n = len(shape)
    return pl.BlockSpec(shape, lambda *_: (0,) * n)


def _mods_kernel(cond_ref, w_ref, b_ref, o_ref):
    s = _silu(cond_ref[...])
    o_ref[0] = jnp.dot(s, w_ref[0], preferred_element_type=F32,
                       precision=lax.Precision.HIGHEST) + b_ref[0]


def _mods(cond, ada_w, ada_b):
    depth, d, d3 = ada_w.shape
    nb = d3 // d
    return pl.pallas_call(
        _mods_kernel,
        grid=(depth, nb),
        in_specs=[pl.BlockSpec((8, d), lambda l, j: (0, 0)),
                  pl.BlockSpec((1, d, d), lambda l, j: (l, 0, j)),
                  pl.BlockSpec((1, 1, d), lambda l, j: (l, 0, j))],
        out_specs=pl.BlockSpec((1, 8, d), lambda l, j: (l, 0, j)),
        out_shape=jax.ShapeDtypeStruct((depth, 8, d3), F32),
        compiler_params=_cparams("arbitrary", "arbitrary"),
        name="ada_mod",
    )(cond, ada_w, ada_b.reshape(depth, 1, d3))


def _even_pre_kernel(x_ref, shift_ref, scale_ref, w_ref, w2_ref, gb_ref, qg_ref, wqa_ref, wqb_ref,
                     kvg_ref, wk_ref, wv_ref, vb_ref, ck_ref, sk_ref, *rest):
    q_ref, k_ref, v_ref, g_ref, sgg_ref, smg_ref, qt_ref, kk_ref, vt_ref = rest[-9:]
    h = _ln(x_ref[0]) * (1.0 + scale_ref[0]) + shift_ref[0]
    z = _dot(h.astype(BF16), w_ref[...])
    q_ref[0] = z[:, E_GQ:E_GK].astype(BF16)
    k_ref[0] = z[:, E_GK:E_GV].astype(BF16)
    v_ref[0] = z[:, E_GV:E_GG].astype(BF16)
    sgg_ref[0] = _silu(z[:, E_GG:E_CQ]).astype(BF16)
    smg_ref[0] = _silu(z[:, E_MG:E_LR]).astype(BF16)
    pre = _dot(z[:, E_LR:E_KRA].astype(BF16), w2_ref[...]) + gb_ref[...]
    g_ref[0] = jax.nn.log_sigmoid(pre) * (1.0 / GLA_TAU)
    cqn = (_rms(z[:, E_CQ:E_CKV]) * qg_ref[...]).astype(BF16)
    lane = lax.broadcasted_iota(jnp.int32, (1, MLA_HEAD_PAD), 1)
    q_scale = MLA_SCALE * LOG2E
    cq = ck_ref[...] * q_scale + jnp.where(lane < MLA_NOPE, q_scale, 0.0)
    sq = sk_ref[...] * q_scale
    qfull = (_dot(cqn, wqa_ref[...]) * _tile_lanes(cq, MLA_HEADS)
             + _dot(cqn, wqb_ref[...]) * _tile_lanes(sq, MLA_HEADS))
    qt_ref[0] = qfull.T.astype(BF16)
    ckvn = (_rms(z[:, E_CKV:E_MG]) * kvg_ref[...]).astype(BF16)
    kr = z[:, E_KRA:E_KRB] * ck_ref[...] + z[:, E_KRB:E_END] * sk_ref[...]
    kk_ref[0] = (_dot(ckvn, wk_ref[...]) + _tile_lanes(kr, MLA_HEADS)).astype(BF16)
    vt_ref[0] = (_dot(ckvn, wv_ref[...]) + vb_ref[...]).T.astype(BF16)


def _even_pre(x, shift, scale, wts, tabs, tm, n_keys, key_lo, kv_bufs=None):
    bsz, t, d = x.shape
    hq = MLA_HEADS * MLA_HEAD_PAD
    assert key_lo % tm == 0
    ko = key_lo // tm
    tok = lambda w: pl.BlockSpec((1, tm, w), lambda b, i: (b, i, 0))
    tab = pl.BlockSpec((tm, MLA_HEAD_PAD), lambda b, i: (i, 0))
    mod = pl.BlockSpec((1, 1, d), lambda b, i: (b, 0, 0))
    out_shape = (
        jax.ShapeDtypeStruct((bsz, t, GLA_QK_W), BF16),
        jax.ShapeDtypeStruct((bsz, t, GLA_QK_W), BF16),
        jax.ShapeDtypeStruct((bsz, t, GLA_V_W), BF16),
        jax.ShapeDtypeStruct((bsz, t, 2 * GLA_QK_W), F32),
        jax.ShapeDtypeStruct((bsz, t, GLA_V_W), BF16),
        jax.ShapeDtypeStruct((bsz, t, MLA_V_W), BF16),
        jax.ShapeDtypeStruct((bsz, hq, t), BF16),
        jax.ShapeDtypeStruct((bsz, n_keys, hq), BF16),
        jax.ShapeDtypeStruct((bsz, MLA_VT_W, n_keys), BF16),
    )
    out_specs = (tok(GLA_QK_W), tok(GLA_QK_W), tok(GLA_V_W), tok(2 * GLA_QK_W), tok(GLA_V_W), tok(MLA_V_W),
                 pl.BlockSpec((1, hq, tm), lambda b, i: (b, 0, i)),
                 pl.BlockSpec((1, tm, hq), lambda b, i: (b, ko + i, 0)),
                 pl.BlockSpec((1, MLA_VT_W, tm), lambda b, i: (b, 0, ko + i)))
    in_specs = [tok(d), mod, mod] + [_full(w.shape) for w in wts] + [tab] * 2
    args = [x, shift, scale, *wts, *tabs]
    aliases = {}
    if kv_bufs is not None:
        aliases = {len(args): 7, len(args) + 1: 8}
        in_specs += [pl.BlockSpec(memory_space=pl.ANY)] * 2
        args += list(kv_bufs)
    return pl.pallas_call(
        _even_pre_kernel, grid=(bsz, t // tm), in_specs=in_specs, out_specs=out_specs, out_shape=out_shape,
        input_output_aliases=aliases,
        compiler_params=_cparams("arbitrary", "arbitrary"), name="even_pre",
    )(*args)


def _gla_block(q, k, v, g, s, tmat, same_tri, mid_off, end_off, head_masks, chunk_masks, reverse):
    r = q.shape[0]
    c = GLA_CHUNK
    nc = r // c
    g_hi = g.astype(BF16)
    g_lo = (g - g_hi.astype(F32)).astype(BF16)
    b = _dot(tmat, g_hi) + _dot(tmat, g_lo)
    chunk_row = lambda off: jnp.concatenate(
        [jnp.broadcast_to(b[j * c + off:j * c + off + 1], (c, GLA_QK_W)) for j in range(nc)], axis=0)
    b_mid, b_end = chunk_row(mid_off), chunk_row(end_off)
    qe = q * jnp.exp(b - b_mid)
    ke = (k * jnp.exp(b_mid - b)).astype(BF16)
    kd_t = (k * jnp.exp(b_end - b)).T
    qb = q * jnp.exp(b)
    stack = lambda a: jnp.concatenate([jnp.where(m, a, 0.0) for m in head_masks], axis=0).astype(BF16)
    att = _dot_nt(stack(qe), ke)
    qb_st = stack(qb)
    ends_t = jnp.concatenate([b[j * c + end_off:j * c + end_off + 1] for j in range(nc)]
                             + [jnp.zeros((8 - nc, GLA_QK_W), F32)], axis=0).T
    intra, ds = [], []
    for h in range(GLA_HEADS):
        a_h = jnp.where(same_tri, att[h * r:(h + 1) * r], 0.0).astype(BF16)
        v_h = v[:, h * GLA_DV:(h + 1) * GLA_DV]
        intra.append(_dot(a_h, v_h))
        kd_h = kd_t[h * GLA_DK:(h + 1) * GLA_DK]
        lhs = jnp.concatenate([jnp.where(cm, kd_h, 0.0) for cm in chunk_masks], axis=0).astype(BF16)
        ds.append(_dot(lhs, v_h))
    outs = [None] * nc
    for j in (reversed(range(nc)) if reverse else range(nc)):
        rows = slice(j * c, (j + 1) * c)
        st = jnp.concatenate([qb_st[h * r + j * c:h * r + (j + 1) * c] for h in range(GLA_HEADS)], axis=0)
        o_inter = _dot(st, s.astype(BF16))
        outs[j] = [intra[h][rows] + o_inter[h * c:(h + 1) * c] for h in range(GLA_HEADS)]
        dec = jnp.exp(jnp.broadcast_to(ends_t[:, j:j + 1], (GLA_QK_W, GLA_DV)))
        s = dec * s + jnp.concatenate([ds[h][rows] for h in range(GLA_HEADS)], axis=0)
    return outs, s


def _gla_kernel(qf_ref, kf_ref, vf_ref, gf_ref, qb_ref, kb_ref, vb_ref, gb_ref, s0_ref, tl_ref, tu_ref,
                of_ref, ob_ref, sfin_ref, s_scr, *, tb):
    i = pl.program_id(1)

    @pl.when(i == 0)
    def _():
        s_scr[...] = s0_ref[0]

    c = GLA_CHUNK
    nc = tb // c
    row = lax.broadcasted_iota(jnp.int32, (tb, tb), 0)
    col = lax.broadcasted_iota(jnp.int32, (tb, tb), 1)
    same = (row // c) == (col // c)
    lane_head = lax.broadcasted_iota(jnp.int32, (tb, GLA_QK_W), 1) // GLA_DK
    head_masks = [lane_head == h for h in range(GLA_HEADS)]
    lane_chunk = lax.broadcasted_iota(jnp.int32, (GLA_DK, tb), 1) // c
    chunk_masks = [lane_chunk == j for j in range(nc)]

    def store(o_ref, outs):
        for j in range(nc):
            for h in range(GLA_HEADS):
                o_ref[0, j * c:(j + 1) * c, h * GLA_DV:(h + 1) * GLA_DV] = outs[j][h].astype(o_ref.dtype)

    outs, s_scr[0] = _gla_block(qf_ref[0].astype(F32), kf_ref[0].astype(F32), vf_ref[0], gf_ref[0], s_scr[0],
                                tl_ref[...], same & (col <= row), c // 2 - 1, c - 1, head_masks, chunk_masks, False)
    store(of_ref, outs)
    outs, s_scr[1] = _gla_block(qb_ref[0].astype(F32), kb_ref[0].astype(F32), vb_ref[0], gb_ref[0], s_scr[1],
                                tu_ref[...], same & (col >= row), c // 2, 0, head_masks, chunk_masks, True)
    store(ob_ref, outs)

    @pl.when(i == pl.num_programs(1) - 1)
    def _():
        sfin_ref[0] = s_scr[...]


def _gla(q, k, v, g, s0, tb):
    bsz, t, _ = q.shape
    nblk = t // tb
    assert tb % GLA_CHUNK == 0 and tb // GLA_CHUNK <= 8
    idx = np.arange(tb)
    same = (idx[:, None] // GLA_CHUNK) == (idx[None, :] // GLA_CHUNK)
    tl = jnp.asarray(same & (idx[None, :] <= idx[:, None]), BF16)
    tu = jnp.asarray(same & (idx[None, :] >= idx[:, None]), BF16)
    fwd = lambda w: pl.BlockSpec((1, tb, w), lambda b, i: (b, i, 0))
    bwd = lambda w: pl.BlockSpec((1, tb, w), lambda b, i: (b, nblk - 1 - i, 0))
    st = pl.BlockSpec((1, 2, GLA_QK_W, GLA_DV), lambda b, i: (b, 0, 0, 0))
    return pl.pallas_call(
        functools.partial(_gla_kernel, tb=tb),
        grid=(bsz, nblk),
        in_specs=[fwd(GLA_QK_W), fwd(GLA_QK_W), fwd(GLA_V_W),
                  pl.BlockSpec((1, tb, GLA_QK_W), lambda b, i: (b, i, 0)),
                  bwd(GLA_QK_W), bwd(GLA_QK_W), bwd(GLA_V_W),
                  pl.BlockSpec((1, tb, GLA_QK_W), lambda b, i: (b, nblk - 1 - i, 1)),
                  st, _full((tb, tb)), _full((tb, tb))],
        out_specs=(fwd(GLA_V_W), bwd(GLA_V_W), st),
        out_shape=(jax.ShapeDtypeStruct((bsz, t, GLA_V_W), BF16),
                   jax.ShapeDtypeStruct((bsz, t, GLA_V_W), BF16),
                   jax.ShapeDtypeStruct((bsz, 2, GLA_QK_W, GLA_DV), F32)),
        scratch_shapes=[pltpu.VMEM((2, GLA_QK_W, GLA_DV), F32)],
        compiler_params=_cparams("arbitrary", "arbitrary"), name="gla_scan",
    )(q, k, v, g, q, k, v, g, s0, tl, tu)


MLA_SLOTS = 3
MLA_TK = 256


def _mla_kernel(qt_ref, k_ref, vt_ref, o_ref, *scratch, n_chunks, tk):
    q_t = qt_ref[0]
    tq = q_t.shape[1]
    s_bufs, p_bufs = scratch[:MLA_SLOTS], scratch[MLA_SLOTS:]
    cmax = None
    m = jnp.full((1, tq), NEG_BIG, F32)
    alpha = None
    acc = jnp.zeros((MLA_VT_ROWS, tq), F32)
    for t in range(n_chunks + 2):
        alpha_prev = alpha
        if 1 <= t <= n_chunks:
            e = (t - 1) % MLA_SLOTS
            m_new = jnp.maximum(m, jnp.max(cmax, axis=0, keepdims=True))
            alpha = jnp.exp2(m - m_new)
            m = m_new
            p_bufs[e][...] = jnp.exp2(s_bufs[e][...] - m).astype(BF16)
        if t < n_chunks:
            s_t = _dot(k_ref[0, t * tk:(t + 1) * tk, :], q_t)
            s_bufs[t % MLA_SLOTS][...] = s_t
            cmax = jnp.max(s_t.reshape(tk // 8, 8, tq), axis=0)
        if t >= 2:
            vt = vt_ref[0, :, (t - 2) * tk:(t - 1) * tk]
            acc = alpha_prev * acc + _dot(vt, p_bufs[(t - 2) % MLA_SLOTS][...])
    o_ref[0] = (acc[0:MLA_V] / acc[MLA_V:MLA_V + 1]).astype(o_ref.dtype)


def _mla(qt, k, vt, tq, tk, key_lo, n_keys):
    bsz, _, t = qt.shape
    assert n_keys % tk == 0 and t % tq == 0 and key_lo % n_keys == 0
    kb = key_lo // n_keys
    return pl.pallas_call(
        functools.partial(_mla_kernel, n_chunks=n_keys // tk, tk=tk),
        grid=(bsz, MLA_HEADS, t // tq),
        in_specs=[pl.BlockSpec((1, MLA_HEAD_PAD, tq), lambda b, h, i: (b, h, i)),
                  pl.BlockSpec((1, n_keys, MLA_HEAD_PAD), lambda b, h, i: (b, kb, h)),
                  pl.BlockSpec((1, MLA_VT_ROWS, n_keys), lambda b, h, i: (b, h, kb))],
        out_specs=pl.BlockSpec((1, MLA_V, tq), lambda b, h, i: (b, h, i)),
        out_shape=jax.ShapeDtypeStruct((bsz, MLA_V_W, t), BF16),
        scratch_shapes=[pltpu.VMEM((tk, tq), F32)] * MLA_SLOTS + [pltpu.VMEM((tk, tq), BF16)] * MLA_SLOTS,
        compiler_params=_cparams("arbitrary", "arbitrary", "arbitrary"), name="mla_attn",
    )(qt, k, vt)


def _post_tail(x, y, gate, lng, lnb):
    return _ln(DEEPNORM_ALPHA * x + gate * y) * lng + lnb


def _even_post_kernel(x_ref, of_ref, ob_ref, sgg_ref, ot_ref, smg_ref, gate_ref, wo_ref, gng_ref,
                      lng_ref, lnb_ref, o_ref):
    o = of_ref[0].astype(F32) + ob_ref[0].astype(F32)
    parts = [_rms(o[:, h * GLA_DV:(h + 1) * GLA_DV]) * gng_ref[...] for h in range(GLA_HEADS)]
    yg = jnp.concatenate(parts, axis=1) * sgg_ref[0].astype(F32)
    ym = ot_ref[0].astype(F32).T * smg_ref[0].astype(F32)
    y = _dot(yg.astype(BF16), wo_ref[0:GLA_V_W, :]) + _dot(ym.astype(BF16), wo_ref[GLA_V_W:, :])
    o_ref[0] = _post_tail(x_ref[0], y, gate_ref[0], lng_ref[...], lnb_ref[...])


def _even_post(x, o_f, o_b, sgg, o_t, smg, gate, wo, gng, lng, lnb, tm):
    bsz, t, d = x.shape
    tok = lambda w: pl.BlockSpec((1, tm, w), lambda b, i: (b, i, 0))
    return pl.pallas_call(
        _even_post_kernel, grid=(bsz, t // tm),
        in_specs=[tok(d), tok(GLA_V_W), tok(GLA_V_W), tok(GLA_V_W),
                  pl.BlockSpec((1, MLA_V_W, tm), lambda b, i: (b, 0, i)), tok(MLA_V_W),
                  pl.BlockSpec((1, 1, d), lambda b, i: (b, 0, 0)),
                  _full(wo.shape), _full(gng.shape), _full(lng.shape), _full(lnb.shape)],
        out_specs=tok(d), out_shape=jax.ShapeDtypeStruct((bsz, t, d), F32),
        compiler_params=_cparams("arbitrary", "arbitrary"), name="even_post",
    )(x, o_f, o_b, sgg, o_t, smg, gate, wo, gng, lng, lnb)


def _odd_pre_kernel(x_ref, shift_ref, scale_ref, w_ref, dft_ref, sw_ref, sb_ref, u_ref, sf_ref, ys_ref):
    h = _ln(x_ref[0]) * (1.0 + scale_ref[0]) + shift_ref[0]
    z = _dot(h.astype(BF16), w_ref[...])
    u_ref[0] = _dot(z[:, 0:FNET_W].astype(BF16), dft_ref[...]).astype(BF16)
    sf_ref[0] = _silu(z[:, FNET_W:2 * FNET_W]).astype(BF16)
    o0 = 2 * FNET_W
    tm = z.shape[0]
    for g in range(SGU_GROUPS):
        cs = slice(g * SGU_GROUP_CH, (g + 1) * SGU_GROUP_CH)
        ug = _gelu(z[:, o0 + g * SGU_GROUP_CH:o0 + (g + 1) * SGU_GROUP_CH])
        vg = _ln(_gelu(z[:, o0 + SGU_W + g * SGU_GROUP_CH:o0 + SGU_W + (g + 1) * SGU_GROUP_CH])).astype(BF16)
        sg = _silu(z[:, o0 + 2 * SGU_W + g * SGU_GROUP_CH:o0 + 2 * SGU_W + (g + 1) * SGU_GROUP_CH])
        for c in range(tm // SGU_CHUNK):
            rs = slice(c * SGU_CHUNK, (c + 1) * SGU_CHUNK)
            sv = _dot(sw_ref[g], vg[rs]) + sb_ref[g]
            ys_ref[0, rs, cs] = (ug[rs] * sv * sg[rs]).astype(BF16)


def _odd_pre(x, shift, scale, w, dft, sw, sb, tm):
    bsz, t, d = x.shape
    tok = lambda w_: pl.BlockSpec((1, tm, w_), lambda b, i: (b, i, 0))
    mod = pl.BlockSpec((1, 1, d), lambda b, i: (b, 0, 0))
    return pl.pallas_call(
        _odd_pre_kernel, grid=(bsz, t // tm),
        in_specs=[tok(d), mod, mod, _full(w.shape), _full(dft.shape), _full(sw.shape), _full(sb.shape)],
        out_specs=(tok(2 * FNET_W), tok(FNET_W), tok(SGU_W)),
        out_shape=(jax.ShapeDtypeStruct((bsz, t, 2 * FNET_W), BF16),
                   jax.ShapeDtypeStruct((bsz, t, FNET_W), BF16),
                   jax.ShapeDtypeStruct((bsz, t, SGU_W), BF16)),
        compiler_params=_cparams("arbitrary", "arbitrary"), name="odd_pre",
    )(x, shift, scale, w, dft, sw, sb)


FFT_T2_BLK = 16
FFT_P1_BLK = 8


def _fft1_kernel(u_ref, w1_ref, tc_ref, ts_ref, z_ref):
    n1, tb = FFT_N1, FFT_T2_BLK
    r = n1 * tb
    pq = _dot(w1_ref[...], u_ref[0].reshape(r, 2 * FNET_W))
    zr = pq[0:r, 0:FNET_W] - pq[r:, FNET_W:]
    zi = -pq[0:r, FNET_W:] - pq[r:, 0:FNET_W]
    tc = _tile_lanes(tc_ref[0], FNET_W // 128)
    ts = _tile_lanes(ts_ref[0], FNET_W // 128)
    z_ref[0, :, :, 0:FNET_W] = (zr * tc + zi * ts).astype(BF16).reshape(n1, tb, FNET_W)
    z_ref[0, :, :, FNET_W:] = (zi * tc - zr * ts).astype(BF16).reshape(n1, tb, FNET_W)


def _fft2_kernel(z_ref, c2_ref, s2_ref, y_ref):
    for j in range(FFT_P1_BLK):
        zp = z_ref[0, j]
        y_ref[0, :, j, :] = _dot(c2_ref[...], zp[:, 0:FNET_W]) + _dot(s2_ref[...], zp[:, FNET_W:])


def _dft_tables(t):
    n1, n2 = FFT_N1, t // FFT_N1
    p1 = np.arange(n1, dtype=np.float64)
    a1 = 2.0 * np.pi * np.outer(p1, p1) / n1
    eye = np.eye(FFT_T2_BLK)
    w1 = np.concatenate([np.kron(np.cos(a1), eye), np.kron(np.sin(a1), eye)], axis=0)
    at = 2.0 * np.pi * np.outer(p1, np.arange(n2, dtype=np.float64)) / t

    def twiddle(a):
        a = a.reshape(n1, n2 // FFT_T2_BLK, FFT_T2_BLK).transpose(1, 0, 2).reshape(n2 // FFT_T2_BLK, -1)
        return np.repeat(a[:, :, None], 128, axis=2)
    tc, ts = twiddle(np.cos(at)), twiddle(np.sin(at))
    p2 = np.arange(n2, dtype=np.float64)
    a2 = 2.0 * np.pi * np.outer(p2, p2) / n2
    norm = 1.0 / math.sqrt(t * FNET_GROUP_CH)
    return (jnp.asarray(w1, F32).astype(BF16), jnp.asarray(tc, F32), jnp.asarray(ts, F32),
            jnp.asarray(np.cos(a2) * norm, F32).astype(BF16), jnp.asarray(np.sin(a2) * norm, F32).astype(BF16))


def _channel_dft_matrix():
    d = np.arange(FNET_GROUP_CH, dtype=np.float64)
    a = 2.0 * np.pi * np.outer(d, d) / FNET_GROUP_CH
    m = np.zeros((FNET_W, 2 * FNET_W))
    for g in range(FNET_GROUPS):
        r = slice(g * FNET_GROUP_CH, (g + 1) * FNET_GROUP_CH)
        m[r, g * FNET_GROUP_CH:(g + 1) * FNET_GROUP_CH] = np.cos(a)
        m[r, FNET_W + g * FNET_GROUP_CH:FNET_W + (g + 1) * FNET_GROUP_CH] = np.sin(a)
    return jnp.asarray(m, F32).astype(BF16)


def _fnet_long(u):
    bsz, t, w2 = u.shape
    n1, n2 = FFT_N1, t // FFT_N1
    w1, tc, ts, c2, s2 = _dft_tables(t)
    blk = pl.BlockSpec((1, n1, FFT_T2_BLK, w2), lambda b, i: (b, 0, i, 0))
    tw = pl.BlockSpec((1, n1 * FFT_T2_BLK, 128), lambda b, i: (i, 0, 0))
    z = pl.pallas_call(
        _fft1_kernel, grid=(bsz, n2 // FFT_T2_BLK),
        in_specs=[blk, _full(w1.shape), tw, tw],
        out_specs=blk,
        out_shape=jax.ShapeDtypeStruct((bsz, n1, n2, w2), BF16),
        compiler_params=_cparams("arbitrary", "arbitrary"), name="fnet_stage1",
    )(u.reshape(bsz, n1, n2, w2), w1, tc, ts)
    y = pl.pallas_call(
        _fft2_kernel, grid=(bsz, n1 // FFT_P1_BLK),
        in_specs=[pl.BlockSpec((1, FFT_P1_BLK, n2, w2), lambda b, i: (b, i, 0, 0)),
                  _full(c2.shape), _full(s2.shape)],
        out_specs=pl.BlockSpec((1, n2, FFT_P1_BLK, FNET_W), lambda b, i: (b, 0, i, 0)),
        out_shape=jax.ShapeDtypeStruct((bsz, n2, n1, FNET_W), F32),
        compiler_params=_cparams("arbitrary", "arbitrary"), name="fnet_stage2",
    )(z, c2, s2)
    return y.reshape(bsz, t, FNET_W)


def _fnet_short_kernel(u_ref, c_ref, s_ref, y_ref):
    u = u_ref[0]
    y_ref[0] = _dot(c_ref[...], u[:, 0:FNET_W]) - _dot(s_ref[...], u[:, FNET_W:])


def _fnet_short(u):
    bsz, t, w2 = u.shape
    p = np.arange(t, dtype=np.float64)
    a = 2.0 * np.pi * np.outer(p, p) / t
    norm = 1.0 / math.sqrt(t * FNET_GROUP_CH)
    c, s = jnp.asarray(np.cos(a) * norm, F32).astype(BF16), jnp.asarray(np.sin(a) * norm, F32).astype(BF16)
    return pl.pallas_call(
        _fnet_short_kernel, grid=(bsz,),
        in_specs=[pl.BlockSpec((1, t, w2), lambda b: (b, 0, 0)), _full(c.shape), _full(s.shape)],
        out_specs=pl.BlockSpec((1, t, FNET_W), lambda b: (b, 0, 0)),
        out_shape=jax.ShapeDtypeStruct((bsz, t, FNET_W), F32),
        compiler_params=_cparams("arbitrary"), name="fnet_short",
    )(u, c, s)


def _odd_post_kernel(x_ref, fr_ref, sf_ref, ys_ref, gate_ref, wo_ref, lng_ref, lnb_ref, o_ref):
    yf = (fr_ref[0] * sf_ref[0].astype(F32)).astype(BF16)
    y = _dot(yf, wo_ref[0:FNET_W, :]) + _dot(ys_ref[0], wo_ref[FNET_W:, :])
    o_ref[0] = _post_tail(x_ref[0], y, gate_ref[0], lng_ref[...], lnb_ref[...])


def _odd_post(x, fr, sf, ys, gate, wo, lng, lnb, tm):
    bsz, t, d = x.shape
    tok = lambda w: pl.BlockSpec((1, tm, w), lambda b, i: (b, i, 0))
    return pl.pallas_call(
        _odd_post_kernel, grid=(bsz, t // tm),
        in_specs=[tok(d), tok(FNET_W), tok(FNET_W), tok(SGU_W),
                  pl.BlockSpec((1, 1, d), lambda b, i: (b, 0, 0)),
                  _full(wo.shape), _full(lng.shape), _full(lnb.shape)],
        out_specs=tok(d), out_shape=jax.ShapeDtypeStruct((bsz, t, d), F32),
        compiler_params=_cparams("arbitrary", "arbitrary"), name="odd_post",
    )(x, fr, sf, ys, gate, wo, lng, lnb)


def _even_weights(w_in, gla_w2, gla_b, q_norm_g, w_uq, kv_norm_g, w_ukv):
    d = w_in.shape[0]
    idx = np.cumsum(EVEN_IN_SIZES)[:-1].tolist()
    gq, gk, gv, glr, gg, cq, ckv, kr, mg = jnp.split(w_in, idx, axis=1)
    half = MLA_ROPE // 2
    z = lambda n: jnp.zeros((d, n), w_in.dtype)
    lr_pad = jnp.concatenate([glr, z(128 - 2 * GLA_GATE_RANK)], axis=1)
    kra = jnp.concatenate([z(MLA_NOPE), kr, z(MLA_HEAD_PAD - MLA_NOPE - MLA_ROPE)], axis=1)
    krb = jnp.concatenate([z(MLA_NOPE), kr[:, half:], kr[:, :half], z(MLA_HEAD_PAD - MLA_NOPE - MLA_ROPE)], axis=1)
    w = jnp.concatenate([gq * GLA_DK ** -0.5, gk, gv, gg, cq, ckv, mg, lr_pad, kra, krb], axis=1).astype(BF16)
    w2 = jnp.zeros((128, 2 * GLA_QK_W), F32)
    w2 = w2.at[0:GLA_GATE_RANK, 0:GLA_QK_W].set(gla_w2[0])
    w2 = w2.at[GLA_GATE_RANK:2 * GLA_GATE_RANK, GLA_QK_W:].set(gla_w2[1])
    gb = jnp.concatenate([gla_b[0], gla_b[1]])[None, :]
    uq = w_uq.reshape(MLA_Q_RANK, MLA_HEADS, MLA_NOPE + MLA_ROPE)
    pad = MLA_HEAD_PAD - MLA_NOPE - MLA_ROPE
    zq = lambda n: jnp.zeros((MLA_Q_RANK, MLA_HEADS, n), w_uq.dtype)
    wqa = jnp.concatenate([uq, zq(pad)], axis=2).reshape(MLA_Q_RANK, -1)
    wqb = jnp.concatenate([zq(MLA_NOPE), uq[:, :, MLA_NOPE + half:], uq[:, :, MLA_NOPE:MLA_NOPE + half], zq(pad)],
                          axis=2).reshape(MLA_Q_RANK, -1)
    ukv = w_ukv.reshape(MLA_KV_RANK, MLA_HEADS, MLA_NOPE + MLA_V)
    wk = jnp.concatenate([ukv[:, :, :MLA_NOPE], jnp.zeros((MLA_KV_RANK, MLA_HEADS, MLA_HEAD_PAD - MLA_NOPE), w_ukv.dtype)],
                         axis=2).reshape(MLA_KV_RANK, -1)
    vpad = MLA_VT_ROWS - MLA_V
    wv = jnp.concatenate([ukv[:, :, MLA_NOPE:], jnp.zeros((MLA_KV_RANK, MLA_HEADS, vpad), w_ukv.dtype)],
                         axis=2).reshape(MLA_KV_RANK, -1)
    vbias = np.zeros((MLA_HEADS, MLA_VT_ROWS), np.float32)
    vbias[:, MLA_V] = 1.0
    return (w, w2.astype(BF16), gb, q_norm_g[None, :], wqa.astype(BF16), wqb.astype(BF16),
            kv_norm_g[None, :], wk.astype(BF16), wv.astype(BF16), jnp.asarray(vbias.reshape(1, -1)))


def _rope_tables(n):
    row = jnp.repeat(jnp.arange(n // GRID_W, dtype=F32), GRID_W)
    col = (jnp.arange(n) % GRID_W).astype(F32)
    n_freq = MLA_ROPE // 4
    inv = ROPE_BASE ** (-jnp.arange(n_freq, dtype=F32) / n_freq)
    ang = jnp.concatenate([row[:, None] * inv, col[:, None] * inv], -1)
    cos, sin = jnp.cos(ang), jnp.sin(ang)
    pad = jnp.zeros((n, MLA_HEAD_PAD - MLA_NOPE - MLA_ROPE), F32)
    zn = jnp.zeros((n, MLA_NOPE), F32)
    ck = jnp.concatenate([zn, cos, cos, pad], axis=1)
    sk = jnp.concatenate([zn, -sin, sin, pad], axis=1)
    return ck, sk


def _plain_tables(n):
    pad = jnp.zeros((n, MLA_HEAD_PAD - MLA_NOPE - MLA_ROPE), F32)
    zn = jnp.zeros((n, MLA_NOPE), F32)
    ck = jnp.concatenate([zn, jnp.ones((n, MLA_ROPE), F32), pad], axis=1)
    return ck, jnp.zeros((n, MLA_HEAD_PAD), F32)


def _pick(t, pref):
    return pref if t % pref == 0 else t


def kernel(x, c, ctx, c_ctx, ada_w, ada_b, post_ln_g, post_ln_b, even_w_in, gla_w2, gla_b, gla_norm_g,
           mla_q_norm_g, mla_w_uq, mla_kv_norm_g, mla_w_ukv, even_w_out, odd_w_in, sgu_w, sgu_b, odd_w_out):
    bsz, n, d = x.shape
    lc = ctx.shape[1]
    depth = ada_w.shape[0]
    assert bsz + 1 <= 8 and n % (FFT_N1 * FFT_T2_BLK) == 0 and n % 512 == 0 and lc % MLA_TK == 0

    cond = jnp.concatenate([c, c_ctx[None, :], jnp.zeros((8 - bsz - 1, d), F32)], axis=0)
    mods = _mods(cond, ada_w, ada_b)

    def lat_mod(l, j):
        return mods[l, :bsz, j * d:(j + 1) * d][:, None, :]

    def ctx_mod(l, j):
        return jnp.broadcast_to(mods[l, bsz, j * d:(j + 1) * d][None, None, :], (bsz, 1, d))

    rope_tabs = _rope_tables(n)
    ctx_tabs = _plain_tables(lc)
    dft_c = _channel_dft_matrix()
    tm_lat, tm_ctx = _pick(n, 512), _pick(lc, 256)
    zero_state = jnp.zeros((bsz, 2, GLA_QK_W, GLA_DV), F32)

    for l in range(depth):
        need_ctx_out = any(j % 2 == 0 for j in range(l + 1, depth))
        i = l // 2
        lng, lnb = post_ln_g[l][None, :], post_ln_b[l][None, :]
        if l % 2 == 0:
            wts = _even_weights(even_w_in[i], gla_w2[i], gla_b[i], mla_q_norm_g[i], mla_w_uq[i],
                                mla_kv_norm_g[i], mla_w_ukv[i])
            wo = even_w_out[i].astype(BF16)
            gng = gla_norm_g[i][None, :]
            pc = _even_pre(ctx, ctx_mod(l, 0), ctx_mod(l, 1), wts, ctx_tabs, tm_ctx, n + lc, n)
            q_c, k_c, v_c, g_c, sgg_c, smg_c, qt_c, kk, vt = pc
            pz = _even_pre(x, lat_mod(l, 0), lat_mod(l, 1), wts, rope_tabs, tm_lat, n + lc, 0, (kk, vt))
            q_l, k_l, v_l, g_l, sgg_l, smg_l, qt_l, kk, vt = pz
            of_c, ob_c, s_c = _gla(q_c, k_c, v_c, g_c, zero_state, _pick(lc, 256))
            of_l, ob_l, _ = _gla(q_l, k_l, v_l, g_l, s_c, _pick(n, 256))
            ot_l = _mla(qt_l, kk, vt, _pick(n, 512), MLA_TK, 0, n + lc)
            x_new = _even_post(x, of_l, ob_l, sgg_l, ot_l, smg_l, lat_mod(l, 2), wo, gng, lng, lnb, tm_lat)
            if need_ctx_out:
                ot_c = _mla(qt_c, kk, vt, lc, MLA_TK, n, lc)
                ctx = _even_post(ctx, of_c, ob_c, sgg_c, ot_c, smg_c, ctx_mod(l, 2), wo, gng, lng, lnb, tm_ctx)
            x = x_new
        else:
            w = odd_w_in[i].astype(BF16)
            wo = odd_w_out[i].astype(BF16)
            sw = sgu_w[i].astype(BF16)
            sb = jnp.broadcast_to(sgu_b[i][:, :, None], (SGU_GROUPS, SGU_CHUNK, SGU_GROUP_CH))
            u, sf, ys = _odd_pre(x, lat_mod(l, 0), lat_mod(l, 1), w, dft_c, sw, sb, tm_lat)
            x_new = _odd_post(x, _fnet_long(u), sf, ys, lat_mod(l, 2), wo, lng, lnb, tm_lat)
            if need_ctx_out:
                u, sf, ys = _odd_pre(ctx, ctx_mod(l, 0), ctx_mod(l, 1), w, dft_c, sw, sb, tm_ctx)
                ctx = _odd_post(ctx, _fnet_short(u), sf, ys, ctx_mod(l, 2), wo, lng, lnb, tm_ctx)
            x = x_new
    return x
```

```python
import functools
import math

import numpy as np
import jax
import jax.numpy as jnp
from jax import lax
from jax.experimental import pallas as pl
from jax.experimental.pallas import tpu as pltpu

F32 = jnp.float32
BF16 = jnp.bfloat16

DEPTH = 4
GRID_W = 64
DEEPNORM_ALPHA = (2 * DEPTH) ** 0.25
LN_EPS = 1e-6

GLA_HEADS = 4
GLA_DK = 64
GLA_DV = 128
GLA_QK_W = GLA_HEADS * GLA_DK
GLA_V_W = GLA_HEADS * GLA_DV
GLA_GATE_RANK = 16
GLA_TAU = 16.0
GLA_CHUNK = 64

MLA_HEADS = 8
MLA_NOPE = 64
MLA_ROPE = 32
MLA_V = 64
MLA_Q_RANK = 256
MLA_KV_RANK = 128
MLA_V_W = MLA_HEADS * MLA_V
MLA_SCALE = (MLA_NOPE + MLA_ROPE) ** -0.5
ROPE_BASE = 10000.0
MLA_HEAD_PAD = 128
MLA_VT_ROWS = MLA_V + 16
MLA_VT_W = MLA_HEADS * MLA_VT_ROWS

FNET_GROUPS = 4
FNET_GROUP_CH = 128
FNET_W = FNET_GROUPS * FNET_GROUP_CH
FFT_N1 = 64

SGU_GROUPS = 4
SGU_GROUP_CH = 128
SGU_W = SGU_GROUPS * SGU_GROUP_CH
SGU_CHUNK = 128

EVEN_IN_SIZES = (GLA_QK_W, GLA_QK_W, GLA_V_W, 2 * GLA_GATE_RANK, GLA_V_W,
                 MLA_Q_RANK, MLA_KV_RANK, MLA_ROPE, MLA_V_W)

E_GQ, E_GK, E_GV, E_GG, E_CQ, E_CKV, E_MG, E_LR, E_KRA, E_KRB, E_END = (
    0, 256, 512, 1024, 1536, 1792, 1920, 2432, 2560, 2688, 2816)

VMEM_LIMIT_BYTES = 56 * 1024 * 1024
LOG2E = math.log2(math.e)
NEG_BIG = -1e30


def _cparams(*sem):
    return pltpu.CompilerParams(dimension_semantics=sem, vmem_limit_bytes=VMEM_LIMIT_BYTES)


def _dot(a, b):
    return jnp.dot(a, b, preferred_element_type=F32)


def _dot_nt(a, b):
    return lax.dot_general(a, b, (((1,), (1,)), ((), ())), preferred_element_type=F32)


def _dot_tn(a, b):
    return lax.dot_general(a, b, (((0,), (0,)), ((), ())), preferred_element_type=F32)


def _ln(x):
    xc = x - jnp.mean(x, -1, keepdims=True)
    return xc * lax.rsqrt(jnp.mean(xc * xc, -1, keepdims=True) + LN_EPS)


def _rms(x):
    return x * lax.rsqrt(jnp.mean(x * x, -1, keepdims=True) + LN_EPS)


def _silu(x):
    return x / (1.0 + jnp.exp(-x))


def _gelu(x):
    return 0.5 * x * (1.0 + lax.erf(x * (2.0 ** -0.5)))


def _tile_lanes(x, reps):
    return jnp.concatenate([x] * reps, axis=1)


def _full(shape):
    n = len(shape)
    return pl.BlockSpec(shape, lambda *_: (0,) * n)


def _layer(arr, i):
    n = arr.ndim
    return pl.BlockSpec((None,) + arr.shape[1:], lambda *_: (i,) + (0,) * (n - 1))


def _mods_kernel(cond_ref, w_ref, b_ref, o_ref):
    s = _silu(cond_ref[...])
    o_ref[0] = jnp.dot(s, w_ref[0], preferred_element_type=F32,
                       precision=lax.Precision.HIGHEST) + b_ref[0]


def _mods(cond, ada_w, ada_b):
    depth, d, d3 = ada_w.shape
    nb = d3 // d
    return pl.pallas_call(
        _mods_kernel,
        grid=(depth, nb),
        in_specs=[pl.BlockSpec((8, d), lambda l, j: (0, 0)),
                  pl.BlockSpec((1, d, d), lambda l, j: (l, 0, j)),
                  pl.BlockSpec((1, 1, d), lambda l, j: (l, 0, j))],
        out_specs=pl.BlockSpec((1, 8, d), lambda l, j: (l, 0, j)),
        out_shape=jax.ShapeDtypeStruct((depth, 8, d3), F32),
        compiler_params=_cparams("arbitrary", "arbitrary"),
        name="ada_mod",
    )(cond, ada_w, ada_b.reshape(depth, 1, d3))


def _even_pre_kernel(x_ref, shift_ref, scale_ref, w_ref, w2_ref, gb_ref, qg_ref, wqa_ref, wqb_ref,
                     kvg_ref, wk_ref, wv_ref, vb_ref, ck_ref, sk_ref, *rest):
    q_ref, k_ref, v_ref, g_ref, sgg_ref, smg_ref, qt_ref, kk_ref, vt_ref = rest[-9:]
    h = _ln(x_ref[0]) * (1.0 + scale_ref[0]) + shift_ref[0]
    z = _dot(h.astype(BF16), w_ref[...])
    q_ref[0] = z[:, E_GQ:E_GK].astype(BF16)
    k_ref[0] = z[:, E_GK:E_GV].astype(BF16)
    v_ref[0] = z[:, E_GV:E_GG].astype(BF16)
    sgg_ref[0] = _silu(z[:, E_GG:E_CQ]).astype(BF16)
    smg_ref[0] = _silu(z[:, E_MG:E_LR]).astype(BF16)
    pre = _dot(z[:, E_LR:E_KRA].astype(BF16), w2_ref[...]) + gb_ref[...]
    g_ref[0] = jax.nn.log_sigmoid(pre) * (1.0 / GLA_TAU)
    cqn = (_rms(z[:, E_CQ:E_CKV]) * qg_ref[...]).astype(BF16)
    lane = lax.broadcasted_iota(jnp.int32, (1, MLA_HEAD_PAD), 1)
    q_scale = MLA_SCALE * LOG2E
    cq = ck_ref[...] * q_scale + jnp.where(lane < MLA_NOPE, q_scale, 0.0)
    sq = sk_ref[...] * q_scale
    qfull = (_dot(cqn, wqa_ref[...]) * _tile_lanes(cq, MLA_HEADS)
             + _dot(cqn, wqb_ref[...]) * _tile_lanes(sq, MLA_HEADS))
    qt_ref[0] = qfull.T.astype(BF16)
    ckvn = (_rms(z[:, E_CKV:E_MG]) * kvg_ref[...]).astype(BF16)
    kr = z[:, E_KRA:E_KRB] * ck_ref[...] + z[:, E_KRB:E_END] * sk_ref[...]
    kk_ref[0] = (_dot(ckvn, wk_ref[...]) + _tile_lanes(kr, MLA_HEADS)).astype(BF16)
    vt_ref[0] = (_dot(ckvn, wv_ref[...]) + vb_ref[...]).T.astype(BF16)


def _even_pre(x, shift, scale, wts, li, tabs, tm, n_keys, key_lo, kv_bufs=None):
    bsz, t, d = x.shape
    hq = MLA_HEADS * MLA_HEAD_PAD
    assert key_lo % tm == 0
    ko = key_lo // tm
    tok = lambda w: pl.BlockSpec((1, tm, w), lambda b, i: (b, i, 0))
    tab = pl.BlockSpec((tm, MLA_HEAD_PAD), lambda b, i: (i, 0))
    mod = pl.BlockSpec((1, 1, d), lambda b, i: (b, 0, 0))
    out_shape = (
        jax.ShapeDtypeStruct((bsz, t, GLA_QK_W), BF16),
        jax.ShapeDtypeStruct((bsz, t, GLA_QK_W), BF16),
        jax.ShapeDtypeStruct((bsz, t, GLA_V_W), BF16),
        jax.ShapeDtypeStruct((bsz, t, 2 * GLA_QK_W), F32),
        jax.ShapeDtypeStruct((bsz, t, GLA_V_W), BF16),
        jax.ShapeDtypeStruct((bsz, t, MLA_V_W), BF16),
        jax.ShapeDtypeStruct((bsz, hq, t), BF16),
        jax.ShapeDtypeStruct((bsz, n_keys, hq), BF16),
        jax.ShapeDtypeStruct((bsz, MLA_VT_W, n_keys), BF16),
    )
    out_specs = (tok(GLA_QK_W), tok(GLA_QK_W), tok(GLA_V_W), tok(2 * GLA_QK_W), tok(GLA_V_W), tok(MLA_V_W),
                 pl.BlockSpec((1, hq, tm), lambda b, i: (b, 0, i)),
                 pl.BlockSpec((1, tm, hq), lambda b, i: (b, ko + i, 0)),
                 pl.BlockSpec((1, MLA_VT_W, tm), lambda b, i: (b, 0, ko + i)))
    in_specs = [tok(d), mod, mod] + [_layer(w, li) for w in wts] + [tab] * 2
    args = [x, shift, scale, *wts, *tabs]
    aliases = {}
    if kv_bufs is not None:
        aliases = {len(args): 7, len(args) + 1: 8}
        in_specs += [pl.BlockSpec(memory_space=pl.ANY)] * 2
        args += list(kv_bufs)
    return pl.pallas_call(
        _even_pre_kernel, grid=(bsz, t // tm), in_specs=in_specs, out_specs=out_specs, out_shape=out_shape,
        input_output_aliases=aliases,
        compiler_params=_cparams("arbitrary", "arbitrary"), name="even_pre",
    )(*args)


def _gla_block(q, k, v, g, tmat, same_tri, mid_off, end_off, head_masks, chunk_masks):
    r = q.shape[0]
    c = GLA_CHUNK
    nc = r // c
    g_hi = g.astype(BF16)
    g_lo = (g - g_hi.astype(F32)).astype(BF16)
    b = _dot(tmat, g_hi) + _dot(tmat, g_lo)
    mid_rows = [b[j * c + mid_off:j * c + mid_off + 1] for j in range(nc)]
    end_rows = [b[j * c + end_off:j * c + end_off + 1] for j in range(nc)]
    per_chunk = lambda rows: jnp.concatenate([jnp.broadcast_to(x, (c, GLA_QK_W)) for x in rows], axis=0)
    b_mid = per_chunk(mid_rows)
    e_up = jnp.exp(b - b_mid)
    e_dn = jnp.exp(b_mid - b)
    qe = q * e_up
    ke = (k * e_dn).astype(BF16)
    kd_t = (k * (e_dn * per_chunk([jnp.exp(e - m) for e, m in zip(end_rows, mid_rows)]))).T
    qb = q * (e_up * per_chunk([jnp.exp(m) for m in mid_rows]))
    stack = lambda a: jnp.concatenate([jnp.where(m, a, 0.0) for m in head_masks], axis=0).astype(BF16)
    att = _dot_nt(stack(qe), ke)
    qb_st = stack(qb)
    dec_t = jnp.exp(jnp.concatenate(end_rows + [jnp.zeros((8 - nc, GLA_QK_W), F32)], axis=0)).T
    intra, ds = [], []
    for h in range(GLA_HEADS):
        a_h = jnp.where(same_tri, att[h * r:(h + 1) * r], 0.0).astype(BF16)
        v_h = v[:, h * GLA_DV:(h + 1) * GLA_DV]
        intra.append(_dot(a_h, v_h))
        kd_h = kd_t[h * GLA_DK:(h + 1) * GLA_DK]
        lhs = jnp.concatenate([jnp.where(cm, kd_h, 0.0) for cm in chunk_masks], axis=0).astype(BF16)
        ds.append(_dot(lhs, v_h))
    return intra, ds, qb_st, dec_t


def _gla_recur(par, s, o_ref, row0, reverse):
    intra, ds, qb_st, dec_t = par
    c = GLA_CHUNK
    r = intra[0].shape[0]
    nc = r // c
    for j in (reversed(range(nc)) if reverse else range(nc)):
        rows = slice(j * c, (j + 1) * c)
        st = jnp.concatenate([qb_st[h * r + j * c:h * r + (j + 1) * c] for h in range(GLA_HEADS)], axis=0)
        o_inter = _dot(st, s.astype(BF16))
        for h in range(GLA_HEADS):
            o_ref[0, row0 + j * c:row0 + (j + 1) * c, h * GLA_DV:(h + 1) * GLA_DV] = (
                intra[h][rows] + o_inter[h * c:(h + 1) * c]).astype(o_ref.dtype)
        dec = jnp.broadcast_to(dec_t[:, j:j + 1], (GLA_QK_W, GLA_DV))
        s = dec * s + jnp.concatenate([ds[h][rows] for h in range(GLA_HEADS)], axis=0)
    return s


GLA_SUB = 256


def _gla_kernel(qf_ref, kf_ref, vf_ref, gf_ref, qb_ref, kb_ref, vb_ref, gb_ref, s0_ref, tl_ref, tu_ref,
                of_ref, ob_ref, sfin_ref, s_scr, *, tb):
    i = pl.program_id(1)

    @pl.when(i == 0)
    def _():
        s_scr[...] = s0_ref[0]

    c, r = GLA_CHUNK, GLA_SUB
    row = lax.broadcasted_iota(jnp.int32, (r, r), 0)
    col = lax.broadcasted_iota(jnp.int32, (r, r), 1)
    same = (row // c) == (col // c)
    lane_head = lax.broadcasted_iota(jnp.int32, (r, GLA_QK_W), 1) // GLA_DK
    head_masks = [lane_head == h for h in range(GLA_HEADS)]
    lane_chunk = lax.broadcasted_iota(jnp.int32, (GLA_DK, r), 1) // c
    chunk_masks = [lane_chunk == j for j in range(r // c)]

    par_f, par_b = [], []
    for u in range(tb // r):
        rs = slice(u * r, (u + 1) * r)
        par_f.append(_gla_block(qf_ref[0, rs].astype(F32), kf_ref[0, rs].astype(F32), vf_ref[0, rs], gf_ref[0, rs],
                                tl_ref[...], same & (col <= row), c // 2 - 1, c - 1, head_masks, chunk_masks))
        par_b.append(_gla_block(qb_ref[0, rs].astype(F32), kb_ref[0, rs].astype(F32), vb_ref[0, rs], gb_ref[0, rs],
                                tu_ref[...], same & (col >= row), c // 2, 0, head_masks, chunk_masks))
    s = s_scr[0]
    for u in range(tb // r):
        s = _gla_recur(par_f[u], s, of_ref, u * r, False)
    s_scr[0] = s
    s = s_scr[1]
    for u in reversed(range(tb // r)):
        s = _gla_recur(par_b[u], s, ob_ref, u * r, True)
    s_scr[1] = s

    @pl.when(i == pl.num_programs(1) - 1)
    def _():
        sfin_ref[0] = s_scr[...]


def _gla(q, k, v, g, s0, tb):
    bsz, t, _ = q.shape
    nblk = t // tb
    assert tb % GLA_SUB == 0
    idx = np.arange(GLA_SUB)
    same = (idx[:, None] // GLA_CHUNK) == (idx[None, :] // GLA_CHUNK)
    tl = jnp.asarray(same & (idx[None, :] <= idx[:, None]), BF16)
    tu = jnp.asarray(same & (idx[None, :] >= idx[:, None]), BF16)
    fwd = lambda w: pl.BlockSpec((1, tb, w), lambda b, i: (b, i, 0))
    bwd = lambda w: pl.BlockSpec((1, tb, w), lambda b, i: (b, nblk - 1 - i, 0))
    st = pl.BlockSpec((1, 2, GLA_QK_W, GLA_DV), lambda b, i: (b, 0, 0, 0))
    return pl.pallas_call(
        functools.partial(_gla_kernel, tb=tb),
        grid=(bsz, nblk),
        in_specs=[fwd(GLA_QK_W), fwd(GLA_QK_W), fwd(GLA_V_W),
                  pl.BlockSpec((1, tb, GLA_QK_W), lambda b, i: (b, i, 0)),
                  bwd(GLA_QK_W), bwd(GLA_QK_W), bwd(GLA_V_W),
                  pl.BlockSpec((1, tb, GLA_QK_W), lambda b, i: (b, nblk - 1 - i, 1)),
                  st, _full((GLA_SUB, GLA_SUB)), _full((GLA_SUB, GLA_SUB))],
        out_specs=(fwd(GLA_V_W), bwd(GLA_V_W), st),
        out_shape=(jax.ShapeDtypeStruct((bsz, t, GLA_V_W), BF16),
                   jax.ShapeDtypeStruct((bsz, t, GLA_V_W), BF16),
                   jax.ShapeDtypeStruct((bsz, 2, GLA_QK_W, GLA_DV), F32)),
        scratch_shapes=[pltpu.VMEM((2, GLA_QK_W, GLA_DV), F32)],
        compiler_params=_cparams("arbitrary", "arbitrary"), name="gla_scan",
    )(q, k, v, g, q, k, v, g, s0, tl, tu)


MLA_SLOTS = 3
MLA_TK = 256


def _mla_kernel(qt_ref, k_ref, vt_ref, o_ref, *scratch, n_chunks, tk):
    q_t = qt_ref[0]
    tq = q_t.shape[1]
    s_bufs, p_bufs = scratch[:MLA_SLOTS], scratch[MLA_SLOTS:]
    cmax = None
    m = jnp.full((1, tq), NEG_BIG, F32)
    alpha = None
    acc = jnp.zeros((MLA_VT_ROWS, tq), F32)
    for t in range(n_chunks + 2):
        alpha_prev = alpha
        if 1 <= t <= n_chunks:
            e = (t - 1) % MLA_SLOTS
            m_new = jnp.maximum(m, jnp.max(cmax, axis=0, keepdims=True))
            alpha = jnp.exp2(m - m_new)
            m = m_new
            p_bufs[e][...] = jnp.exp2(s_bufs[e][...] - m).astype(BF16)
        if t < n_chunks:
            s_t = _dot(k_ref[0, t * tk:(t + 1) * tk, :], q_t)
            s_bufs[t % MLA_SLOTS][...] = s_t
            cmax = jnp.max(s_t.reshape(tk // 8, 8, tq), axis=0)
        if t >= 2:
            vt = vt_ref[0, :, (t - 2) * tk:(t - 1) * tk]
            acc = alpha_prev * acc + _dot(vt, p_bufs[(t - 2) % MLA_SLOTS][...])
    o_ref[0] = (acc[0:MLA_V] / acc[MLA_V:MLA_V + 1]).astype(o_ref.dtype)


def _mla(qt, k, vt, tq, tk, key_lo, n_keys):
    bsz, _, t = qt.shape
    assert n_keys % tk == 0 and t % tq == 0 and key_lo % n_keys == 0
    kb = key_lo // n_keys
    return pl.pallas_call(
        functools.partial(_mla_kernel, n_chunks=n_keys // tk, tk=tk),
        grid=(bsz, MLA_HEADS, t // tq),
        in_specs=[pl.BlockSpec((1, MLA_HEAD_PAD, tq), lambda b, h, i: (b, h, i)),
                  pl.BlockSpec((1, n_keys, MLA_HEAD_PAD), lambda b, h, i: (b, kb, h)),
                  pl.BlockSpec((1, MLA_VT_ROWS, n_keys), lambda b, h, i: (b, h, kb))],
        out_specs=pl.BlockSpec((1, MLA_V, tq), lambda b, h, i: (b, h, i)),
        out_shape=jax.ShapeDtypeStruct((bsz, MLA_V_W, t), BF16),
        scratch_shapes=[pltpu.VMEM((tk, tq), F32)] * MLA_SLOTS + [pltpu.VMEM((tk, tq), BF16)] * MLA_SLOTS,
        compiler_params=_cparams("arbitrary", "arbitrary", "arbitrary"), name="mla_attn",
    )(qt, k, vt)


def _post_tail(x, y, gate, lng, lnb):
    return _ln(DEEPNORM_ALPHA * x + gate * y) * lng + lnb


def _even_post_kernel(x_ref, of_ref, ob_ref, sgg_ref, ot_ref, smg_ref, gate_ref, wo_ref, gng_ref,
                      lng_ref, lnb_ref, o_ref):
    o = of_ref[0].astype(F32) + ob_ref[0].astype(F32)
    parts = [_rms(o[:, h * GLA_DV:(h + 1) * GLA_DV]) * gng_ref[...] for h in range(GLA_HEADS)]
    yg = jnp.concatenate(parts, axis=1) * sgg_ref[0].astype(F32)
    ym = ot_ref[0].astype(F32).T * smg_ref[0].astype(F32)
    y = _dot(yg.astype(BF16), wo_ref[0:GLA_V_W, :]) + _dot(ym.astype(BF16), wo_ref[GLA_V_W:, :])
    o_ref[0] = _post_tail(x_ref[0], y, gate_ref[0], lng_ref[...], lnb_ref[...])


def _even_post(x, o_f, o_b, sgg, o_t, smg, gate, wo, gng, lng, lnb, li, l, tm):
    bsz, t, d = x.shape
    tok = lambda w: pl.BlockSpec((1, tm, w), lambda b, i: (b, i, 0))
    return pl.pallas_call(
        _even_post_kernel, grid=(bsz, t // tm),
        in_specs=[tok(d), tok(GLA_V_W), tok(GLA_V_W), tok(GLA_V_W),
                  pl.BlockSpec((1, MLA_V_W, tm), lambda b, i: (b, 0, i)), tok(MLA_V_W),
                  pl.BlockSpec((1, 1, d), lambda b, i: (b, 0, 0)),
                  _layer(wo, li), _layer(gng, li), _layer(lng, l), _layer(lnb, l)],
        out_specs=tok(d), out_shape=jax.ShapeDtypeStruct((bsz, t, d), F32),
        compiler_params=_cparams("arbitrary", "arbitrary"), name="even_post",
    )(x, o_f, o_b, sgg, o_t, smg, gate, wo, gng, lng, lnb)


def _odd_pre_kernel(x_ref, shift_ref, scale_ref, w_ref, dft_ref, sw_ref, sb_ref, u_ref, sf_ref, ys_ref):
    h = _ln(x_ref[0]) * (1.0 + scale_ref[0]) + shift_ref[0]
    z = _dot(h.astype(BF16), w_ref[...])
    for g in range(FNET_GROUPS):
        cs = slice(g * FNET_GROUP_CH, (g + 1) * FNET_GROUP_CH)
        ab = _dot(z[:, cs].astype(BF16), dft_ref[...])
        u_ref[0, :, cs] = ab[:, 0:FNET_GROUP_CH].astype(BF16)
        u_ref[0, :, FNET_W + g * FNET_GROUP_CH:FNET_W + (g + 1) * FNET_GROUP_CH] = ab[:, FNET_GROUP_CH:].astype(BF16)
    sf_ref[0] = _silu(z[:, FNET_W:2 * FNET_W]).astype(BF16)
    o0 = 2 * FNET_W
    tm = z.shape[0]
    for g in range(SGU_GROUPS):
        cs = slice(g * SGU_GROUP_CH, (g + 1) * SGU_GROUP_CH)
        ug = _gelu(z[:, o0 + g * SGU_GROUP_CH:o0 + (g + 1) * SGU_GROUP_CH])
        vg = _ln(_gelu(z[:, o0 + SGU_W + g * SGU_GROUP_CH:o0 + SGU_W + (g + 1) * SGU_GROUP_CH])).astype(BF16)
        sg = _silu(z[:, o0 + 2 * SGU_W + g * SGU_GROUP_CH:o0 + 2 * SGU_W + (g + 1) * SGU_GROUP_CH])
        for c in range(tm // SGU_CHUNK):
            rs = slice(c * SGU_CHUNK, (c + 1) * SGU_CHUNK)
            sv = _dot(sw_ref[g], vg[rs]) + sb_ref[g]
            ys_ref[0, rs, cs] = (ug[rs] * sv * sg[rs]).astype(BF16)


def _odd_pre(x, shift, scale, w, dft, sw, sb, li, tm):
    bsz, t, d = x.shape
    tok = lambda w_: pl.BlockSpec((1, tm, w_), lambda b, i: (b, i, 0))
    mod = pl.BlockSpec((1, 1, d), lambda b, i: (b, 0, 0))
    return pl.pallas_call(
        _odd_pre_kernel, grid=(bsz, t // tm),
        in_specs=[tok(d), mod, mod, _layer(w, li), _full(dft.shape), _layer(sw, li), _layer(sb, li)],
        out_specs=(tok(2 * FNET_W), tok(FNET_W), tok(SGU_W)),
        out_shape=(jax.ShapeDtypeStruct((bsz, t, 2 * FNET_W), BF16),
                   jax.ShapeDtypeStruct((bsz, t, FNET_W), BF16),
                   jax.ShapeDtypeStruct((bsz, t, SGU_W), BF16)),
        compiler_params=_cparams("arbitrary", "arbitrary"), name="odd_pre",
    )(x, shift, scale, w, dft, sw, sb)


FFT_T2_BLK = 16
FFT_P1_BLK = 8


def _fft1_kernel(u_ref, w1_ref, tc_ref, ts_ref, z_ref):
    n1, tb = FFT_N1, FFT_T2_BLK
    r = n1 * tb
    pq = _dot(w1_ref[...], u_ref[0].reshape(r, 2 * FNET_W))
    zr = pq[0:r, 0:FNET_W] - pq[r:, FNET_W:]
    zi = -pq[0:r, FNET_W:] - pq[r:, 0:FNET_W]
    tc = _tile_lanes(tc_ref[0], FNET_W // 128)
    ts = _tile_lanes(ts_ref[0], FNET_W // 128)
    z_ref[0, :, :, 0:FNET_W] = (zr * tc + zi * ts).astype(BF16).reshape(n1, tb, FNET_W)
    z_ref[0, :, :, FNET_W:] = (zi * tc - zr * ts).astype(BF16).reshape(n1, tb, FNET_W)


def _fft2_kernel(z_ref, c2_ref, s2_ref, y_ref):
    for j in range(FFT_P1_BLK):
        zp = z_ref[0, j]
        y_ref[0, :, j, :] = _dot(c2_ref[...], zp[:, 0:FNET_W]) + _dot(s2_ref[...], zp[:, FNET_W:])


def _dft_tables(t):
    n1, n2 = FFT_N1, t // FFT_N1
    p1 = np.arange(n1, dtype=np.float64)
    a1 = 2.0 * np.pi * np.outer(p1, p1) / n1
    eye = np.eye(FFT_T2_BLK)
    w1 = np.concatenate([np.kron(np.cos(a1), eye), np.kron(np.sin(a1), eye)], axis=0)
    at = 2.0 * np.pi * np.outer(p1, np.arange(n2, dtype=np.float64)) / t

    def twiddle(a):
        a = a.reshape(n1, n2 // FFT_T2_BLK, FFT_T2_BLK).transpose(1, 0, 2).reshape(n2 // FFT_T2_BLK, -1)
        return np.repeat(a[:, :, None], 128, axis=2)
    tc, ts = twiddle(np.cos(at)), twiddle(np.sin(at))
    p2 = np.arange(n2, dtype=np.float64)
    a2 = 2.0 * np.pi * np.outer(p2, p2) / n2
    norm = 1.0 / math.sqrt(t * FNET_GROUP_CH)
    return (jnp.asarray(w1, F32).astype(BF16), jnp.asarray(tc, F32), jnp.asarray(ts, F32),
            jnp.asarray(np.cos(a2) * norm, F32).astype(BF16), jnp.asarray(np.sin(a2) * norm, F32).astype(BF16))


def _channel_dft_matrix():
    d = np.arange(FNET_GROUP_CH, dtype=np.float64)
    a = 2.0 * np.pi * np.outer(d, d) / FNET_GROUP_CH
    return jnp.asarray(np.concatenate([np.cos(a), np.sin(a)], axis=1), F32).astype(BF16)


def _fnet_long(u):
    bsz, t, w2 = u.shape
    n1, n2 = FFT_N1, t // FFT_N1
    w1, tc, ts, c2, s2 = _dft_tables(t)
    blk = pl.BlockSpec((1, n1, FFT_T2_BLK, w2), lambda b, i: (b, 0, i, 0))
    tw = pl.BlockSpec((1, n1 * FFT_T2_BLK, 128), lambda b, i: (i, 0, 0))
    z = pl.pallas_call(
        _fft1_kernel, grid=(bsz, n2 // FFT_T2_BLK),
        in_specs=[blk, _full(w1.shape), tw, tw],
        out_specs=blk,
        out_shape=jax.ShapeDtypeStruct((bsz, n1, n2, w2), BF16),
        compiler_params=_cparams("arbitrary", "arbitrary"), name="fnet_stage1",
    )(u.reshape(bsz, n1, n2, w2), w1, tc, ts)
    y = pl.pallas_call(
        _fft2_kernel, grid=(bsz, n1 // FFT_P1_BLK),
        in_specs=[pl.BlockSpec((1, FFT_P1_BLK, n2, w2), lambda b, i: (b, i, 0, 0)),
                  _full(c2.shape), _full(s2.shape)],
        out_specs=pl.BlockSpec((1, n2, FFT_P1_BLK, FNET_W), lambda b, i: (b, 0, i, 0)),
        out_shape=jax.ShapeDtypeStruct((bsz, n2, n1, FNET_W), F32),
        compiler_params=_cparams("arbitrary", "arbitrary"), name="fnet_stage2",
    )(z, c2, s2)
    return y.reshape(bsz, t, FNET_W)


def _fnet_short_kernel(u_ref, c_ref, s_ref, y_ref):
    u = u_ref[0]
    y_ref[0] = _dot(c_ref[...], u[:, 0:FNET_W]) - _dot(s_ref[...], u[:, FNET_W:])


def _fnet_short(u):
    bsz, t, w2 = u.shape
    p = np.arange(t, dtype=np.float64)
    a = 2.0 * np.pi * np.outer(p, p) / t
    norm = 1.0 / math.sqrt(t * FNET_GROUP_CH)
    c, s = jnp.asarray(np.cos(a) * norm, F32).astype(BF16), jnp.asarray(np.sin(a) * norm, F32).astype(BF16)
    return pl.pallas_call(
        _fnet_short_kernel, grid=(bsz,),
        in_specs=[pl.BlockSpec((1, t, w2), lambda b: (b, 0, 0)), _full(c.shape), _full(s.shape)],
        out_specs=pl.BlockSpec((1, t, FNET_W), lambda b: (b, 0, 0)),
        out_shape=jax.ShapeDtypeStruct((bsz, t, FNET_W), F32),
        compiler_params=_cparams("arbitrary"), name="fnet_short",
    )(u, c, s)


def _odd_post_kernel(x_ref, fr_ref, sf_ref, ys_ref, gate_ref, wo_ref, lng_ref, lnb_ref, o_ref):
    yf = (fr_ref[0] * sf_ref[0].astype(F32)).astype(BF16)
    y = _dot(yf, wo_ref[0:FNET_W, :]) + _dot(ys_ref[0], wo_ref[FNET_W:, :])
    o_ref[0] = _post_tail(x_ref[0], y, gate_ref[0], lng_ref[...], lnb_ref[...])


def _odd_post(x, fr, sf, ys, gate, wo, lng, lnb, li, l, tm):
    bsz, t, d = x.shape
    tok = lambda w: pl.BlockSpec((1, tm, w), lambda b, i: (b, i, 0))
    return pl.pallas_call(
        _odd_post_kernel, grid=(bsz, t // tm),
        in_specs=[tok(d), tok(FNET_W), tok(FNET_W), tok(SGU_W),
                  pl.BlockSpec((1, 1, d), lambda b, i: (b, 0, 0)),
                  _layer(wo, li), _layer(lng, l), _layer(lnb, l)],
        out_specs=tok(d), out_shape=jax.ShapeDtypeStruct((bsz, t, d), F32),
        compiler_params=_cparams("arbitrary", "arbitrary"), name="odd_post",
    )(x, fr, sf, ys, gate, wo, lng, lnb)


def _even_weights(w_in, gla_w2, gla_b, q_norm_g, w_uq, kv_norm_g, w_ukv):
    d = w_in.shape[0]
    idx = np.cumsum(EVEN_IN_SIZES)[:-1].tolist()
    gq, gk, gv, glr, gg, cq, ckv, kr, mg = jnp.split(w_in, idx, axis=1)
    half = MLA_ROPE // 2
    z = lambda n: jnp.zeros((d, n), w_in.dtype)
    lr_pad = jnp.concatenate([glr, z(128 - 2 * GLA_GATE_RANK)], axis=1)
    kra = jnp.concatenate([z(MLA_NOPE), kr, z(MLA_HEAD_PAD - MLA_NOPE - MLA_ROPE)], axis=1)
    krb = jnp.concatenate([z(MLA_NOPE), kr[:, half:], kr[:, :half], z(MLA_HEAD_PAD - MLA_NOPE - MLA_ROPE)], axis=1)
    w = jnp.concatenate([gq * GLA_DK ** -0.5, gk, gv, gg, cq, ckv, mg, lr_pad, kra, krb], axis=1).astype(BF16)
    zw = jnp.zeros((GLA_GATE_RANK, GLA_QK_W), F32)
    w2 = jnp.concatenate([jnp.concatenate([gla_w2[0], zw], axis=1), jnp.concatenate([zw, gla_w2[1]], axis=1),
                          jnp.zeros((128 - 2 * GLA_GATE_RANK, 2 * GLA_QK_W), F32)], axis=0)
    gb = jnp.concatenate([gla_b[0], gla_b[1]])[None, :]
    uq = w_uq.reshape(MLA_Q_RANK, MLA_HEADS, MLA_NOPE + MLA_ROPE)
    pad = MLA_HEAD_PAD - MLA_NOPE - MLA_ROPE
    zq = lambda n: jnp.zeros((MLA_Q_RANK, MLA_HEADS, n), w_uq.dtype)
    wqa = jnp.concatenate([uq, zq(pad)], axis=2).reshape(MLA_Q_RANK, -1)
    wqb = jnp.concatenate([zq(MLA_NOPE), uq[:, :, MLA_NOPE + half:], uq[:, :, MLA_NOPE:MLA_NOPE + half], zq(pad)],
                          axis=2).reshape(MLA_Q_RANK, -1)
    ukv = w_ukv.reshape(MLA_KV_RANK, MLA_HEADS, MLA_NOPE + MLA_V)
    wk = jnp.concatenate([ukv[:, :, :MLA_NOPE], jnp.zeros((MLA_KV_RANK, MLA_HEADS, MLA_HEAD_PAD - MLA_NOPE), w_ukv.dtype)],
                         axis=2).reshape(MLA_KV_RANK, -1)
    vpad = MLA_VT_ROWS - MLA_V
    wv = jnp.concatenate([ukv[:, :, MLA_NOPE:], jnp.zeros((MLA_KV_RANK, MLA_HEADS, vpad), w_ukv.dtype)],
                         axis=2).reshape(MLA_KV_RANK, -1)
    vbias = np.zeros((MLA_HEADS, MLA_VT_ROWS), np.float32)
    vbias[:, MLA_V] = 1.0
    return (w, w2.astype(BF16), gb, q_norm_g[None, :], wqa.astype(BF16), wqb.astype(BF16),
            kv_norm_g[None, :], wk.astype(BF16), wv.astype(BF16), jnp.asarray(vbias.reshape(1, -1)))


def _rope_tables(n):
    row = jnp.repeat(jnp.arange(n // GRID_W, dtype=F32), GRID_W)
    col = (jnp.arange(n) % GRID_W).astype(F32)
    n_freq = MLA_ROPE // 4
    inv = ROPE_BASE ** (-jnp.arange(n_freq, dtype=F32) / n_freq)
    ang = jnp.concatenate([row[:, None] * inv, col[:, None] * inv], -1)
    cos, sin = jnp.cos(ang), jnp.sin(ang)
    pad = jnp.zeros((n, MLA_HEAD_PAD - MLA_NOPE - MLA_ROPE), F32)
    zn = jnp.zeros((n, MLA_NOPE), F32)
    ck = jnp.concatenate([zn, cos, cos, pad], axis=1)
    sk = jnp.concatenate([zn, -sin, sin, pad], axis=1)
    return ck, sk


def _plain_tables(n):
    pad = jnp.zeros((n, MLA_HEAD_PAD - MLA_NOPE - MLA_ROPE), F32)
    zn = jnp.zeros((n, MLA_NOPE), F32)
    ck = jnp.concatenate([zn, jnp.ones((n, MLA_ROPE), F32), pad], axis=1)
    return ck, jnp.zeros((n, MLA_HEAD_PAD), F32)


def _pick(t, pref):
    return pref if t % pref == 0 else t


def kernel(x, c, ctx, c_ctx, ada_w, ada_b, post_ln_g, post_ln_b, even_w_in, gla_w2, gla_b, gla_norm_g,
           mla_q_norm_g, mla_w_uq, mla_kv_norm_g, mla_w_ukv, even_w_out, odd_w_in, sgu_w, sgu_b, odd_w_out):
    bsz, n, d = x.shape
    lc = ctx.shape[1]
    depth = ada_w.shape[0]
    assert bsz + 1 <= 8 and n % (FFT_N1 * FFT_T2_BLK) == 0 and n % 512 == 0 and lc % MLA_TK == 0

    cond = jnp.concatenate([c, c_ctx[None, :], jnp.zeros((8 - bsz - 1, d), F32)], axis=0)
    mods = _mods(cond, ada_w, ada_b)

    def lat_mod(l, j):
        return mods[l, :bsz, j * d:(j + 1) * d][:, None, :]

    def ctx_mod(l, j):
        return jnp.broadcast_to(mods[l, bsz, j * d:(j + 1) * d][None, None, :], (bsz, 1, d))

    rope_tabs = _rope_tables(n)
    ctx_tabs = _plain_tables(lc)
    dft_c = _channel_dft_matrix()
    tm_lat, tm_ctx = _pick(n, 512), _pick(lc, 256)
    tm_post = _pick(n, 1024)
    zero_state = jnp.zeros((bsz, 2, GLA_QK_W, GLA_DV), F32)

    e_wts = jax.vmap(_even_weights)(even_w_in, gla_w2, gla_b, mla_q_norm_g, mla_w_uq, mla_kv_norm_g, mla_w_ukv)
    e_wo = even_w_out.astype(BF16)
    e_gng = gla_norm_g[:, None, :]
    o_w = odd_w_in.astype(BF16)
    o_wo = odd_w_out.astype(BF16)
    o_sw = sgu_w.astype(BF16)
    o_sb = jnp.broadcast_to(sgu_b[:, :, :, None], sgu_b.shape + (SGU_GROUP_CH,))
    lng, lnb = post_ln_g[:, None, :], post_ln_b[:, None, :]

    for l in range(depth):
        need_ctx_out = any(j % 2 == 0 for j in range(l + 1, depth))
        i = l // 2
        if l % 2 == 0:
            pc = _even_pre(ctx, ctx_mod(l, 0), ctx_mod(l, 1), e_wts, i, ctx_tabs, tm_ctx, n + lc, n)
            q_c, k_c, v_c, g_c, sgg_c, smg_c, qt_c, kk, vt = pc
            pz = _even_pre(x, lat_mod(l, 0), lat_mod(l, 1), e_wts, i, rope_tabs, tm_lat, n + lc, 0, (kk, vt))
            q_l, k_l, v_l, g_l, sgg_l, smg_l, qt_l, kk, vt = pz
            of_c, ob_c, s_c = _gla(q_c, k_c, v_c, g_c, zero_state, GLA_SUB)
            of_l, ob_l, _ = _gla(q_l, k_l, v_l, g_l, s_c, 2 * GLA_SUB)
            ot_l = _mla(qt_l, kk, vt, _pick(n, 512), MLA_TK, 0, n + lc)
            x_new = _even_post(x, of_l, ob_l, sgg_l, ot_l, smg_l, lat_mod(l, 2), e_wo, e_gng, lng, lnb, i, l, tm_post)
            if need_ctx_out:
                ot_c = _mla(qt_c, kk, vt, lc, MLA_TK, n, lc)
                ctx = _even_post(ctx, of_c, ob_c, sgg_c, ot_c, smg_c, ctx_mod(l, 2), e_wo, e_gng, lng, lnb, i, l, tm_ctx)
            x = x_new
        else:
            u, sf, ys = _odd_pre(x, lat_mod(l, 0), lat_mod(l, 1), o_w, dft_c, o_sw, o_sb, i, tm_lat)
            x_new = _odd_post(x, _fnet_long(u), sf, ys, lat_mod(l, 2), o_wo, lng, lnb, i, l, tm_post)
            if need_ctx_out:
                u, sf, ys = _odd_pre(ctx, ctx_mod(l, 0), ctx_mod(l, 1), o_w, dft_c, o_sw, o_sb, i, tm_ctx)
                ctx = _odd_post(ctx, _fnet_short(u), sf, ys, ctx_mod(l, 2), o_wo, lng, lnb, i, l, tm_ctx)
            x = x_new
    return x
```

```python
import functools
import math

import numpy as np
import jax
import jax.numpy as jnp
from jax import lax
from jax.experimental import pallas as pl
from jax.experimental.pallas import tpu as pltpu

F32 = jnp.float32
BF16 = jnp.bfloat16

DEPTH = 4
GRID_W = 64
DEEPNORM_ALPHA = (2 * DEPTH) ** 0.25
LN_EPS = 1e-6

GLA_HEADS = 4
GLA_DK = 64
GLA_DV = 128
GLA_QK_W = GLA_HEADS * GLA_DK
GLA_V_W = GLA_HEADS * GLA_DV
GLA_GATE_RANK = 16
GLA_TAU = 16.0
GLA_CHUNK = 64

MLA_HEADS = 8
MLA_NOPE = 64
MLA_ROPE = 32
MLA_V = 64
MLA_Q_RANK = 256
MLA_KV_RANK = 128
MLA_V_W = MLA_HEADS * MLA_V
MLA_SCALE = (MLA_NOPE + MLA_ROPE) ** -0.5
ROPE_BASE = 10000.0
MLA_HEAD_PAD = 128
MLA_VT_ROWS = MLA_V + 16
MLA_VT_W = MLA_HEADS * MLA_VT_ROWS

FNET_GROUPS = 4
FNET_GROUP_CH = 128
FNET_W = FNET_GROUPS * FNET_GROUP_CH
FFT_N1 = 64

SGU_GROUPS = 4
SGU_GROUP_CH = 128
SGU_W = SGU_GROUPS * SGU_GROUP_CH
SGU_CHUNK = 128

EVEN_IN_SIZES = (GLA_QK_W, GLA_QK_W, GLA_V_W, 2 * GLA_GATE_RANK, GLA_V_W,
                 MLA_Q_RANK, MLA_KV_RANK, MLA_ROPE, MLA_V_W)

E_GQ, E_GK, E_GV, E_GG, E_CQ, E_CKV, E_MG, E_LR, E_KRA, E_KRB, E_END = (
    0, 256, 512, 1024, 1536, 1792, 1920, 2432, 2560, 2688, 2816)

VMEM_LIMIT_BYTES = 56 * 1024 * 1024
LOG2E = math.log2(math.e)
NEG_BIG = -1e30


def _cparams(*sem):
    return pltpu.CompilerParams(dimension_semantics=sem, vmem_limit_bytes=VMEM_LIMIT_BYTES)


def _dot(a, b):
    return jnp.dot(a, b, preferred_element_type=F32)


def _dot_nt(a, b):
    return lax.dot_general(a, b, (((1,), (1,)), ((), ())), preferred_element_type=F32)


def _dot_tn(a, b):
    return lax.dot_general(a, b, (((0,), (0,)), ((), ())), preferred_element_type=F32)


def _ln(x):
    xc = x - jnp.mean(x, -1, keepdims=True)
    return xc * lax.rsqrt(jnp.mean(xc * xc, -1, keepdims=True) + LN_EPS)


def _rms(x):
    return x * lax.rsqrt(jnp.mean(x * x, -1, keepdims=True) + LN_EPS)


def _silu(x):
    return x / (1.0 + jnp.exp(-x))


def _gelu(x):
    return 0.5 * x * (1.0 + lax.erf(x * (2.0 ** -0.5)))


def _tile_lanes(x, reps):
    return jnp.concatenate([x] * reps, axis=1)


def _full(shape):
    n = len(shape)
    return pl.BlockSpec(shape, lambda *_: (0,) * n)


def _layer(arr, i):
    n = arr.ndim
    return pl.BlockSpec((None,) + arr.shape[1:], lambda *_: (i,) + (0,) * (n - 1))


def _mods_kernel(cond_ref, w_ref, b_ref, o_ref):
    s = _silu(cond_ref[...])
    o_ref[0] = _dot(s.astype(BF16), w_ref[0].astype(BF16)) + b_ref[0]


def _mods(cond, ada_w, ada_b):
    depth, d, d3 = ada_w.shape
    return pl.pallas_call(
        _mods_kernel,
        grid=(depth,),
        in_specs=[pl.BlockSpec((8, d), lambda l: (0, 0)),
                  pl.BlockSpec((1, d, d3), lambda l: (l, 0, 0)),
                  pl.BlockSpec((1, 1, d3), lambda l: (l, 0, 0))],
        out_specs=pl.BlockSpec((1, 8, d3), lambda l: (l, 0, 0)),
        out_shape=jax.ShapeDtypeStruct((depth, 8, d3), F32),
        compiler_params=_cparams("arbitrary"),
        name="ada_mod",
    )(cond, ada_w, ada_b.reshape(depth, 1, d3))


def _even_pre_kernel(x_ref, shift_ref, scale_ref, w_ref, w2_ref, gb_ref, qg_ref, wqa_ref, wqb_ref,
                     kvg_ref, wk_ref, wv_ref, vb_ref, ck_ref, sk_ref,
                     q_ref, k_ref, v_ref, g_ref, sgg_ref, smg_ref, qt_ref, kk_ref, vt_ref):
    h = _ln(x_ref[0]) * (1.0 + scale_ref[0]) + shift_ref[0]
    z = _dot(h.astype(BF16), w_ref[...])
    q_ref[0] = z[:, E_GQ:E_GK].astype(BF16)
    k_ref[0] = z[:, E_GK:E_GV].astype(BF16)
    v_ref[0] = z[:, E_GV:E_GG].astype(BF16)
    sgg_ref[0] = _silu(z[:, E_GG:E_CQ]).astype(BF16)
    smg_ref[0] = _silu(z[:, E_MG:E_LR]).astype(BF16)
    pre = _dot(z[:, E_LR:E_KRA].astype(BF16), w2_ref[...]) + gb_ref[...]
    g_ref[0] = jax.nn.log_sigmoid(pre) * (1.0 / GLA_TAU)
    cqn = (_rms(z[:, E_CQ:E_CKV]) * qg_ref[...]).astype(BF16)
    lane = lax.broadcasted_iota(jnp.int32, (1, MLA_HEAD_PAD), 1)
    q_scale = MLA_SCALE * LOG2E
    cq = ck_ref[...] * q_scale + jnp.where(lane < MLA_NOPE, q_scale, 0.0)
    sq = sk_ref[...] * q_scale
    qfull = (_dot(cqn, wqa_ref[...]) * _tile_lanes(cq, MLA_HEADS)
             + _dot(cqn, wqb_ref[...]) * _tile_lanes(sq, MLA_HEADS))
    qt_ref[0] = qfull.T.astype(BF16)
    ckvn = (_rms(z[:, E_CKV:E_MG]) * kvg_ref[...]).astype(BF16)
    kr = z[:, E_KRA:E_KRB] * ck_ref[...] + z[:, E_KRB:E_END] * sk_ref[...]
    kk_ref[0] = (_dot(ckvn, wk_ref[...]) + _tile_lanes(kr, MLA_HEADS)).astype(BF16)
    vt_ref[0] = (_dot(ckvn, wv_ref[...]) + vb_ref[...]).T.astype(BF16)


def _even_pre(x, shift, scale, wts, li, tabs, tm):
    bsz, t, d = x.shape
    hq = MLA_HEADS * MLA_HEAD_PAD
    tok = lambda w: pl.BlockSpec((1, tm, w), lambda b, i: (b, i, 0))
    tab = pl.BlockSpec((tm, MLA_HEAD_PAD), lambda b, i: (i, 0))
    out_shape = (
        jax.ShapeDtypeStruct((bsz, t, GLA_QK_W), BF16),
        jax.ShapeDtypeStruct((bsz, t, GLA_QK_W), BF16),
        jax.ShapeDtypeStruct((bsz, t, GLA_V_W), BF16),
        jax.ShapeDtypeStruct((bsz, t, 2 * GLA_QK_W), F32),
        jax.ShapeDtypeStruct((bsz, t, GLA_V_W), BF16),
        jax.ShapeDtypeStruct((bsz, t, MLA_V_W), BF16),
        jax.ShapeDtypeStruct((bsz, hq, t), BF16),
        jax.ShapeDtypeStruct((bsz, t, hq), BF16),
        jax.ShapeDtypeStruct((bsz, MLA_VT_W, t), BF16),
    )
    out_specs = (tok(GLA_QK_W), tok(GLA_QK_W), tok(GLA_V_W), tok(2 * GLA_QK_W), tok(GLA_V_W), tok(MLA_V_W),
                 pl.BlockSpec((1, hq, tm), lambda b, i: (b, 0, i)),
                 tok(hq),
                 pl.BlockSpec((1, MLA_VT_W, tm), lambda b, i: (b, 0, i)))
    in_specs = [tok(d), shift[1], scale[1]] + [_layer(w, li) for w in wts] + [tab] * 2
    return pl.pallas_call(
        _even_pre_kernel, grid=(bsz, t // tm), in_specs=in_specs, out_specs=out_specs, out_shape=out_shape,
        compiler_params=_cparams("arbitrary", "arbitrary"), name="even_pre",
    )(x, shift[0], scale[0], *wts, *tabs)


def _gla_block(q, k, v, g, tmat, same_tri, mid_off, end_off, head_masks, chunk_masks):
    r = q.shape[0]
    c = GLA_CHUNK
    nc = r // c
    g_hi = g.astype(BF16)
    g_lo = (g - g_hi.astype(F32)).astype(BF16)
    b = _dot(tmat, g_hi) + _dot(tmat, g_lo)
    mid_rows = [b[j * c + mid_off:j * c + mid_off + 1] for j in range(nc)]
    end_rows = [b[j * c + end_off:j * c + end_off + 1] for j in range(nc)]
    per_chunk = lambda rows: jnp.concatenate([jnp.broadcast_to(x, (c, GLA_QK_W)) for x in rows], axis=0)
    b_mid = per_chunk(mid_rows)
    e_up = jnp.exp(b - b_mid)
    e_dn = jnp.exp(b_mid - b)
    qe = q * e_up
    ke = (k * e_dn).astype(BF16)
    kd_t = (k * (e_dn * per_chunk([jnp.exp(e - m) for e, m in zip(end_rows, mid_rows)]))).T
    qb = q * (e_up * per_chunk([jnp.exp(m) for m in mid_rows]))
    stack = lambda a: jnp.concatenate([jnp.where(m, a, 0.0) for m in head_masks], axis=0).astype(BF16)
    att = _dot_nt(stack(qe), ke)
    qb_st = stack(qb)
    dec_t = jnp.exp(jnp.concatenate(end_rows + [jnp.zeros((8 - nc, GLA_QK_W), F32)], axis=0)).T
    intra, ds = [], []
    for h in range(GLA_HEADS):
        a_h = jnp.where(same_tri, att[h * r:(h + 1) * r], 0.0).astype(BF16)
        v_h = v[:, h * GLA_DV:(h + 1) * GLA_DV]
        intra.append(_dot(a_h, v_h))
        kd_h = kd_t[h * GLA_DK:(h + 1) * GLA_DK]
        lhs = jnp.concatenate([jnp.where(cm, kd_h, 0.0) for cm in chunk_masks], axis=0).astype(BF16)
        ds.append(_dot(lhs, v_h))
    return intra, ds, qb_st, dec_t


def _gla_recur(par, s, o_ref, row0, reverse):
    intra, ds, qb_st, dec_t = par
    c = GLA_CHUNK
    r = intra[0].shape[0]
    nc = r // c
    for j in (reversed(range(nc)) if reverse else range(nc)):
        rows = slice(j * c, (j + 1) * c)
        st = jnp.concatenate([qb_st[h * r + j * c:h * r + (j + 1) * c] for h in range(GLA_HEADS)], axis=0)
        o_inter = _dot(st, s.astype(BF16))
        for h in range(GLA_HEADS):
            o_ref[0, row0 + j * c:row0 + (j + 1) * c, h * GLA_DV:(h + 1) * GLA_DV] = (
                intra[h][rows] + o_inter[h * c:(h + 1) * c]).astype(o_ref.dtype)
        dec = jnp.broadcast_to(dec_t[:, j:j + 1], (GLA_QK_W, GLA_DV))
        s = dec * s + jnp.concatenate([ds[h][rows] for h in range(GLA_HEADS)], axis=0)
    return s


GLA_SUB = 256


def _gla_kernel(qf_ref, kf_ref, vf_ref, gf_ref, qb_ref, kb_ref, vb_ref, gb_ref, s0_ref, tl_ref, tu_ref,
                of_ref, ob_ref, sfin_ref, s_scr, *, tb):
    i = pl.program_id(1)

    @pl.when(i == 0)
    def _():
        s_scr[...] = s0_ref[0]

    c, r = GLA_CHUNK, GLA_SUB
    row = lax.broadcasted_iota(jnp.int32, (r, r), 0)
    col = lax.broadcasted_iota(jnp.int32, (r, r), 1)
    same = (row // c) == (col // c)
    lane_head = lax.broadcasted_iota(jnp.int32, (r, GLA_QK_W), 1) // GLA_DK
    head_masks = [lane_head == h for h in range(GLA_HEADS)]
    lane_chunk = lax.broadcasted_iota(jnp.int32, (GLA_DK, r), 1) // c
    chunk_masks = [lane_chunk == j for j in range(r // c)]

    par_f, par_b = [], []
    for u in range(tb // r):
        rs = slice(u * r, (u + 1) * r)
        par_f.append(_gla_block(qf_ref[0, rs].astype(F32), kf_ref[0, rs].astype(F32), vf_ref[0, rs], gf_ref[0, rs],
                                tl_ref[...], same & (col <= row), c // 2 - 1, c - 1, head_masks, chunk_masks))
        par_b.append(_gla_block(qb_ref[0, rs].astype(F32), kb_ref[0, rs].astype(F32), vb_ref[0, rs], gb_ref[0, rs],
                                tu_ref[...], same & (col >= row), c // 2, 0, head_masks, chunk_masks))
    s = s_scr[0]
    for u in range(tb // r):
        s = _gla_recur(par_f[u], s, of_ref, u * r, False)
    s_scr[0] = s
    s = s_scr[1]
    for u in reversed(range(tb // r)):
        s = _gla_recur(par_b[u], s, ob_ref, u * r, True)
    s_scr[1] = s

    @pl.when(i == pl.num_programs(1) - 1)
    def _():
        sfin_ref[0] = s_scr[...]


def _gla(q, k, v, g, s0, tb):
    bsz, t, _ = q.shape
    nblk = t // tb
    assert tb % GLA_SUB == 0
    idx = np.arange(GLA_SUB)
    same = (idx[:, None] // GLA_CHUNK) == (idx[None, :] // GLA_CHUNK)
    tl = jnp.asarray(same & (idx[None, :] <= idx[:, None]), BF16)
    tu = jnp.asarray(same & (idx[None, :] >= idx[:, None]), BF16)
    fwd = lambda w: pl.BlockSpec((1, tb, w), lambda b, i: (b, i, 0))
    bwd = lambda w: pl.BlockSpec((1, tb, w), lambda b, i: (b, nblk - 1 - i, 0))
    st = pl.BlockSpec((1, 2, GLA_QK_W, GLA_DV), lambda b, i: (b, 0, 0, 0))
    return pl.pallas_call(
        functools.partial(_gla_kernel, tb=tb),
        grid=(bsz, nblk),
        in_specs=[fwd(GLA_QK_W), fwd(GLA_QK_W), fwd(GLA_V_W),
                  pl.BlockSpec((1, tb, GLA_QK_W), lambda b, i: (b, i, 0)),
                  bwd(GLA_QK_W), bwd(GLA_QK_W), bwd(GLA_V_W),
                  pl.BlockSpec((1, tb, GLA_QK_W), lambda b, i: (b, nblk - 1 - i, 1)),
                  st, _full((GLA_SUB, GLA_SUB)), _full((GLA_SUB, GLA_SUB))],
        out_specs=(fwd(GLA_V_W), bwd(GLA_V_W), st),
        out_shape=(jax.ShapeDtypeStruct((bsz, t, GLA_V_W), BF16),
                   jax.ShapeDtypeStruct((bsz, t, GLA_V_W), BF16),
                   jax.ShapeDtypeStruct((bsz, 2, GLA_QK_W, GLA_DV), F32)),
        scratch_shapes=[pltpu.VMEM((2, GLA_QK_W, GLA_DV), F32)],
        compiler_params=_cparams("arbitrary", "arbitrary"), name="gla_scan",
    )(q, k, v, g, q, k, v, g, s0, tl, tu)


MLA_SLOTS = 3
MLA_TK = 256


def _mla_kernel(qt_ref, *refs, chunks, tk):
    n_src = max(src for src, _ in chunks) + 1
    kv_refs, o_ref, scratch = refs[:2 * n_src], refs[2 * n_src], refs[2 * n_src + 1:]
    n_chunks = len(chunks)
    q_t = qt_ref[0]
    tq = q_t.shape[1]
    s_bufs, p_bufs = scratch[:MLA_SLOTS], scratch[MLA_SLOTS:]
    cmax = None
    m = jnp.full((1, tq), NEG_BIG, F32)
    alpha = None
    acc = jnp.zeros((MLA_VT_ROWS, tq), F32)
    for t in range(n_chunks + 2):
        alpha_prev = alpha
        if 1 <= t <= n_chunks:
            e = (t - 1) % MLA_SLOTS
            m_new = jnp.maximum(m, jnp.max(cmax, axis=0, keepdims=True))
            alpha = jnp.exp2(m - m_new)
            m = m_new
            p_bufs[e][...] = jnp.exp2(s_bufs[e][...] - m).astype(BF16)
        if t < n_chunks:
            src, lo = chunks[t]
            s_t = _dot(kv_refs[2 * src][0, lo:lo + tk, :], q_t)
            s_bufs[t % MLA_SLOTS][...] = s_t
            cmax = jnp.max(s_t.reshape(tk // 8, 8, tq), axis=0)
        if t >= 2:
            src, lo = chunks[t - 2]
            vt = kv_refs[2 * src + 1][0, :, lo:lo + tk]
            acc = alpha_prev * acc + _dot(vt, p_bufs[(t - 2) % MLA_SLOTS][...])
    o_ref[0] = (acc[0:MLA_V] / acc[MLA_V:MLA_V + 1]).astype(o_ref.dtype)


def _mla(qt, kv, tq, tk):
    bsz, _, t = qt.shape
    assert t % tq == 0 and all(k.shape[1] % tk == 0 for k, _ in kv)
    chunks = [(src, lo) for src, (k, _) in enumerate(kv) for lo in range(0, k.shape[1], tk)]
    in_specs = [pl.BlockSpec((1, MLA_HEAD_PAD, tq), lambda b, h, i: (b, h, i))]
    for k, _ in kv:
        in_specs += [pl.BlockSpec((1, k.shape[1], MLA_HEAD_PAD), lambda b, h, i: (b, 0, h)),
                     pl.BlockSpec((1, MLA_VT_ROWS, k.shape[1]), lambda b, h, i: (b, h, 0))]
    return pl.pallas_call(
        functools.partial(_mla_kernel, chunks=chunks, tk=tk),
        grid=(bsz, MLA_HEADS, t // tq),
        in_specs=in_specs,
        out_specs=pl.BlockSpec((1, MLA_V, tq), lambda b, h, i: (b, h, i)),
        out_shape=jax.ShapeDtypeStruct((bsz, MLA_V_W, t), BF16),
        scratch_shapes=[pltpu.VMEM((tk, tq), F32)] * MLA_SLOTS + [pltpu.VMEM((tk, tq), BF16)] * MLA_SLOTS,
        compiler_params=_cparams("arbitrary", "arbitrary", "arbitrary"), name="mla_attn",
    )(qt, *[a for pair in kv for a in pair])


def _post_tail(x, y, gate, lng, lnb):
    return _ln(DEEPNORM_ALPHA * x + gate * y) * lng + lnb


def _even_post_kernel(x_ref, of_ref, ob_ref, sgg_ref, ot_ref, smg_ref, gate_ref, wo_ref, gng_ref,
                      lng_ref, lnb_ref, o_ref):
    o = of_ref[0].astype(F32) + ob_ref[0].astype(F32)
    parts = [_rms(o[:, h * GLA_DV:(h + 1) * GLA_DV]) * gng_ref[...] for h in range(GLA_HEADS)]
    yg = jnp.concatenate(parts, axis=1) * sgg_ref[0].astype(F32)
    ym = ot_ref[0].astype(F32).T * smg_ref[0].astype(F32)
    y = _dot(yg.astype(BF16), wo_ref[0:GLA_V_W, :]) + _dot(ym.astype(BF16), wo_ref[GLA_V_W:, :])
    o_ref[0] = _post_tail(x_ref[0], y, gate_ref[0], lng_ref[...], lnb_ref[...])


def _even_post(x, o_f, o_b, sgg, o_t, smg, gate, wo, gng, lng, lnb, li, l, tm):
    bsz, t, d = x.shape
    tok = lambda w: pl.BlockSpec((1, tm, w), lambda b, i: (b, i, 0))
    return pl.pallas_call(
        _even_post_kernel, grid=(bsz, t // tm),
        in_specs=[tok(d), tok(GLA_V_W), tok(GLA_V_W), tok(GLA_V_W),
                  pl.BlockSpec((1, MLA_V_W, tm), lambda b, i: (b, 0, i)), tok(MLA_V_W),
                  gate[1],
                  _layer(wo, li), _layer(gng, li), _layer(lng, l), _layer(lnb, l)],
        out_specs=tok(d), out_shape=jax.ShapeDtypeStruct((bsz, t, d), F32),
        compiler_params=_cparams("arbitrary", "arbitrary"), name="even_post",
    )(x, o_f, o_b, sgg, o_t, smg, gate[0], wo, gng, lng, lnb)


def _odd_pre_kernel(x_ref, shift_ref, scale_ref, w_ref, dft_ref, sw_ref, sb_ref, u_ref, sf_ref, ys_ref):
    h = _ln(x_ref[0]) * (1.0 + scale_ref[0]) + shift_ref[0]
    z = _dot(h.astype(BF16), w_ref[...])
    for g in range(FNET_GROUPS):
        cs = slice(g * FNET_GROUP_CH, (g + 1) * FNET_GROUP_CH)
        ab = _dot(z[:, cs].astype(BF16), dft_ref[...])
        u_ref[0, :, cs] = ab[:, 0:FNET_GROUP_CH].astype(BF16)
        u_ref[0, :, FNET_W + g * FNET_GROUP_CH:FNET_W + (g + 1) * FNET_GROUP_CH] = ab[:, FNET_GROUP_CH:].astype(BF16)
    sf_ref[0] = _silu(z[:, FNET_W:2 * FNET_W]).astype(BF16)
    o0 = 2 * FNET_W
    tm = z.shape[0]
    for g in range(SGU_GROUPS):
        cs = slice(g * SGU_GROUP_CH, (g + 1) * SGU_GROUP_CH)
        ug = _gelu(z[:, o0 + g * SGU_GROUP_CH:o0 + (g + 1) * SGU_GROUP_CH])
        vg = _ln(_gelu(z[:, o0 + SGU_W + g * SGU_GROUP_CH:o0 + SGU_W + (g + 1) * SGU_GROUP_CH])).astype(BF16)
        sg = _silu(z[:, o0 + 2 * SGU_W + g * SGU_GROUP_CH:o0 + 2 * SGU_W + (g + 1) * SGU_GROUP_CH])
        for c in range(tm // SGU_CHUNK):
            rs = slice(c * SGU_CHUNK, (c + 1) * SGU_CHUNK)
            sv = _dot(sw_ref[g], vg[rs]) + sb_ref[g]
            ys_ref[0, rs, cs] = (ug[rs] * sv * sg[rs]).astype(BF16)


def _odd_pre(x, shift, scale, w, dft, sw, sb, li, tm):
    bsz, t, d = x.shape
    tok = lambda w_: pl.BlockSpec((1, tm, w_), lambda b, i: (b, i, 0))
    return pl.pallas_call(
        _odd_pre_kernel, grid=(bsz, t // tm),
        in_specs=[tok(d), shift[1], scale[1], _layer(w, li), _full(dft.shape), _layer(sw, li), _layer(sb, li)],
        out_specs=(tok(2 * FNET_W), tok(FNET_W), tok(SGU_W)),
        out_shape=(jax.ShapeDtypeStruct((bsz, t, 2 * FNET_W), BF16),
                   jax.ShapeDtypeStruct((bsz, t, FNET_W), BF16),
                   jax.ShapeDtypeStruct((bsz, t, SGU_W), BF16)),
        compiler_params=_cparams("arbitrary", "arbitrary"), name="odd_pre",
    )(x, shift[0], scale[0], w, dft, sw, sb)


FFT_T2_BLK = 16
FFT_P1_BLK = 8


def _fft1_kernel(u_ref, w1_ref, tc_ref, ts_ref, z_ref):
    n1, tb = FFT_N1, FFT_T2_BLK
    r = n1 * tb
    pq = _dot(w1_ref[...], u_ref[0].reshape(r, 2 * FNET_W))
    zr = pq[0:r, 0:FNET_W] - pq[r:, FNET_W:]
    zi = -pq[0:r, FNET_W:] - pq[r:, 0:FNET_W]
    tc = _tile_lanes(tc_ref[0], FNET_W // 128)
    ts = _tile_lanes(ts_ref[0], FNET_W // 128)
    z_ref[0, :, :, 0:FNET_W] = (zr * tc + zi * ts).astype(BF16).reshape(n1, tb, FNET_W)
    z_ref[0, :, :, FNET_W:] = (zi * tc - zr * ts).astype(BF16).reshape(n1, tb, FNET_W)


def _fft2_kernel(z_ref, c2_ref, s2_ref, y_ref):
    for j in range(FFT_P1_BLK):
        zp = z_ref[0, j]
        y_ref[0, :, j, :] = _dot(c2_ref[...], zp[:, 0:FNET_W]) + _dot(s2_ref[...], zp[:, FNET_W:])


def _dft_tables(t):
    n1, n2 = FFT_N1, t // FFT_N1
    p1 = np.arange(n1, dtype=np.float64)
    a1 = 2.0 * np.pi * np.outer(p1, p1) / n1
    eye = np.eye(FFT_T2_BLK)
    w1 = np.concatenate([np.kron(np.cos(a1), eye), np.kron(np.sin(a1), eye)], axis=0)
    at = 2.0 * np.pi * np.outer(p1, np.arange(n2, dtype=np.float64)) / t

    def twiddle(a):
        a = a.reshape(n1, n2 // FFT_T2_BLK, FFT_T2_BLK).transpose(1, 0, 2).reshape(n2 // FFT_T2_BLK, -1)
        return np.repeat(a[:, :, None], 128, axis=2)
    tc, ts = twiddle(np.cos(at)), twiddle(np.sin(at))
    p2 = np.arange(n2, dtype=np.float64)
    a2 = 2.0 * np.pi * np.outer(p2, p2) / n2
    norm = 1.0 / math.sqrt(t * FNET_GROUP_CH)
    return (jnp.asarray(w1, F32).astype(BF16), jnp.asarray(tc, F32), jnp.asarray(ts, F32),
            jnp.asarray(np.cos(a2) * norm, F32).astype(BF16), jnp.asarray(np.sin(a2) * norm, F32).astype(BF16))


def _channel_dft_matrix():
    d = np.arange(FNET_GROUP_CH, dtype=np.float64)
    a = 2.0 * np.pi * np.outer(d, d) / FNET_GROUP_CH
    return jnp.asarray(np.concatenate([np.cos(a), np.sin(a)], axis=1), F32).astype(BF16)


def _fnet_long(u):
    bsz, t, w2 = u.shape
    n1, n2 = FFT_N1, t // FFT_N1
    w1, tc, ts, c2, s2 = _dft_tables(t)
    blk = pl.BlockSpec((1, n1, FFT_T2_BLK, w2), lambda b, i: (b, 0, i, 0))
    tw = pl.BlockSpec((1, n1 * FFT_T2_BLK, 128), lambda b, i: (i, 0, 0))
    z = pl.pallas_call(
        _fft1_kernel, grid=(bsz, n2 // FFT_T2_BLK),
        in_specs=[blk, _full(w1.shape), tw, tw],
        out_specs=blk,
        out_shape=jax.ShapeDtypeStruct((bsz, n1, n2, w2), BF16),
        compiler_params=_cparams("arbitrary", "arbitrary"), name="fnet_stage1",
    )(u.reshape(bsz, n1, n2, w2), w1, tc, ts)
    y = pl.pallas_call(
        _fft2_kernel, grid=(bsz, n1 // FFT_P1_BLK),
        in_specs=[pl.BlockSpec((1, FFT_P1_BLK, n2, w2), lambda b, i: (b, i, 0, 0)),
                  _full(c2.shape), _full(s2.shape)],
        out_specs=pl.BlockSpec((1, n2, FFT_P1_BLK, FNET_W), lambda b, i: (b, 0, i, 0)),
        out_shape=jax.ShapeDtypeStruct((bsz, n2, n1, FNET_W), F32),
        compiler_params=_cparams("arbitrary", "arbitrary"), name="fnet_stage2",
    )(z, c2, s2)
    return y.reshape(bsz, t, FNET_W)


def _fnet_short_kernel(u_ref, c_ref, s_ref, y_ref):
    u = u_ref[0]
    y_ref[0] = _dot(c_ref[...], u[:, 0:FNET_W]) - _dot(s_ref[...], u[:, FNET_W:])


def _fnet_short(u):
    bsz, t, w2 = u.shape
    p = np.arange(t, dtype=np.float64)
    a = 2.0 * np.pi * np.outer(p, p) / t
    norm = 1.0 / math.sqrt(t * FNET_GROUP_CH)
    c, s = jnp.asarray(np.cos(a) * norm, F32).astype(BF16), jnp.asarray(np.sin(a) * norm, F32).astype(BF16)
    return pl.pallas_call(
        _fnet_short_kernel, grid=(bsz,),
        in_specs=[pl.BlockSpec((1, t, w2), lambda b: (b, 0, 0)), _full(c.shape), _full(s.shape)],
        out_specs=pl.BlockSpec((1, t, FNET_W), lambda b: (b, 0, 0)),
        out_shape=jax.ShapeDtypeStruct((bsz, t, FNET_W), F32),
        compiler_params=_cparams("arbitrary"), name="fnet_short",
    )(u, c, s)


def _odd_post_kernel(x_ref, fr_ref, sf_ref, ys_ref, gate_ref, wo_ref, lng_ref, lnb_ref, o_ref):
    yf = (fr_ref[0] * sf_ref[0].astype(F32)).astype(BF16)
    y = _dot(yf, wo_ref[0:FNET_W, :]) + _dot(ys_ref[0], wo_ref[FNET_W:, :])
    o_ref[0] = _post_tail(x_ref[0], y, gate_ref[0], lng_ref[...], lnb_ref[...])


def _odd_post(x, fr, sf, ys, gate, wo, lng, lnb, li, l, tm):
    bsz, t, d = x.shape
    tok = lambda w: pl.BlockSpec((1, tm, w), lambda b, i: (b, i, 0))
    return pl.pallas_call(
        _odd_post_kernel, grid=(bsz, t // tm),
        in_specs=[tok(d), tok(FNET_W), tok(FNET_W), tok(SGU_W),
                  gate[1],
                  _layer(wo, li), _layer(lng, l), _layer(lnb, l)],
        out_specs=tok(d), out_shape=jax.ShapeDtypeStruct((bsz, t, d), F32),
        compiler_params=_cparams("arbitrary", "arbitrary"), name="odd_post",
    )(x, fr, sf, ys, gate[0], wo, lng, lnb)


def _even_weights(w_in, gla_w2, gla_b, q_norm_g, w_uq, kv_norm_g, w_ukv):
    d = w_in.shape[0]
    idx = np.cumsum(EVEN_IN_SIZES)[:-1].tolist()
    gq, gk, gv, glr, gg, cq, ckv, kr, mg = jnp.split(w_in, idx, axis=1)
    half = MLA_ROPE // 2
    z = lambda n: jnp.zeros((d, n), w_in.dtype)
    lr_pad = jnp.concatenate([glr, z(128 - 2 * GLA_GATE_RANK)], axis=1)
    kra = jnp.concatenate([z(MLA_NOPE), kr, z(MLA_HEAD_PAD - MLA_NOPE - MLA_ROPE)], axis=1)
    krb = jnp.concatenate([z(MLA_NOPE), kr[:, half:], kr[:, :half], z(MLA_HEAD_PAD - MLA_NOPE - MLA_ROPE)], axis=1)
    w = jnp.concatenate([gq * GLA_DK ** -0.5, gk, gv, gg, cq, ckv, mg, lr_pad, kra, krb], axis=1).astype(BF16)
    zw = jnp.zeros((GLA_GATE_RANK, GLA_QK_W), F32)
    w2 = jnp.concatenate([jnp.concatenate([gla_w2[0], zw], axis=1), jnp.concatenate([zw, gla_w2[1]], axis=1),
                          jnp.zeros((128 - 2 * GLA_GATE_RANK, 2 * GLA_QK_W), F32)], axis=0)
    gb = jnp.concatenate([gla_b[0], gla_b[1]])[None, :]
    uq = w_uq.reshape(MLA_Q_RANK, MLA_HEADS, MLA_NOPE + MLA_ROPE)
    pad = MLA_HEAD_PAD - MLA_NOPE - MLA_ROPE
    zq = lambda n: jnp.zeros((MLA_Q_RANK, MLA_HEADS, n), w_uq.dtype)
    wqa = jnp.concatenate([uq, zq(pad)], axis=2).reshape(MLA_Q_RANK, -1)
    wqb = jnp.concatenate([zq(MLA_NOPE), uq[:, :, MLA_NOPE + half:], uq[:, :, MLA_NOPE:MLA_NOPE + half], zq(pad)],
                          axis=2).reshape(MLA_Q_RANK, -1)
    ukv = w_ukv.reshape(MLA_KV_RANK, MLA_HEADS, MLA_NOPE + MLA_V)
    wk = jnp.concatenate([ukv[:, :, :MLA_NOPE], jnp.zeros((MLA_KV_RANK, MLA_HEADS, MLA_HEAD_PAD - MLA_NOPE), w_ukv.dtype)],
                         axis=2).reshape(MLA_KV_RANK, -1)
    vpad = MLA_VT_ROWS - MLA_V
    wv = jnp.concatenate([ukv[:, :, MLA_NOPE:], jnp.zeros((MLA_KV_RANK, MLA_HEADS, vpad), w_ukv.dtype)],
                         axis=2).reshape(MLA_KV_RANK, -1)
    vbias = np.zeros((MLA_HEADS, MLA_VT_ROWS), np.float32)
    vbias[:, MLA_V] = 1.0
    return (w, w2.astype(BF16), gb, q_norm_g[None, :], wqa.astype(BF16), wqb.astype(BF16),
            kv_norm_g[None, :], wk.astype(BF16), wv.astype(BF16), jnp.asarray(vbias.reshape(1, -1)))


def _rope_tables(n):
    row = jnp.repeat(jnp.arange(n // GRID_W, dtype=F32), GRID_W)
    col = (jnp.arange(n) % GRID_W).astype(F32)
    n_freq = MLA_ROPE // 4
    inv = ROPE_BASE ** (-jnp.arange(n_freq, dtype=F32) / n_freq)
    ang = jnp.concatenate([row[:, None] * inv, col[:, None] * inv], -1)
    cos, sin = jnp.cos(ang), jnp.sin(ang)
    pad = jnp.zeros((n, MLA_HEAD_PAD - MLA_NOPE - MLA_ROPE), F32)
    zn = jnp.zeros((n, MLA_NOPE), F32)
    ck = jnp.concatenate([zn, cos, cos, pad], axis=1)
    sk = jnp.concatenate([zn, -sin, sin, pad], axis=1)
    return ck, sk


def _plain_tables(n):
    pad = jnp.zeros((n, MLA_HEAD_PAD - MLA_NOPE - MLA_ROPE), F32)
    zn = jnp.zeros((n, MLA_NOPE), F32)
    ck = jnp.concatenate([zn, jnp.ones((n, MLA_ROPE), F32), pad], axis=1)
    return ck, jnp.zeros((n, MLA_HEAD_PAD), F32)


def _pick(t, pref):
    return pref if t % pref == 0 else t


def kernel(x, c, ctx, c_ctx, ada_w, ada_b, post_ln_g, post_ln_b, even_w_in, gla_w2, gla_b, gla_norm_g,
           mla_q_norm_g, mla_w_uq, mla_kv_norm_g, mla_w_ukv, even_w_out, odd_w_in, sgu_w, sgu_b, odd_w_out):
    bsz, n, d = x.shape
    lc = ctx.shape[1]
    depth = ada_w.shape[0]
    assert bsz + 1 <= 8 and n % (FFT_N1 * FFT_T2_BLK) == 0 and n % 512 == 0 and lc % MLA_TK == 0

    cond = jnp.concatenate([c, c_ctx[None, :], jnp.zeros((8 - bsz - 1, d), F32)], axis=0)
    mods = _mods(cond, ada_w, ada_b)

    mods4 = mods.reshape(depth, 8, 1, 3 * d)

    def lat_mod(l, j):
        return mods4, pl.BlockSpec((None, 1, 1, d), lambda b, *_: (l, b, 0, j))

    def ctx_mod(l, j):
        return mods4, pl.BlockSpec((None, 1, 1, d), lambda b, *_: (l, bsz, 0, j))

    rope_tabs = _rope_tables(n)
    ctx_tabs = _plain_tables(lc)
    dft_c = _channel_dft_matrix()
    tm_lat, tm_ctx = _pick(n, 512), _pick(lc, 256)
    tm_post = _pick(n, 1024)
    zero_state = jnp.zeros((bsz, 2, GLA_QK_W, GLA_DV), F32)

    e_wts = jax.vmap(_even_weights)(even_w_in, gla_w2, gla_b, mla_q_norm_g, mla_w_uq, mla_kv_norm_g, mla_w_ukv)
    e_wo = even_w_out.astype(BF16)
    e_gng = gla_norm_g[:, None, :]
    o_w = odd_w_in.astype(BF16)
    o_wo = odd_w_out.astype(BF16)
    o_sw = sgu_w.astype(BF16)
    o_sb = jnp.broadcast_to(sgu_b[:, :, :, None], sgu_b.shape + (SGU_GROUP_CH,))
    lng, lnb = post_ln_g[:, None, :], post_ln_b[:, None, :]

    for l in range(depth):
        need_ctx_out = any(j % 2 == 0 for j in range(l + 1, depth))
        i = l // 2
        if l % 2 == 0:
            pc = _even_pre(ctx, ctx_mod(l, 0), ctx_mod(l, 1), e_wts, i, ctx_tabs, tm_ctx)
            q_c, k_c, v_c, g_c, sgg_c, smg_c, qt_c, kk_c, vt_c = pc
            pz = _even_pre(x, lat_mod(l, 0), lat_mod(l, 1), e_wts, i, rope_tabs, tm_lat)
            q_l, k_l, v_l, g_l, sgg_l, smg_l, qt_l, kk_l, vt_l = pz
            of_c, ob_c, s_c = _gla(q_c, k_c, v_c, g_c, zero_state, GLA_SUB)
            of_l, ob_l, _ = _gla(q_l, k_l, v_l, g_l, s_c, 2 * GLA_SUB)
            ot_l = _mla(qt_l, [(kk_l, vt_l), (kk_c, vt_c)], _pick(n, 512), MLA_TK)
            x_new = _even_post(x, of_l, ob_l, sgg_l, ot_l, smg_l, lat_mod(l, 2), e_wo, e_gng, lng, lnb, i, l, tm_post)
            if need_ctx_out:
                ot_c = _mla(qt_c, [(kk_c, vt_c)], lc, MLA_TK)
                ctx = _even_post(ctx, of_c, ob_c, sgg_c, ot_c, smg_c, ctx_mod(l, 2), e_wo, e_gng, lng, lnb, i, l, tm_ctx)
            x = x_new
        else:
            u, sf, ys = _odd_pre(x, lat_mod(l, 0), lat_mod(l, 1), o_w, dft_c, o_sw, o_sb, i, tm_lat)
            x_new = _odd_post(x, _fnet_long(u), sf, ys, lat_mod(l, 2), o_wo, lng, lnb, i, l, tm_post)
            if need_ctx_out:
                u, sf, ys = _odd_pre(ctx, ctx_mod(l, 0), ctx_mod(l, 1), o_w, dft_c, o_sw, o_sb, i, tm_ctx)
                ctx = _odd_post(ctx, _fnet_short(u), sf, ys, ctx_mod(l, 2), o_wo, lng, lnb, i, l, tm_ctx)
            x = x_new
    return x
```

```python
import functools
import math

import numpy as np
import jax
import jax.numpy as jnp
from jax import lax
from jax.experimental import pallas as pl
from jax.experimental.pallas import tpu as pltpu

F32 = jnp.float32
BF16 = jnp.bfloat16

DEPTH = 4
GRID_W = 64
DEEPNORM_ALPHA = (2 * DEPTH) ** 0.25
LN_EPS = 1e-6

GLA_HEADS = 4
GLA_DK = 64
GLA_DV = 128
GLA_QK_W = GLA_HEADS * GLA_DK
GLA_V_W = GLA_HEADS * GLA_DV
GLA_GATE_RANK = 16
GLA_TAU = 16.0
GLA_CHUNK = 64

MLA_HEADS = 8
MLA_NOPE = 64
MLA_ROPE = 32
MLA_V = 64
MLA_Q_RANK = 256
MLA_KV_RANK = 128
MLA_V_W = MLA_HEADS * MLA_V
MLA_SCALE = (MLA_NOPE + MLA_ROPE) ** -0.5
ROPE_BASE = 10000.0
MLA_HEAD_PAD = 128
MLA_VT_ROWS = MLA_V + 16
MLA_VT_W = MLA_HEADS * MLA_VT_ROWS

FNET_GROUPS = 4
FNET_GROUP_CH = 128
FNET_W = FNET_GROUPS * FNET_GROUP_CH
FFT_N1 = 64

SGU_GROUPS = 4
SGU_GROUP_CH = 128
SGU_W = SGU_GROUPS * SGU_GROUP_CH
SGU_CHUNK = 128

EVEN_IN_SIZES = (GLA_QK_W, GLA_QK_W, GLA_V_W, 2 * GLA_GATE_RANK, GLA_V_W,
                 MLA_Q_RANK, MLA_KV_RANK, MLA_ROPE, MLA_V_W)

E_GQ, E_GK, E_GV, E_GG, E_CQ, E_CKV, E_MG, E_LR, E_KRA, E_KRB, E_END = (
    0, 256, 512, 1024, 1536, 1792, 1920, 2432, 2560, 2688, 2816)

VMEM_LIMIT_BYTES = 56 * 1024 * 1024
LOG2E = math.log2(math.e)
NEG_BIG = -1e30


def _cparams(*sem):
    return pltpu.CompilerParams(dimension_semantics=sem, vmem_limit_bytes=VMEM_LIMIT_BYTES)


def _dot(a, b):
    return jnp.dot(a, b, preferred_element_type=F32)


def _dot_nt(a, b):
    return lax.dot_general(a, b, (((1,), (1,)), ((), ())), preferred_element_type=F32)


def _dot_tn(a, b):
    return lax.dot_general(a, b, (((0,), (0,)), ((), ())), preferred_element_type=F32)


def _ln(x):
    xc = x - jnp.mean(x, -1, keepdims=True)
    return xc * lax.rsqrt(jnp.mean(xc * xc, -1, keepdims=True) + LN_EPS)


def _rms(x):
    return x * lax.rsqrt(jnp.mean(x * x, -1, keepdims=True) + LN_EPS)


def _silu(x):
    return x / (1.0 + jnp.exp(-x))


def _gelu(x):
    return 0.5 * x * (1.0 + lax.erf(x * (2.0 ** -0.5)))


def _tile_lanes(x, reps):
    return jnp.concatenate([x] * reps, axis=1)


def _full(shape):
    n = len(shape)
    return pl.BlockSpec(shape, lambda *_: (0,) * n)


def _layer(arr, i):
    n = arr.ndim
    return pl.BlockSpec((None,) + arr.shape[1:], lambda *_: (i,) + (0,) * (n - 1))


def _mods_kernel(cond_ref, w_ref, b_ref, o_ref):
    s = _silu(cond_ref[...])
    o_ref[0] = _dot(s.astype(BF16), w_ref[0].astype(BF16)) + b_ref[0]


def _mods(cond, ada_w, ada_b):
    depth, d, d3 = ada_w.shape
    return pl.pallas_call(
        _mods_kernel,
        grid=(depth,),
        in_specs=[pl.BlockSpec((8, d), lambda l: (0, 0)),
                  pl.BlockSpec((1, d, d3), lambda l: (l, 0, 0)),
                  pl.BlockSpec((1, 1, d3), lambda l: (l, 0, 0))],
        out_specs=pl.BlockSpec((1, 8, d3), lambda l: (l, 0, 0)),
        out_shape=jax.ShapeDtypeStruct((depth, 8, d3), F32),
        compiler_params=_cparams("arbitrary"),
        name="ada_mod",
    )(cond, ada_w, ada_b.reshape(depth, 1, d3))


def _even_pre_kernel(x_ref, *refs):
    _even_pre_tile(x_ref[0], *refs)


def _even_pre_tile(x, shift_ref, scale_ref, w_ref, w2_ref, gb_ref, qg_ref, wqa_ref, wqb_ref,
                   kvg_ref, wk_ref, wv_ref, vb_ref, ck_ref, sk_ref,
                   q_ref, k_ref, v_ref, g_ref, sgg_ref, smg_ref, qt_ref, kk_ref, vt_ref):
    h = _ln(x) * (1.0 + scale_ref[0]) + shift_ref[0]
    z = _dot(h.astype(BF16), w_ref[...])
    q_ref[0] = z[:, E_GQ:E_GK].astype(BF16)
    k_ref[0] = z[:, E_GK:E_GV].astype(BF16)
    v_ref[0] = z[:, E_GV:E_GG].astype(BF16)
    sgg_ref[0] = _silu(z[:, E_GG:E_CQ]).astype(BF16)
    smg_ref[0] = _silu(z[:, E_MG:E_LR]).astype(BF16)
    pre = _dot(z[:, E_LR:E_KRA].astype(BF16), w2_ref[...]) + gb_ref[...]
    g_ref[0] = jax.nn.log_sigmoid(pre) * (1.0 / GLA_TAU)
    cqn = (_rms(z[:, E_CQ:E_CKV]) * qg_ref[...]).astype(BF16)
    lane = lax.broadcasted_iota(jnp.int32, (1, MLA_HEAD_PAD), 1)
    q_scale = MLA_SCALE * LOG2E
    cq = ck_ref[...] * q_scale + jnp.where(lane < MLA_NOPE, q_scale, 0.0)
    sq = sk_ref[...] * q_scale
    qfull = (_dot(cqn, wqa_ref[...]) * _tile_lanes(cq, MLA_HEADS)
             + _dot(cqn, wqb_ref[...]) * _tile_lanes(sq, MLA_HEADS))
    qt_ref[0] = qfull.T.astype(BF16)
    ckvn = (_rms(z[:, E_CKV:E_MG]) * kvg_ref[...]).astype(BF16)
    kr = z[:, E_KRA:E_KRB] * ck_ref[...] + z[:, E_KRB:E_END] * sk_ref[...]
    kk_ref[0] = (_dot(ckvn, wk_ref[...]) + _tile_lanes(kr, MLA_HEADS)).astype(BF16)
    vt_ref[0] = (_dot(ckvn, wv_ref[...]) + vb_ref[...]).T.astype(BF16)


def _even_pre_outputs(bsz, t, tm):
    hq = MLA_HEADS * MLA_HEAD_PAD
    tok = lambda w: pl.BlockSpec((1, tm, w), lambda b, i: (b, i, 0))
    out_shape = (
        jax.ShapeDtypeStruct((bsz, t, GLA_QK_W), BF16),
        jax.ShapeDtypeStruct((bsz, t, GLA_QK_W), BF16),
        jax.ShapeDtypeStruct((bsz, t, GLA_V_W), BF16),
        jax.ShapeDtypeStruct((bsz, t, 2 * GLA_QK_W), F32),
        jax.ShapeDtypeStruct((bsz, t, GLA_V_W), BF16),
        jax.ShapeDtypeStruct((bsz, t, MLA_V_W), BF16),
        jax.ShapeDtypeStruct((bsz, hq, t), BF16),
        jax.ShapeDtypeStruct((bsz, t, hq), BF16),
        jax.ShapeDtypeStruct((bsz, MLA_VT_W, t), BF16),
    )
    out_specs = (tok(GLA_QK_W), tok(GLA_QK_W), tok(GLA_V_W), tok(2 * GLA_QK_W), tok(GLA_V_W), tok(MLA_V_W),
                 pl.BlockSpec((1, hq, tm), lambda b, i: (b, 0, i)),
                 tok(hq),
                 pl.BlockSpec((1, MLA_VT_W, tm), lambda b, i: (b, 0, i)))
    return out_shape, out_specs


def _even_pre(x, shift, scale, wts, li, tabs, tm):
    bsz, t, d = x.shape
    tok = lambda w: pl.BlockSpec((1, tm, w), lambda b, i: (b, i, 0))
    tab = pl.BlockSpec((tm, MLA_HEAD_PAD), lambda b, i: (i, 0))
    out_shape, out_specs = _even_pre_outputs(bsz, t, tm)
    in_specs = [tok(d), shift[1], scale[1]] + [_layer(w, li) for w in wts] + [tab] * 2
    return pl.pallas_call(
        _even_pre_kernel, grid=(bsz, t // tm), in_specs=in_specs, out_specs=out_specs, out_shape=out_shape,
        compiler_params=_cparams("arbitrary", "arbitrary"), name="even_pre",
    )(x, shift[0], scale[0], *wts, *tabs)


def _gla_block(q, k, v, g, tmat, same_tri, mid_off, end_off, head_masks, chunk_masks):
    r = q.shape[0]
    c = GLA_CHUNK
    nc = r // c
    g_hi = g.astype(BF16)
    g_lo = (g - g_hi.astype(F32)).astype(BF16)
    b = _dot(tmat, g_hi) + _dot(tmat, g_lo)
    mid_rows = [b[j * c + mid_off:j * c + mid_off + 1] for j in range(nc)]
    end_rows = [b[j * c + end_off:j * c + end_off + 1] for j in range(nc)]
    per_chunk = lambda rows: jnp.concatenate([jnp.broadcast_to(x, (c, GLA_QK_W)) for x in rows], axis=0)
    b_mid = per_chunk(mid_rows)
    e_up = jnp.exp(b - b_mid)
    e_dn = jnp.exp(b_mid - b)
    qe = q * e_up
    ke = (k * e_dn).astype(BF16)
    kd_t = (k * (e_dn * per_chunk([jnp.exp(e - m) for e, m in zip(end_rows, mid_rows)]))).T
    qb = q * (e_up * per_chunk([jnp.exp(m) for m in mid_rows]))
    stack = lambda a: jnp.concatenate([jnp.where(m, a, 0.0) for m in head_masks], axis=0).astype(BF16)
    att = _dot_nt(stack(qe), ke)
    qb_st = stack(qb)
    dec_t = jnp.exp(jnp.concatenate(end_rows + [jnp.zeros((8 - nc, GLA_QK_W), F32)], axis=0)).T
    intra, ds = [], []
    for h in range(GLA_HEADS):
        a_h = jnp.where(same_tri, att[h * r:(h + 1) * r], 0.0).astype(BF16)
        v_h = v[:, h * GLA_DV:(h + 1) * GLA_DV]
        intra.append(_dot(a_h, v_h))
        kd_h = kd_t[h * GLA_DK:(h + 1) * GLA_DK]
        lhs = jnp.concatenate([jnp.where(cm, kd_h, 0.0) for cm in chunk_masks], axis=0).astype(BF16)
        ds.append(_dot(lhs, v_h))
    return intra, ds, qb_st, dec_t


def _gla_recur(par, s, o_ref, row0, reverse):
    intra, ds, qb_st, dec_t = par
    c = GLA_CHUNK
    r = intra[0].shape[0]
    nc = r // c
    for j in (reversed(range(nc)) if reverse else range(nc)):
        rows = slice(j * c, (j + 1) * c)
        st = jnp.concatenate([qb_st[h * r + j * c:h * r + (j + 1) * c] for h in range(GLA_HEADS)], axis=0)
        o_inter = _dot(st, s.astype(BF16))
        for h in range(GLA_HEADS):
            o_ref[0, row0 + j * c:row0 + (j + 1) * c, h * GLA_DV:(h + 1) * GLA_DV] = (
                intra[h][rows] + o_inter[h * c:(h + 1) * c]).astype(o_ref.dtype)
        dec = jnp.broadcast_to(dec_t[:, j:j + 1], (GLA_QK_W, GLA_DV))
        s = dec * s + jnp.concatenate([ds[h][rows] for h in range(GLA_HEADS)], axis=0)
    return s


GLA_SUB = 256


def _gla_kernel(qf_ref, kf_ref, vf_ref, gf_ref, qb_ref, kb_ref, vb_ref, gb_ref, s0_ref, tl_ref, tu_ref,
                of_ref, ob_ref, sfin_ref, s_scr, *, tb):
    i = pl.program_id(1)

    @pl.when(i == 0)
    def _():
        s_scr[...] = s0_ref[0]

    c, r = GLA_CHUNK, GLA_SUB
    row = lax.broadcasted_iota(jnp.int32, (r, r), 0)
    col = lax.broadcasted_iota(jnp.int32, (r, r), 1)
    same = (row // c) == (col // c)
    lane_head = lax.broadcasted_iota(jnp.int32, (r, GLA_QK_W), 1) // GLA_DK
    head_masks = [lane_head == h for h in range(GLA_HEADS)]
    lane_chunk = lax.broadcasted_iota(jnp.int32, (GLA_DK, r), 1) // c
    chunk_masks = [lane_chunk == j for j in range(r // c)]

    par_f, par_b = [], []
    for u in range(tb // r):
        rs = slice(u * r, (u + 1) * r)
        par_f.append(_gla_block(qf_ref[0, rs].astype(F32), kf_ref[0, rs].astype(F32), vf_ref[0, rs], gf_ref[0, rs],
                                tl_ref[...], same & (col <= row), c // 2 - 1, c - 1, head_masks, chunk_masks))
        par_b.append(_gla_block(qb_ref[0, rs].astype(F32), kb_ref[0, rs].astype(F32), vb_ref[0, rs], gb_ref[0, rs],
                                tu_ref[...], same & (col >= row), c // 2, 0, head_masks, chunk_masks))
    s = s_scr[0]
    for u in range(tb // r):
        s = _gla_recur(par_f[u], s, of_ref, u * r, False)
    s_scr[0] = s
    s = s_scr[1]
    for u in reversed(range(tb // r)):
        s = _gla_recur(par_b[u], s, ob_ref, u * r, True)
    s_scr[1] = s

    @pl.when(i == pl.num_programs(1) - 1)
    def _():
        sfin_ref[0] = s_scr[...]


def _gla(q, k, v, g, s0, tb):
    bsz, t, _ = q.shape
    nblk = t // tb
    assert tb % GLA_SUB == 0
    idx = np.arange(GLA_SUB)
    same = (idx[:, None] // GLA_CHUNK) == (idx[None, :] // GLA_CHUNK)
    tl = jnp.asarray(same & (idx[None, :] <= idx[:, None]), BF16)
    tu = jnp.asarray(same & (idx[None, :] >= idx[:, None]), BF16)
    fwd = lambda w: pl.BlockSpec((1, tb, w), lambda b, i: (b, i, 0))
    bwd = lambda w: pl.BlockSpec((1, tb, w), lambda b, i: (b, nblk - 1 - i, 0))
    st = pl.BlockSpec((1, 2, GLA_QK_W, GLA_DV), lambda b, i: (b, 0, 0, 0))
    return pl.pallas_call(
        functools.partial(_gla_kernel, tb=tb),
        grid=(bsz, nblk),
        in_specs=[fwd(GLA_QK_W), fwd(GLA_QK_W), fwd(GLA_V_W),
                  pl.BlockSpec((1, tb, GLA_QK_W), lambda b, i: (b, i, 0)),
                  bwd(GLA_QK_W), bwd(GLA_QK_W), bwd(GLA_V_W),
                  pl.BlockSpec((1, tb, GLA_QK_W), lambda b, i: (b, nblk - 1 - i, 1)),
                  st, _full((GLA_SUB, GLA_SUB)), _full((GLA_SUB, GLA_SUB))],
        out_specs=(fwd(GLA_V_W), bwd(GLA_V_W), st),
        out_shape=(jax.ShapeDtypeStruct((bsz, t, GLA_V_W), BF16),
                   jax.ShapeDtypeStruct((bsz, t, GLA_V_W), BF16),
                   jax.ShapeDtypeStruct((bsz, 2, GLA_QK_W, GLA_DV), F32)),
        scratch_shapes=[pltpu.VMEM((2, GLA_QK_W, GLA_DV), F32)],
        compiler_params=_cparams("arbitrary", "arbitrary"), name="gla_scan",
    )(q, k, v, g, q, k, v, g, s0, tl, tu)


MLA_SLOTS = 3
MLA_TK = 256


def _mla_kernel(qt_ref, *refs, chunks, tk):
    n_src = max(src for src, _ in chunks) + 1
    kv_refs, o_ref, scratch = refs[:2 * n_src], refs[2 * n_src], refs[2 * n_src + 1:]
    n_chunks = len(chunks)
    q_t = qt_ref[0]
    tq = q_t.shape[1]
    s_bufs, p_bufs = scratch[:MLA_SLOTS], scratch[MLA_SLOTS:]
    cmax = None
    m = jnp.full((1, tq), NEG_BIG, F32)
    alpha = None
    acc = jnp.zeros((MLA_VT_ROWS, tq), F32)
    for t in range(n_chunks + 2):
        alpha_prev = alpha
        if 1 <= t <= n_chunks:
            e = (t - 1) % MLA_SLOTS
            m_new = jnp.maximum(m, jnp.max(cmax, axis=0, keepdims=True))
            alpha = jnp.exp2(m - m_new)
            m = m_new
            p_bufs[e][...] = jnp.exp2(s_bufs[e][...] - m).astype(BF16)
        if t < n_chunks:
            src, lo = chunks[t]
            s_t = _dot(kv_refs[2 * src][0, lo:lo + tk, :], q_t)
            s_bufs[t % MLA_SLOTS][...] = s_t
            cmax = jnp.max(s_t.reshape(tk // 8, 8, tq), axis=0)
        if t >= 2:
            src, lo = chunks[t - 2]
            vt = kv_refs[2 * src + 1][0, :, lo:lo + tk]
            acc = alpha_prev * acc + _dot(vt, p_bufs[(t - 2) % MLA_SLOTS][...])
    o_ref[0] = (acc[0:MLA_V] / acc[MLA_V:MLA_V + 1]).astype(o_ref.dtype)


def _mla(qt, kv, tq, tk):
    bsz, _, t = qt.shape
    assert t % tq == 0 and all(k.shape[1] % tk == 0 for k, _ in kv)
    chunks = [(src, lo) for src, (k, _) in enumerate(kv) for lo in range(0, k.shape[1], tk)]
    in_specs = [pl.BlockSpec((1, MLA_HEAD_PAD, tq), lambda b, h, i: (b, h, i))]
    for k, _ in kv:
        in_specs += [pl.BlockSpec((1, k.shape[1], MLA_HEAD_PAD), lambda b, h, i: (b, 0, h)),
                     pl.BlockSpec((1, MLA_VT_ROWS, k.shape[1]), lambda b, h, i: (b, h, 0))]
    return pl.pallas_call(
        functools.partial(_mla_kernel, chunks=chunks, tk=tk),
        grid=(bsz, MLA_HEADS, t // tq),
        in_specs=in_specs,
        out_specs=pl.BlockSpec((1, MLA_V, tq), lambda b, h, i: (b, h, i)),
        out_shape=jax.ShapeDtypeStruct((bsz, MLA_V_W, t), BF16),
        scratch_shapes=[pltpu.VMEM((tk, tq), F32)] * MLA_SLOTS + [pltpu.VMEM((tk, tq), BF16)] * MLA_SLOTS,
        compiler_params=_cparams("arbitrary", "arbitrary", "arbitrary"), name="mla_attn",
    )(qt, *[a for pair in kv for a in pair])


def _post_tail(x, y, gate, lng, lnb):
    return _ln(DEEPNORM_ALPHA * x + gate * y) * lng + lnb


def _even_post_tile(x_ref, of_ref, ob_ref, sgg_ref, ot_ref, smg_ref, gate_ref, wo_ref, gng_ref, lng_ref, lnb_ref):
    o = of_ref[0].astype(F32) + ob_ref[0].astype(F32)
    parts = [_rms(o[:, h * GLA_DV:(h + 1) * GLA_DV]) * gng_ref[...] for h in range(GLA_HEADS)]
    yg = jnp.concatenate(parts, axis=1) * sgg_ref[0].astype(F32)
    ym = ot_ref[0].astype(F32).T * smg_ref[0].astype(F32)
    y = _dot(yg.astype(BF16), wo_ref[0:GLA_V_W, :]) + _dot(ym.astype(BF16), wo_ref[GLA_V_W:, :])
    return _post_tail(x_ref[0], y, gate_ref[0], lng_ref[...], lnb_ref[...])


def _even_post_kernel(*refs):
    refs[-1][0] = _even_post_tile(*refs[:-1])


def _even_post_odd_pre_kernel(*refs):
    post_in, pre_par, (x_out_ref, u_ref, sf_ref, ys_ref) = refs[:11], refs[11:17], refs[17:]
    x_new = _even_post_tile(*post_in)
    x_out_ref[0] = x_new
    _odd_pre_tile(x_new, *pre_par, u_ref, sf_ref, ys_ref)


def _even_post(x, o_f, o_b, sgg, o_t, smg, gate, wo, gng, lng, lnb, li, l, tm):
    bsz, t, d = x.shape
    tok = lambda w: pl.BlockSpec((1, tm, w), lambda b, i: (b, i, 0))
    return pl.pallas_call(
        _even_post_kernel, grid=(bsz, t // tm),
        in_specs=[tok(d), tok(GLA_V_W), tok(GLA_V_W), tok(GLA_V_W),
                  pl.BlockSpec((1, MLA_V_W, tm), lambda b, i: (b, 0, i)), tok(MLA_V_W),
                  gate[1],
                  _layer(wo, li), _layer(gng, li), _layer(lng, l), _layer(lnb, l)],
        out_specs=tok(d), out_shape=jax.ShapeDtypeStruct((bsz, t, d), F32),
        compiler_params=_cparams("arbitrary", "arbitrary"), name="even_post",
    )(x, o_f, o_b, sgg, o_t, smg, gate[0], wo, gng, lng, lnb)


def _odd_pre_kernel(x_ref, *refs):
    _odd_pre_tile(x_ref[0], *refs)


def _odd_pre_tile(x, shift_ref, scale_ref, w_ref, dft_ref, sw_ref, sb_ref, u_ref, sf_ref, ys_ref):
    h = _ln(x) * (1.0 + scale_ref[0]) + shift_ref[0]
    z = _dot(h.astype(BF16), w_ref[...])
    for g in range(FNET_GROUPS):
        cs = slice(g * FNET_GROUP_CH, (g + 1) * FNET_GROUP_CH)
        ab = _dot(z[:, cs].astype(BF16), dft_ref[...])
        u_ref[0, :, cs] = ab[:, 0:FNET_GROUP_CH].astype(BF16)
        u_ref[0, :, FNET_W + g * FNET_GROUP_CH:FNET_W + (g + 1) * FNET_GROUP_CH] = ab[:, FNET_GROUP_CH:].astype(BF16)
    sf_ref[0] = _silu(z[:, FNET_W:2 * FNET_W]).astype(BF16)
    o0 = 2 * FNET_W
    tm = z.shape[0]
    for g in range(SGU_GROUPS):
        cs = slice(g * SGU_GROUP_CH, (g + 1) * SGU_GROUP_CH)
        ug = _gelu(z[:, o0 + g * SGU_GROUP_CH:o0 + (g + 1) * SGU_GROUP_CH])
        vg = _ln(_gelu(z[:, o0 + SGU_W + g * SGU_GROUP_CH:o0 + SGU_W + (g + 1) * SGU_GROUP_CH])).astype(BF16)
        sg = _silu(z[:, o0 + 2 * SGU_W + g * SGU_GROUP_CH:o0 + 2 * SGU_W + (g + 1) * SGU_GROUP_CH])
        for c in range(tm // SGU_CHUNK):
            rs = slice(c * SGU_CHUNK, (c + 1) * SGU_CHUNK)
            sv = _dot(sw_ref[g], vg[rs]) + sb_ref[g]
            ys_ref[0, rs, cs] = (ug[rs] * sv * sg[rs]).astype(BF16)


def _even_post_odd_pre(x, o_f, o_b, sgg, o_t, smg, gate, wo, gng, lng, lnb, li, l,
                       shift, scale, w, dft, sw, sb, lj, tm):
    bsz, t, d = x.shape
    tok = lambda w_: pl.BlockSpec((1, tm, w_), lambda b, i: (b, i, 0))
    return pl.pallas_call(
        _even_post_odd_pre_kernel, grid=(bsz, t // tm),
        in_specs=[tok(d), tok(GLA_V_W), tok(GLA_V_W), tok(GLA_V_W),
                  pl.BlockSpec((1, MLA_V_W, tm), lambda b, i: (b, 0, i)), tok(MLA_V_W),
                  gate[1],
                  _layer(wo, li), _layer(gng, li), _layer(lng, l), _layer(lnb, l),
                  shift[1], scale[1], _layer(w, lj), _full(dft.shape), _layer(sw, lj), _layer(sb, lj)],
        out_specs=(tok(d), tok(2 * FNET_W), tok(FNET_W), tok(SGU_W)),
        out_shape=(jax.ShapeDtypeStruct((bsz, t, d), F32),
                   jax.ShapeDtypeStruct((bsz, t, 2 * FNET_W), BF16),
                   jax.ShapeDtypeStruct((bsz, t, FNET_W), BF16),
                   jax.ShapeDtypeStruct((bsz, t, SGU_W), BF16)),
        compiler_params=_cparams("arbitrary", "arbitrary"), name="even_post_odd_pre",
    )(x, o_f, o_b, sgg, o_t, smg, gate[0], wo, gng, lng, lnb, shift[0], scale[0], w, dft, sw, sb)


def _odd_pre(x, shift, scale, w, dft, sw, sb, li, tm):
    bsz, t, d = x.shape
    tok = lambda w_: pl.BlockSpec((1, tm, w_), lambda b, i: (b, i, 0))
    return pl.pallas_call(
        _odd_pre_kernel, grid=(bsz, t // tm),
        in_specs=[tok(d), shift[1], scale[1], _layer(w, li), _full(dft.shape), _layer(sw, li), _layer(sb, li)],
        out_specs=(tok(2 * FNET_W), tok(FNET_W), tok(SGU_W)),
        out_shape=(jax.ShapeDtypeStruct((bsz, t, 2 * FNET_W), BF16),
                   jax.ShapeDtypeStruct((bsz, t, FNET_W), BF16),
                   jax.ShapeDtypeStruct((bsz, t, SGU_W), BF16)),
        compiler_params=_cparams("arbitrary", "arbitrary"), name="odd_pre",
    )(x, shift[0], scale[0], w, dft, sw, sb)


FFT_T2_BLK = 16
FFT_P1_BLK = 8


def _fft1_kernel(u_ref, w1_ref, tc_ref, ts_ref, z_ref):
    n1, tb = FFT_N1, FFT_T2_BLK
    r = n1 * tb
    pq = _dot(w1_ref[...], u_ref[0].reshape(r, 2 * FNET_W))
    zr = pq[0:r, 0:FNET_W] - pq[r:, FNET_W:]
    zi = -pq[0:r, FNET_W:] - pq[r:, 0:FNET_W]
    tc = _tile_lanes(tc_ref[0], FNET_W // 128)
    ts = _tile_lanes(ts_ref[0], FNET_W // 128)
    z_ref[0, :, :, 0:FNET_W] = (zr * tc + zi * ts).astype(BF16).reshape(n1, tb, FNET_W)
    z_ref[0, :, :, FNET_W:] = (zi * tc - zr * ts).astype(BF16).reshape(n1, tb, FNET_W)


def _fft2_kernel(z_ref, c2_ref, s2_ref, y_ref):
    for j in range(FFT_P1_BLK):
        zp = z_ref[0, j]
        y_ref[0, :, j, :] = _dot(c2_ref[...], zp[:, 0:FNET_W]) + _dot(s2_ref[...], zp[:, FNET_W:])


def _dft_tables(t):
    n1, n2 = FFT_N1, t // FFT_N1
    p1 = np.arange(n1, dtype=np.float64)
    a1 = 2.0 * np.pi * np.outer(p1, p1) / n1
    eye = np.eye(FFT_T2_BLK)
    w1 = np.concatenate([np.kron(np.cos(a1), eye), np.kron(np.sin(a1), eye)], axis=0)
    at = 2.0 * np.pi * np.outer(p1, np.arange(n2, dtype=np.float64)) / t

    def twiddle(a):
        a = a.reshape(n1, n2 // FFT_T2_BLK, FFT_T2_BLK).transpose(1, 0, 2).reshape(n2 // FFT_T2_BLK, -1)
        return np.repeat(a[:, :, None], 128, axis=2)
    tc, ts = twiddle(np.cos(at)), twiddle(np.sin(at))
    p2 = np.arange(n2, dtype=np.float64)
    a2 = 2.0 * np.pi * np.outer(p2, p2) / n2
    norm = 1.0 / math.sqrt(t * FNET_GROUP_CH)
    return (jnp.asarray(w1, F32).astype(BF16), jnp.asarray(tc, F32), jnp.asarray(ts, F32),
            jnp.asarray(np.cos(a2) * norm, F32).astype(BF16), jnp.asarray(np.sin(a2) * norm, F32).astype(BF16))


def _channel_dft_matrix():
    d = np.arange(FNET_GROUP_CH, dtype=np.float64)
    a = 2.0 * np.pi * np.outer(d, d) / FNET_GROUP_CH
    return jnp.asarray(np.concatenate([np.cos(a), np.sin(a)], axis=1), F32).astype(BF16)


def _fnet_long(u):
    bsz, t, w2 = u.shape
    n1, n2 = FFT_N1, t // FFT_N1
    w1, tc, ts, c2, s2 = _dft_tables(t)
    blk = pl.BlockSpec((1, n1, FFT_T2_BLK, w2), lambda b, i: (b, 0, i, 0))
    tw = pl.BlockSpec((1, n1 * FFT_T2_BLK, 128), lambda b, i: (i, 0, 0))
    z = pl.pallas_call(
        _fft1_kernel, grid=(bsz, n2 // FFT_T2_BLK),
        in_specs=[blk, _full(w1.shape), tw, tw],
        out_specs=blk,
        out_shape=jax.ShapeDtypeStruct((bsz, n1, n2, w2), BF16),
        compiler_params=_cparams("arbitrary", "arbitrary"), name="fnet_stage1",
    )(u.reshape(bsz, n1, n2, w2), w1, tc, ts)
    y = pl.pallas_call(
        _fft2_kernel, grid=(bsz, n1 // FFT_P1_BLK),
        in_specs=[pl.BlockSpec((1, FFT_P1_BLK, n2, w2), lambda b, i: (b, i, 0, 0)),
                  _full(c2.shape), _full(s2.shape)],
        out_specs=pl.BlockSpec((1, n2, FFT_P1_BLK, FNET_W), lambda b, i: (b, 0, i, 0)),
        out_shape=jax.ShapeDtypeStruct((bsz, n2, n1, FNET_W), F32),
        compiler_params=_cparams("arbitrary", "arbitrary"), name="fnet_stage2",
    )(z, c2, s2)
    return y.reshape(bsz, t, FNET_W)


def _fnet_short_kernel(u_ref, c_ref, s_ref, y_ref):
    u = u_ref[0]
    y_ref[0] = _dot(c_ref[...], u[:, 0:FNET_W]) - _dot(s_ref[...], u[:, FNET_W:])


def _fnet_short(u):
    bsz, t, w2 = u.shape
    p = np.arange(t, dtype=np.float64)
    a = 2.0 * np.pi * np.outer(p, p) / t
    norm = 1.0 / math.sqrt(t * FNET_GROUP_CH)
    c, s = jnp.asarray(np.cos(a) * norm, F32).astype(BF16), jnp.asarray(np.sin(a) * norm, F32).astype(BF16)
    return pl.pallas_call(
        _fnet_short_kernel, grid=(bsz,),
        in_specs=[pl.BlockSpec((1, t, w2), lambda b: (b, 0, 0)), _full(c.shape), _full(s.shape)],
        out_specs=pl.BlockSpec((1, t, FNET_W), lambda b: (b, 0, 0)),
        out_shape=jax.ShapeDtypeStruct((bsz, t, FNET_W), F32),
        compiler_params=_cparams("arbitrary"), name="fnet_short",
    )(u, c, s)


def _odd_post_tile(x_ref, fr_ref, sf_ref, ys_ref, gate_ref, wo_ref, lng_ref, lnb_ref):
    yf = (fr_ref[0] * sf_ref[0].astype(F32)).astype(BF16)
    y = _dot(yf, wo_ref[0:FNET_W, :]) + _dot(ys_ref[0], wo_ref[FNET_W:, :])
    return _post_tail(x_ref[0], y, gate_ref[0], lng_ref[...], lnb_ref[...])


def _odd_post_kernel(*refs):
    refs[-1][0] = _odd_post_tile(*refs[:-1])


def _odd_post_even_pre_kernel(*refs):
    post_in, pre_par, x_out_ref, pre_out = refs[:8], refs[8:22], refs[22], refs[23:]
    x_new = _odd_post_tile(*post_in)
    x_out_ref[0] = x_new
    _even_pre_tile(x_new, *pre_par, *pre_out)


def _odd_post_even_pre(x, fr, sf, ys, gate, wo, lng, lnb, li, l, shift, scale, wts, lj, tabs, tm):
    bsz, t, d = x.shape
    hq = MLA_HEADS * MLA_HEAD_PAD
    tok = lambda w: pl.BlockSpec((1, tm, w), lambda b, i: (b, i, 0))
    tab = pl.BlockSpec((tm, MLA_HEAD_PAD), lambda b, i: (i, 0))
    pre_shapes, pre_specs = _even_pre_outputs(bsz, t, tm)
    return pl.pallas_call(
        _odd_post_even_pre_kernel, grid=(bsz, t // tm),
        in_specs=[tok(d), tok(FNET_W), tok(FNET_W), tok(SGU_W), gate[1],
                  _layer(wo, li), _layer(lng, l), _layer(lnb, l),
                  shift[1], scale[1]] + [_layer(w, lj) for w in wts] + [tab] * 2,
        out_specs=(tok(d),) + pre_specs,
        out_shape=(jax.ShapeDtypeStruct((bsz, t, d), F32),) + pre_shapes,
        compiler_params=_cparams("arbitrary", "arbitrary"), name="odd_post_even_pre",
    )(x, fr, sf, ys, gate[0], wo, lng, lnb, shift[0], scale[0], *wts, *tabs)


def _odd_post(x, fr, sf, ys, gate, wo, lng, lnb, li, l, tm):
    bsz, t, d = x.shape
    tok = lambda w: pl.BlockSpec((1, tm, w), lambda b, i: (b, i, 0))
    return pl.pallas_call(
        _odd_post_kernel, grid=(bsz, t // tm),
        in_specs=[tok(d), tok(FNET_W), tok(FNET_W), tok(SGU_W),
                  gate[1],
                  _layer(wo, li), _layer(lng, l), _layer(lnb, l)],
        out_specs=tok(d), out_shape=jax.ShapeDtypeStruct((bsz, t, d), F32),
        compiler_params=_cparams("arbitrary", "arbitrary"), name="odd_post",
    )(x, fr, sf, ys, gate[0], wo, lng, lnb)


def _even_weights(w_in, gla_w2, gla_b, q_norm_g, w_uq, kv_norm_g, w_ukv):
    d = w_in.shape[0]
    idx = np.cumsum(EVEN_IN_SIZES)[:-1].tolist()
    gq, gk, gv, glr, gg, cq, ckv, kr, mg = jnp.split(w_in, idx, axis=1)
    half = MLA_ROPE // 2
    z = lambda n: jnp.zeros((d, n), w_in.dtype)
    lr_pad = jnp.concatenate([glr, z(128 - 2 * GLA_GATE_RANK)], axis=1)
    kra = jnp.concatenate([z(MLA_NOPE), kr, z(MLA_HEAD_PAD - MLA_NOPE - MLA_ROPE)], axis=1)
    krb = jnp.concatenate([z(MLA_NOPE), kr[:, half:], kr[:, :half], z(MLA_HEAD_PAD - MLA_NOPE - MLA_ROPE)], axis=1)
    w = jnp.concatenate([gq * GLA_DK ** -0.5, gk, gv, gg, cq, ckv, mg, lr_pad, kra, krb], axis=1).astype(BF16)
    zw = jnp.zeros((GLA_GATE_RANK, GLA_QK_W), F32)
    w2 = jnp.concatenate([jnp.concatenate([gla_w2[0], zw], axis=1), jnp.concatenate([zw, gla_w2[1]], axis=1),
                          jnp.zeros((128 - 2 * GLA_GATE_RANK, 2 * GLA_QK_W), F32)], axis=0)
    gb = jnp.concatenate([gla_b[0], gla_b[1]])[None, :]
    uq = w_uq.reshape(MLA_Q_RANK, MLA_HEADS, MLA_NOPE + MLA_ROPE)
    pad = MLA_HEAD_PAD - MLA_NOPE - MLA_ROPE
    zq = lambda n: jnp.zeros((MLA_Q_RANK, MLA_HEADS, n), w_uq.dtype)
    wqa = jnp.concatenate([uq, zq(pad)], axis=2).reshape(MLA_Q_RANK, -1)
    wqb = jnp.concatenate([zq(MLA_NOPE), uq[:, :, MLA_NOPE + half:], uq[:, :, MLA_NOPE:MLA_NOPE + half], zq(pad)],
                          axis=2).reshape(MLA_Q_RANK, -1)
    ukv = w_ukv.reshape(MLA_KV_RANK, MLA_HEADS, MLA_NOPE + MLA_V)
    wk = jnp.concatenate([ukv[:, :, :MLA_NOPE], jnp.zeros((MLA_KV_RANK, MLA_HEADS, MLA_HEAD_PAD - MLA_NOPE), w_ukv.dtype)],
                         axis=2).reshape(MLA_KV_RANK, -1)
    vpad = MLA_VT_ROWS - MLA_V
    wv = jnp.concatenate([ukv[:, :, MLA_NOPE:], jnp.zeros((MLA_KV_RANK, MLA_HEADS, vpad), w_ukv.dtype)],
                         axis=2).reshape(MLA_KV_RANK, -1)
    vbias = np.zeros((MLA_HEADS, MLA_VT_ROWS), np.float32)
    vbias[:, MLA_V] = 1.0
    return (w, w2.astype(BF16), gb, q_norm_g[None, :], wqa.astype(BF16), wqb.astype(BF16),
            kv_norm_g[None, :], wk.astype(BF16), wv.astype(BF16), jnp.asarray(vbias.reshape(1, -1)))


def _rope_tables(n):
    row = jnp.repeat(jnp.arange(n // GRID_W, dtype=F32), GRID_W)
    col = (jnp.arange(n) % GRID_W).astype(F32)
    n_freq = MLA_ROPE // 4
    inv = ROPE_BASE ** (-jnp.arange(n_freq, dtype=F32) / n_freq)
    ang = jnp.concatenate([row[:, None] * inv, col[:, None] * inv], -1)
    cos, sin = jnp.cos(ang), jnp.sin(ang)
    pad = jnp.zeros((n, MLA_HEAD_PAD - MLA_NOPE - MLA_ROPE), F32)
    zn = jnp.zeros((n, MLA_NOPE), F32)
    ck = jnp.concatenate([zn, cos, cos, pad], axis=1)
    sk = jnp.concatenate([zn, -sin, sin, pad], axis=1)
    return ck, sk


def _plain_tables(n):
    pad = jnp.zeros((n, MLA_HEAD_PAD - MLA_NOPE - MLA_ROPE), F32)
    zn = jnp.zeros((n, MLA_NOPE), F32)
    ck = jnp.concatenate([zn, jnp.ones((n, MLA_ROPE), F32), pad], axis=1)
    return ck, jnp.zeros((n, MLA_HEAD_PAD), F32)


def _pick(t, pref):
    return pref if t % pref == 0 else t


def kernel(x, c, ctx, c_ctx, ada_w, ada_b, post_ln_g, post_ln_b, even_w_in, gla_w2, gla_b, gla_norm_g,
           mla_q_norm_g, mla_w_uq, mla_kv_norm_g, mla_w_ukv, even_w_out, odd_w_in, sgu_w, sgu_b, odd_w_out):
    bsz, n, d = x.shape
    lc = ctx.shape[1]
    depth = ada_w.shape[0]
    assert bsz + 1 <= 8 and n % (FFT_N1 * FFT_T2_BLK) == 0 and n % 512 == 0 and lc % MLA_TK == 0

    cond = jnp.concatenate([c, c_ctx[None, :], jnp.zeros((8 - bsz - 1, d), F32)], axis=0)
    mods = _mods(cond, ada_w, ada_b)

    mods4 = mods.reshape(depth, 8, 1, 3 * d)

    def lat_mod(l, j):
        return mods4, pl.BlockSpec((None, 1, 1, d), lambda b, *_: (l, b, 0, j))

    def ctx_mod(l, j):
        return mods4, pl.BlockSpec((None, 1, 1, d), lambda b, *_: (l, bsz, 0, j))

    rope_tabs = _rope_tables(n)
    ctx_tabs = _plain_tables(lc)
    dft_c = _channel_dft_matrix()
    tm_lat, tm_ctx = _pick(n, 512), _pick(lc, 256)
    tm_post = _pick(n, 1024)
    zero_state = jnp.zeros((bsz, 2, GLA_QK_W, GLA_DV), F32)

    e_wts = jax.vmap(_even_weights)(even_w_in, gla_w2, gla_b, mla_q_norm_g, mla_w_uq, mla_kv_norm_g, mla_w_ukv)
    e_wo = even_w_out.astype(BF16)
    e_gng = gla_norm_g[:, None, :]
    o_w = odd_w_in.astype(BF16)
    o_wo = odd_w_out.astype(BF16)
    o_sw = sgu_w.astype(BF16)
    o_sb = jnp.broadcast_to(sgu_b[:, :, :, None], sgu_b.shape + (SGU_GROUP_CH,))
    lng, lnb = post_ln_g[:, None, :], post_ln_b[:, None, :]

    odd_in = even_in = None
    for l in range(depth):
        need_ctx_out = any(j % 2 == 0 for j in range(l + 1, depth))
        i = l // 2
        if l % 2 == 0:
            pc = _even_pre(ctx, ctx_mod(l, 0), ctx_mod(l, 1), e_wts, i, ctx_tabs, tm_ctx)
            q_c, k_c, v_c, g_c, sgg_c, smg_c, qt_c, kk_c, vt_c = pc
            if even_in is None:
                even_in = _even_pre(x, lat_mod(l, 0), lat_mod(l, 1), e_wts, i, rope_tabs, tm_lat)
            q_l, k_l, v_l, g_l, sgg_l, smg_l, qt_l, kk_l, vt_l = even_in
            even_in = None
            of_c, ob_c, s_c = _gla(q_c, k_c, v_c, g_c, zero_state, GLA_SUB)
            of_l, ob_l, _ = _gla(q_l, k_l, v_l, g_l, s_c, 2 * GLA_SUB)
            ot_l = _mla(qt_l, [(kk_l, vt_l), (kk_c, vt_c)], _pick(n, 512), MLA_TK)
            if l + 1 < depth:
                x_new, *odd_in = _even_post_odd_pre(
                    x, of_l, ob_l, sgg_l, ot_l, smg_l, lat_mod(l, 2), e_wo, e_gng, lng, lnb, i, l,
                    lat_mod(l + 1, 0), lat_mod(l + 1, 1), o_w, dft_c, o_sw, o_sb, (l + 1) // 2, tm_lat)
            else:
                x_new = _even_post(x, of_l, ob_l, sgg_l, ot_l, smg_l, lat_mod(l, 2), e_wo, e_gng, lng, lnb,
                                   i, l, tm_post)
            if need_ctx_out:
                ot_c = _mla(qt_c, [(kk_c, vt_c)], lc, MLA_TK)
                ctx = _even_post(ctx, of_c, ob_c, sgg_c, ot_c, smg_c, ctx_mod(l, 2), e_wo, e_gng, lng, lnb, i, l, tm_ctx)
            x = x_new
        else:
            if odd_in is None:
                odd_in = _odd_pre(x, lat_mod(l, 0), lat_mod(l, 1), o_w, dft_c, o_sw, o_sb, i, tm_lat)
            u, sf, ys = odd_in
            odd_in = None
            if l + 1 < depth:
                x_new, *even_in = _odd_post_even_pre(
                    x, _fnet_long(u), sf, ys, lat_mod(l, 2), o_wo, lng, lnb, i, l,
                    lat_mod(l + 1, 0), lat_mod(l + 1, 1), e_wts, (l + 1) // 2, rope_tabs, tm_lat)
            else:
                x_new = _odd_post(x, _fnet_long(u), sf, ys, lat_mod(l, 2), o_wo, lng, lnb, i, l, tm_post)
            if need_ctx_out:
                u, sf, ys = _odd_pre(ctx, ctx_mod(l, 0), ctx_mod(l, 1), o_w, dft_c, o_sw, o_sb, i, tm_ctx)
                ctx = _odd_post(ctx, _fnet_short(u), sf, ys, ctx_mod(l, 2), o_wo, lng, lnb, i, l, tm_ctx)
            x = x_new
    return x
```

```python
import functools
import math

import numpy as np
import jax
import jax.numpy as jnp
from jax import lax
from jax.experimental import pallas as pl
from jax.experimental.pallas import tpu as pltpu

F32 = jnp.float32
BF16 = jnp.bfloat16

DEPTH = 4
GRID_W = 64
DEEPNORM_ALPHA = (2 * DEPTH) ** 0.25
LN_EPS = 1e-6

GLA_HEADS = 4
GLA_DK = 64
GLA_DV = 128
GLA_QK_W = GLA_HEADS * GLA_DK
GLA_V_W = GLA_HEADS * GLA_DV
GLA_GATE_RANK = 16
GLA_TAU = 16.0
GLA_CHUNK = 64

MLA_HEADS = 8
MLA_NOPE = 64
MLA_ROPE = 32
MLA_V = 64
MLA_Q_RANK = 256
MLA_KV_RANK = 128
MLA_V_W = MLA_HEADS * MLA_V
MLA_SCALE = (MLA_NOPE + MLA_ROPE) ** -0.5
ROPE_BASE = 10000.0
MLA_HEAD_PAD = 128
MLA_VT_ROWS = MLA_V + 16
MLA_VT_W = MLA_HEADS * MLA_VT_ROWS

FNET_GROUPS = 4
FNET_GROUP_CH = 128
FNET_W = FNET_GROUPS * FNET_GROUP_CH
FFT_N1 = 64

SGU_GROUPS = 4
SGU_GROUP_CH = 128
SGU_W = SGU_GROUPS * SGU_GROUP_CH
SGU_CHUNK = 128

EVEN_IN_SIZES = (GLA_QK_W, GLA_QK_W, GLA_V_W, 2 * GLA_GATE_RANK, GLA_V_W,
                 MLA_Q_RANK, MLA_KV_RANK, MLA_ROPE, MLA_V_W)

E_GQ, E_GK, E_GV, E_GG, E_CQ, E_CKV, E_MG, E_SMALL, E_END = (
    0, 256, 512, 1024, 1536, 1792, 1920, 2432, 2560)
SM_KR = 2 * GLA_GATE_RANK
SM_KR_SWAPPED = MLA_NOPE

VMEM_LIMIT_BYTES = 56 * 1024 * 1024
LOG2E = math.log2(math.e)
NEG_BIG = -1e30


def _cparams(*sem):
    return pltpu.CompilerParams(dimension_semantics=sem, vmem_limit_bytes=VMEM_LIMIT_BYTES)


def _dot(a, b):
    return jnp.dot(a, b, preferred_element_type=F32)


def _dot_nt(a, b):
    return lax.dot_general(a, b, (((1,), (1,)), ((), ())), preferred_element_type=F32)


def _dot_tn(a, b):
    return lax.dot_general(a, b, (((0,), (0,)), ((), ())), preferred_element_type=F32)


def _ln(x):
    xc = x - jnp.mean(x, -1, keepdims=True)
    return xc * lax.rsqrt(jnp.mean(xc * xc, -1, keepdims=True) + LN_EPS)


def _rms(x):
    return x * lax.rsqrt(jnp.mean(x * x, -1, keepdims=True) + LN_EPS)


def _silu(x):
    return x / (1.0 + jnp.exp(-x))


def _gelu(x):
    return 0.5 * x * (1.0 + lax.erf(x * (2.0 ** -0.5)))


def _tile_lanes(x, reps):
    return jnp.concatenate([x] * reps, axis=1)


def _full(shape):
    n = len(shape)
    return pl.BlockSpec(shape, lambda *_: (0,) * n)


def _layer(arr, i):
    n = arr.ndim
    return pl.BlockSpec((None,) + arr.shape[1:], lambda *_: (i,) + (0,) * (n - 1))


def _mods_kernel(cond_ref, w_ref, b_ref, o_ref):
    s = _silu(cond_ref[...])
    o_ref[0] = _dot(s.astype(BF16), w_ref[0].astype(BF16)) + b_ref[0]


def _mods(cond, ada_w, ada_b):
    depth, d, d3 = ada_w.shape
    return pl.pallas_call(
        _mods_kernel,
        grid=(depth,),
        in_specs=[pl.BlockSpec((8, d), lambda l: (0, 0)),
                  pl.BlockSpec((1, d, d3), lambda l: (l, 0, 0)),
                  pl.BlockSpec((1, 1, d3), lambda l: (l, 0, 0))],
        out_specs=pl.BlockSpec((1, 8, d3), lambda l: (l, 0, 0)),
        out_shape=jax.ShapeDtypeStruct((depth, 8, d3), F32),
        compiler_params=_cparams("arbitrary"),
        name="ada_mod",
    )(cond, ada_w, ada_b.reshape(depth, 1, d3))


def _even_pre_kernel(x_ref, *refs):
    _even_pre_tile(x_ref[0], *refs)


def _even_pre_tile(x, shift_ref, scale_ref, w_ref, w2_ref, gb_ref, qg_ref, wqa_ref, wqb_ref,
                   kvg_ref, wk_ref, wv_ref, vb_ref, ck_ref, sk_ref, place_ref,
                   q_ref, k_ref, v_ref, g_ref, sgg_ref, smg_ref, qt_ref, kk_ref, vt_ref):
    h = _ln(x) * (1.0 + scale_ref[0]) + shift_ref[0]
    z = _dot(h.astype(BF16), w_ref[...])
    q_ref[0] = z[:, E_GQ:E_GK].astype(BF16)
    k_ref[0] = z[:, E_GK:E_GV].astype(BF16)
    v_ref[0] = z[:, E_GV:E_GG].astype(BF16)
    sgg_ref[0] = _silu(z[:, E_GG:E_CQ]).astype(BF16)
    smg_ref[0] = _silu(z[:, E_MG:E_SMALL]).astype(BF16)
    small = z[:, E_SMALL:E_END]
    small16 = small.astype(BF16)
    pre = _dot(small16, w2_ref[...]) + gb_ref[...]
    g_ref[0] = jax.nn.log_sigmoid(pre) * (1.0 / GLA_TAU)
    cqn = (_rms(z[:, E_CQ:E_CKV]) * qg_ref[...]).astype(BF16)
    lane = lax.broadcasted_iota(jnp.int32, (1, MLA_HEAD_PAD), 1)
    q_scale = MLA_SCALE * LOG2E
    cq = ck_ref[...] * q_scale + jnp.where(lane < MLA_NOPE, q_scale, 0.0)
    sq = sk_ref[...] * q_scale
    qfull = (_dot(cqn, wqa_ref[...]) * _tile_lanes(cq, MLA_HEADS)
             + _dot(cqn, wqb_ref[...]) * _tile_lanes(sq, MLA_HEADS))
    qt_ref[0] = qfull.T.astype(BF16)
    ckvn = (_rms(z[:, E_CKV:E_MG]) * kvg_ref[...]).astype(BF16)
    kr = _dot(small16, place_ref[...]) * ck_ref[...] + small * sk_ref[...]
    kk_ref[0] = (_dot(ckvn, wk_ref[...]) + _tile_lanes(kr, MLA_HEADS)).astype(BF16)
    vt_ref[0] = (_dot(ckvn, wv_ref[...]) + vb_ref[...]).T.astype(BF16)


def _even_pre_outputs(bsz, t, tm):
    hq = MLA_HEADS * MLA_HEAD_PAD
    tok = lambda w: pl.BlockSpec((1, tm, w), lambda b, i: (b, i, 0))
    out_shape = (
        jax.ShapeDtypeStruct((bsz, t, GLA_QK_W), BF16),
        jax.ShapeDtypeStruct((bsz, t, GLA_QK_W), BF16),
        jax.ShapeDtypeStruct((bsz, t, GLA_V_W), BF16),
        jax.ShapeDtypeStruct((bsz, t, 2 * GLA_QK_W), F32),
        jax.ShapeDtypeStruct((bsz, t, GLA_V_W), BF16),
        jax.ShapeDtypeStruct((bsz, t, MLA_V_W), BF16),
        jax.ShapeDtypeStruct((bsz, hq, t), BF16),
        jax.ShapeDtypeStruct((bsz, t, hq), BF16),
        jax.ShapeDtypeStruct((bsz, MLA_VT_W, t), BF16),
    )
    out_specs = (tok(GLA_QK_W), tok(GLA_QK_W), tok(GLA_V_W), tok(2 * GLA_QK_W), tok(GLA_V_W), tok(MLA_V_W),
                 pl.BlockSpec((1, hq, tm), lambda b, i: (b, 0, i)),
                 tok(hq),
                 pl.BlockSpec((1, MLA_VT_W, tm), lambda b, i: (b, 0, i)))
    return out_shape, out_specs


def _even_pre(x, shift, scale, wts, li, tabs, tm):
    bsz, t, d = x.shape
    tok = lambda w: pl.BlockSpec((1, tm, w), lambda b, i: (b, i, 0))
    tab = pl.BlockSpec((tm, MLA_HEAD_PAD), lambda b, i: (i, 0))
    out_shape, out_specs = _even_pre_outputs(bsz, t, tm)
    place = _rotary_key_placement()
    in_specs = [tok(d), shift[1], scale[1]] + [_layer(w, li) for w in wts] + [tab] * 2 + [_full(place.shape)]
    return pl.pallas_call(
        _even_pre_kernel, grid=(bsz, t // tm), in_specs=in_specs, out_specs=out_specs, out_shape=out_shape,
        compiler_params=_cparams("arbitrary", "arbitrary"), name="even_pre",
    )(x, shift[0], scale[0], *wts, *tabs, place)


def _rotary_key_placement():
    m = np.zeros((128, MLA_HEAD_PAD), np.float32)
    m[SM_KR + np.arange(MLA_ROPE), MLA_NOPE + np.arange(MLA_ROPE)] = 1.0
    return jnp.asarray(m, BF16)


def _gla_block(q, k, v, g, tmat, same_tri, mid_off, end_off, head_masks, chunk_masks):
    r = q.shape[0]
    c = GLA_CHUNK
    nc = r // c
    g_hi = g.astype(BF16)
    g_lo = (g - g_hi.astype(F32)).astype(BF16)
    b = _dot(tmat, g_hi) + _dot(tmat, g_lo)
    mid_rows = [b[j * c + mid_off:j * c + mid_off + 1] for j in range(nc)]
    end_rows = [b[j * c + end_off:j * c + end_off + 1] for j in range(nc)]
    per_chunk = lambda rows: jnp.concatenate([jnp.broadcast_to(x, (c, GLA_QK_W)) for x in rows], axis=0)
    b_mid = per_chunk(mid_rows)
    e_up = jnp.exp(b - b_mid)
    e_dn = jnp.exp(b_mid - b)
    qe = q * e_up
    ke = (k * e_dn).astype(BF16)
    kd_t = (k * (e_dn * per_chunk([jnp.exp(e - m) for e, m in zip(end_rows, mid_rows)]))).T
    qb = q * (e_up * per_chunk([jnp.exp(m) for m in mid_rows]))
    stack = lambda a: jnp.concatenate([jnp.where(m, a, 0.0) for m in head_masks], axis=0).astype(BF16)
    att = _dot_nt(stack(qe), ke)
    qb_st = stack(qb)
    dec_t = jnp.exp(jnp.concatenate(end_rows + [jnp.zeros((8 - nc, GLA_QK_W), F32)], axis=0)).T
    intra, ds = [], []
    for h in range(GLA_HEADS):
        a_h = jnp.where(same_tri, att[h * r:(h + 1) * r], 0.0).astype(BF16)
        v_h = v[:, h * GLA_DV:(h + 1) * GLA_DV]
        intra.append(_dot(a_h, v_h))
        kd_h = kd_t[h * GLA_DK:(h + 1) * GLA_DK]
        lhs = jnp.concatenate([jnp.where(cm, kd_h, 0.0) for cm in chunk_masks], axis=0).astype(BF16)
        ds.append(_dot(lhs, v_h))
    return intra, ds, qb_st, dec_t


def _gla_recur(par, s, o_ref, row0, reverse):
    intra, ds, qb_st, dec_t = par
    c = GLA_CHUNK
    r = intra[0].shape[0]
    nc = r // c
    for j in (reversed(range(nc)) if reverse else range(nc)):
        rows = slice(j * c, (j + 1) * c)
        st = jnp.concatenate([qb_st[h * r + j * c:h * r + (j + 1) * c] for h in range(GLA_HEADS)], axis=0)
        o_inter = _dot(st, s.astype(BF16))
        for h in range(GLA_HEADS):
            o_ref[0, row0 + j * c:row0 + (j + 1) * c, h * GLA_DV:(h + 1) * GLA_DV] = (
                intra[h][rows] + o_inter[h * c:(h + 1) * c]).astype(o_ref.dtype)
        dec = jnp.broadcast_to(dec_t[:, j:j + 1], (GLA_QK_W, GLA_DV))
        s = dec * s + jnp.concatenate([ds[h][rows] for h in range(GLA_HEADS)], axis=0)
    return s


GLA_SUB = 256


def _gla_kernel(qf_ref, kf_ref, vf_ref, gf_ref, qb_ref, kb_ref, vb_ref, gb_ref, s0_ref, tl_ref, tu_ref,
                of_ref, ob_ref, sfin_ref, s_scr, *, tb):
    i = pl.program_id(1)

    @pl.when(i == 0)
    def _():
        s_scr[...] = s0_ref[0]

    c, r = GLA_CHUNK, GLA_SUB
    row = lax.broadcasted_iota(jnp.int32, (r, r), 0)
    col = lax.broadcasted_iota(jnp.int32, (r, r), 1)
    same = (row // c) == (col // c)
    lane_head = lax.broadcasted_iota(jnp.int32, (r, GLA_QK_W), 1) // GLA_DK
    head_masks = [lane_head == h for h in range(GLA_HEADS)]
    lane_chunk = lax.broadcasted_iota(jnp.int32, (GLA_DK, r), 1) // c
    chunk_masks = [lane_chunk == j for j in range(r // c)]

    par_f, par_b = [], []
    for u in range(tb // r):
        rs = slice(u * r, (u + 1) * r)
        par_f.append(_gla_block(qf_ref[0, rs].astype(F32), kf_ref[0, rs].astype(F32), vf_ref[0, rs], gf_ref[0, rs],
                                tl_ref[...], same & (col <= row), c // 2 - 1, c - 1, head_masks, chunk_masks))
        par_b.append(_gla_block(qb_ref[0, rs].astype(F32), kb_ref[0, rs].astype(F32), vb_ref[0, rs], gb_ref[0, rs],
                                tu_ref[...], same & (col >= row), c // 2, 0, head_masks, chunk_masks))
    s = s_scr[0]
    for u in range(tb // r):
        s = _gla_recur(par_f[u], s, of_ref, u * r, False)
    s_scr[0] = s
    s = s_scr[1]
    for u in reversed(range(tb // r)):
        s = _gla_recur(par_b[u], s, ob_ref, u * r, True)
    s_scr[1] = s

    @pl.when(i == pl.num_programs(1) - 1)
    def _():
        sfin_ref[0] = s_scr[...]


def _gla(q, k, v, g, s0, tb):
    bsz, t, _ = q.shape
    nblk = t // tb
    assert tb % GLA_SUB == 0
    idx = np.arange(GLA_SUB)
    same = (idx[:, None] // GLA_CHUNK) == (idx[None, :] // GLA_CHUNK)
    tl = jnp.asarray(same & (idx[None, :] <= idx[:, None]), BF16)
    tu = jnp.asarray(same & (idx[None, :] >= idx[:, None]), BF16)
    fwd = lambda w: pl.BlockSpec((1, tb, w), lambda b, i: (b, i, 0))
    bwd = lambda w: pl.BlockSpec((1, tb, w), lambda b, i: (b, nblk - 1 - i, 0))
    st = pl.BlockSpec((1, 2, GLA_QK_W, GLA_DV), lambda b, i: (b, 0, 0, 0))
    return pl.pallas_call(
        functools.partial(_gla_kernel, tb=tb),
        grid=(bsz, nblk),
        in_specs=[fwd(GLA_QK_W), fwd(GLA_QK_W), fwd(GLA_V_W),
                  pl.BlockSpec((1, tb, GLA_QK_W), lambda b, i: (b, i, 0)),
                  bwd(GLA_QK_W), bwd(GLA_QK_W), bwd(GLA_V_W),
                  pl.BlockSpec((1, tb, GLA_QK_W), lambda b, i: (b, nblk - 1 - i, 1)),
                  st, _full((GLA_SUB, GLA_SUB)), _full((GLA_SUB, GLA_SUB))],
        out_specs=(fwd(GLA_V_W), bwd(GLA_V_W), st),
        out_shape=(jax.ShapeDtypeStruct((bsz, t, GLA_V_W), BF16),
                   jax.ShapeDtypeStruct((bsz, t, GLA_V_W), BF16),
                   jax.ShapeDtypeStruct((bsz, 2, GLA_QK_W, GLA_DV), F32)),
        scratch_shapes=[pltpu.VMEM((2, GLA_QK_W, GLA_DV), F32)],
        compiler_params=_cparams("arbitrary", "arbitrary"), name="gla_scan",
    )(q, k, v, g, q, k, v, g, s0, tl, tu)


MLA_SLOTS = 3
MLA_TK = 256


def _mla_kernel(qt_ref, *refs, chunks, tk):
    n_src = max(src for src, _ in chunks) + 1
    kv_refs, o_ref, scratch = refs[:2 * n_src], refs[2 * n_src], refs[2 * n_src + 1:]
    n_chunks = len(chunks)
    q_t = qt_ref[0]
    tq = q_t.shape[1]
    s_bufs, p_bufs = scratch[:MLA_SLOTS], scratch[MLA_SLOTS:]
    cmax = None
    m = jnp.full((1, tq), NEG_BIG, F32)
    alpha = None
    acc = jnp.zeros((MLA_VT_ROWS, tq), F32)
    for t in range(n_chunks + 2):
        alpha_prev = alpha
        if 1 <= t <= n_chunks:
            e = (t - 1) % MLA_SLOTS
            m_new = jnp.maximum(m, jnp.max(cmax, axis=0, keepdims=True))
            alpha = jnp.exp2(m - m_new)
            m = m_new
            p_bufs[e][...] = jnp.exp2(s_bufs[e][...] - m).astype(BF16)
        if t < n_chunks:
            src, lo = chunks[t]
            s_t = _dot(kv_refs[2 * src][0, lo:lo + tk, :], q_t)
            s_bufs[t % MLA_SLOTS][...] = s_t
            cmax = jnp.max(s_t.reshape(tk // 8, 8, tq), axis=0)
        if t >= 2:
            src, lo = chunks[t - 2]
            vt = kv_refs[2 * src + 1][0, :, lo:lo + tk]
            acc = alpha_prev * acc + _dot(vt, p_bufs[(t - 2) % MLA_SLOTS][...])
    o_ref[0] = (acc[0:MLA_V] / acc[MLA_V:MLA_V + 1]).astype(o_ref.dtype)


def _mla(qt, kv, tq, tk):
    bsz, _, t = qt.shape
    assert t % tq == 0 and all(k.shape[1] % tk == 0 for k, _ in kv)
    chunks = [(src, lo) for src, (k, _) in enumerate(kv) for lo in range(0, k.shape[1], tk)]
    in_specs = [pl.BlockSpec((1, MLA_HEAD_PAD, tq), lambda b, h, i: (b, h, i))]
    for k, _ in kv:
        in_specs += [pl.BlockSpec((1, k.shape[1], MLA_HEAD_PAD), lambda b, h, i: (b, 0, h)),
                     pl.BlockSpec((1, MLA_VT_ROWS, k.shape[1]), lambda b, h, i: (b, h, 0))]
    return pl.pallas_call(
        functools.partial(_mla_kernel, chunks=chunks, tk=tk),
        grid=(bsz, MLA_HEADS, t // tq),
        in_specs=in_specs,
        out_specs=pl.BlockSpec((1, MLA_V, tq), lambda b, h, i: (b, h, i)),
        out_shape=jax.ShapeDtypeStruct((bsz, MLA_V_W, t), BF16),
        scratch_shapes=[pltpu.VMEM((tk, tq), F32)] * MLA_SLOTS + [pltpu.VMEM((tk, tq), BF16)] * MLA_SLOTS,
        compiler_params=_cparams("arbitrary", "arbitrary", "arbitrary"), name="mla_attn",
    )(qt, *[a for pair in kv for a in pair])


def _post_tail(x, y, gate, lng, lnb):
    return _ln(DEEPNORM_ALPHA * x + gate * y) * lng + lnb


def _even_post_tile(x_ref, of_ref, ob_ref, sgg_ref, ot_ref, smg_ref, gate_ref, wo_ref, gng_ref, lng_ref, lnb_ref):
    o = of_ref[0].astype(F32) + ob_ref[0].astype(F32)
    parts = [_rms(o[:, h * GLA_DV:(h + 1) * GLA_DV]) * gng_ref[...] for h in range(GLA_HEADS)]
    yg = jnp.concatenate(parts, axis=1) * sgg_ref[0].astype(F32)
    ym = ot_ref[0].astype(F32).T * smg_ref[0].astype(F32)
    y = _dot(yg.astype(BF16), wo_ref[0:GLA_V_W, :]) + _dot(ym.astype(BF16), wo_ref[GLA_V_W:, :])
    return _post_tail(x_ref[0], y, gate_ref[0], lng_ref[...], lnb_ref[...])


def _even_post_kernel(*refs):
    refs[-1][0] = _even_post_tile(*refs[:-1])


def _even_post_odd_pre_kernel(*refs):
    post_in, pre_par, (x_out_ref, u_ref, sf_ref, ys_ref) = refs[:11], refs[11:17], refs[17:]
    x_new = _even_post_tile(*post_in)
    x_out_ref[0] = x_new
    _odd_pre_tile(x_new, *pre_par, u_ref, sf_ref, ys_ref)


def _even_post(x, o_f, o_b, sgg, o_t, smg, gate, wo, gng, lng, lnb, li, l, tm):
    bsz, t, d = x.shape
    tok = lambda w: pl.BlockSpec((1, tm, w), lambda b, i: (b, i, 0))
    return pl.pallas_call(
        _even_post_kernel, grid=(bsz, t // tm),
        in_specs=[tok(d), tok(GLA_V_W), tok(GLA_V_W), tok(GLA_V_W),
                  pl.BlockSpec((1, MLA_V_W, tm), lambda b, i: (b, 0, i)), tok(MLA_V_W),
                  gate[1],
                  _layer(wo, li), _layer(gng, li), _layer(lng, l), _layer(lnb, l)],
        out_specs=tok(d), out_shape=jax.ShapeDtypeStruct((bsz, t, d), F32),
        compiler_params=_cparams("arbitrary", "arbitrary"), name="even_post",
    )(x, o_f, o_b, sgg, o_t, smg, gate[0], wo, gng, lng, lnb)


def _odd_pre_kernel(x_ref, *refs):
    _odd_pre_tile(x_ref[0], *refs)


def _odd_pre_tile(x, shift_ref, scale_ref, w_ref, dft_ref, sw_ref, sb_ref, u_ref, sf_ref, ys_ref):
    h = _ln(x) * (1.0 + scale_ref[0]) + shift_ref[0]
    z = _dot(h.astype(BF16), w_ref[...])
    for g in range(FNET_GROUPS):
        cs = slice(g * FNET_GROUP_CH, (g + 1) * FNET_GROUP_CH)
        ab = _dot(z[:, cs].astype(BF16), dft_ref[...])
        u_ref[0, :, cs] = ab[:, 0:FNET_GROUP_CH].astype(BF16)
        u_ref[0, :, FNET_W + g * FNET_GROUP_CH:FNET_W + (g + 1) * FNET_GROUP_CH] = ab[:, FNET_GROUP_CH:].astype(BF16)
    sf_ref[0] = _silu(z[:, FNET_W:2 * FNET_W]).astype(BF16)
    o0 = 2 * FNET_W
    tm = z.shape[0]
    for g in range(SGU_GROUPS):
        cs = slice(g * SGU_GROUP_CH, (g + 1) * SGU_GROUP_CH)
        ug = _gelu(z[:, o0 + g * SGU_GROUP_CH:o0 + (g + 1) * SGU_GROUP_CH])
        vg = _ln(_gelu(z[:, o0 + SGU_W + g * SGU_GROUP_CH:o0 + SGU_W + (g + 1) * SGU_GROUP_CH])).astype(BF16)
        sg = _silu(z[:, o0 + 2 * SGU_W + g * SGU_GROUP_CH:o0 + 2 * SGU_W + (g + 1) * SGU_GROUP_CH])
        for c in range(tm // SGU_CHUNK):
            rs = slice(c * SGU_CHUNK, (c + 1) * SGU_CHUNK)
            sv = _dot(sw_ref[g], vg[rs]) + sb_ref[g]
            ys_ref[0, rs, cs] = (ug[rs] * sv * sg[rs]).astype(BF16)


def _even_post_odd_pre(x, o_f, o_b, sgg, o_t, smg, gate, wo, gng, lng, lnb, li, l,
                       shift, scale, w, dft, sw, sb, lj, tm):
    bsz, t, d = x.shape
    tok = lambda w_: pl.BlockSpec((1, tm, w_), lambda b, i: (b, i, 0))
    return pl.pallas_call(
        _even_post_odd_pre_kernel, grid=(bsz, t // tm),
        in_specs=[tok(d), tok(GLA_V_W), tok(GLA_V_W), tok(GLA_V_W),
                  pl.BlockSpec((1, MLA_V_W, tm), lambda b, i: (b, 0, i)), tok(MLA_V_W),
                  gate[1],
                  _layer(wo, li), _layer(gng, li), _layer(lng, l), _layer(lnb, l),
                  shift[1], scale[1], _layer(w, lj), _full(dft.shape), _layer(sw, lj), _layer(sb, lj)],
        out_specs=(tok(d), tok(2 * FNET_W), tok(FNET_W), tok(SGU_W)),
        out_shape=(jax.ShapeDtypeStruct((bsz, t, d), F32),
                   jax.ShapeDtypeStruct((bsz, t, 2 * FNET_W), BF16),
                   jax.ShapeDtypeStruct((bsz, t, FNET_W), BF16),
                   jax.ShapeDtypeStruct((bsz, t, SGU_W), BF16)),
        compiler_params=_cparams("arbitrary", "arbitrary"), name="even_post_odd_pre",
    )(x, o_f, o_b, sgg, o_t, smg, gate[0], wo, gng, lng, lnb, shift[0], scale[0], w, dft, sw, sb)


def _odd_pre(x, shift, scale, w, dft, sw, sb, li, tm):
    bsz, t, d = x.shape
    tok = lambda w_: pl.BlockSpec((1, tm, w_), lambda b, i: (b, i, 0))
    return pl.pallas_call(
        _odd_pre_kernel, grid=(bsz, t // tm),
        in_specs=[tok(d), shift[1], scale[1], _layer(w, li), _full(dft.shape), _layer(sw, li), _layer(sb, li)],
        out_specs=(tok(2 * FNET_W), tok(FNET_W), tok(SGU_W)),
        out_shape=(jax.ShapeDtypeStruct((bsz, t, 2 * FNET_W), BF16),
                   jax.ShapeDtypeStruct((bsz, t, FNET_W), BF16),
                   jax.ShapeDtypeStruct((bsz, t, SGU_W), BF16)),
        compiler_params=_cparams("arbitrary", "arbitrary"), name="odd_pre",
    )(x, shift[0], scale[0], w, dft, sw, sb)


FFT_T2_BLK = 16
FFT_P1_BLK = 8


def _fft1_kernel(u_ref, w1_ref, tc_ref, ts_ref, z_ref):
    n1, tb = FFT_N1, FFT_T2_BLK
    r = n1 * tb
    pq = _dot(w1_ref[...], u_ref[0].reshape(r, 2 * FNET_W))
    zr = pq[0:r, 0:FNET_W] - pq[r:, FNET_W:]
    zi = -pq[0:r, FNET_W:] - pq[r:, 0:FNET_W]
    tc = _tile_lanes(tc_ref[0], FNET_W // 128)
    ts = _tile_lanes(ts_ref[0], FNET_W // 128)
    z_ref[0, :, :, 0:FNET_W] = (zr * tc + zi * ts).astype(BF16).reshape(n1, tb, FNET_W)
    z_ref[0, :, :, FNET_W:] = (zi * tc - zr * ts).astype(BF16).reshape(n1, tb, FNET_W)


def _fft2_kernel(z_ref, c2_ref, s2_ref, y_ref):
    for j in range(FFT_P1_BLK):
        zp = z_ref[0, j]
        y_ref[0, :, j, :] = _dot(c2_ref[...], zp[:, 0:FNET_W]) + _dot(s2_ref[...], zp[:, FNET_W:])


def _dft_tables(t):
    n1, n2 = FFT_N1, t // FFT_N1
    p1 = np.arange(n1, dtype=np.float64)
    a1 = 2.0 * np.pi * np.outer(p1, p1) / n1
    eye = np.eye(FFT_T2_BLK)
    w1 = np.concatenate([np.kron(np.cos(a1), eye), np.kron(np.sin(a1), eye)], axis=0)
    at = 2.0 * np.pi * np.outer(p1, np.arange(n2, dtype=np.float64)) / t

    def twiddle(a):
        a = a.reshape(n1, n2 // FFT_T2_BLK, FFT_T2_BLK).transpose(1, 0, 2).reshape(n2 // FFT_T2_BLK, -1)
        return np.repeat(a[:, :, None], 128, axis=2)
    tc, ts = twiddle(np.cos(at)), twiddle(np.sin(at))
    p2 = np.arange(n2, dtype=np.float64)
    a2 = 2.0 * np.pi * np.outer(p2, p2) / n2
    norm = 1.0 / math.sqrt(t * FNET_GROUP_CH)
    return (jnp.asarray(w1, F32).astype(BF16), jnp.asarray(tc, F32), jnp.asarray(ts, F32),
            jnp.asarray(np.cos(a2) * norm, F32).astype(BF16), jnp.asarray(np.sin(a2) * norm, F32).astype(BF16))


def _channel_dft_matrix():
    d = np.arange(FNET_GROUP_CH, dtype=np.float64)
    a = 2.0 * np.pi * np.outer(d, d) / FNET_GROUP_CH
    return jnp.asarray(np.concatenate([np.cos(a), np.sin(a)], axis=1), F32).astype(BF16)


def _fnet_long(u):
    bsz, t, w2 = u.shape
    n1, n2 = FFT_N1, t // FFT_N1
    w1, tc, ts, c2, s2 = _dft_tables(t)
    blk = pl.BlockSpec((1, n1, FFT_T2_BLK, w2), lambda b, i: (b, 0, i, 0))
    tw = pl.BlockSpec((1, n1 * FFT_T2_BLK, 128), lambda b, i: (i, 0, 0))
    z = pl.pallas_call(
        _fft1_kernel, grid=(bsz, n2 // FFT_T2_BLK),
        in_specs=[blk, _full(w1.shape), tw, tw],
        out_specs=blk,
        out_shape=jax.ShapeDtypeStruct((bsz, n1, n2, w2), BF16),
        compiler_params=_cparams("arbitrary", "arbitrary"), name="fnet_stage1",
    )(u.reshape(bsz, n1, n2, w2), w1, tc, ts)
    y = pl.pallas_call(
        _fft2_kernel, grid=(bsz, n1 // FFT_P1_BLK),
        in_specs=[pl.BlockSpec((1, FFT_P1_BLK, n2, w2), lambda b, i: (b, i, 0, 0)),
                  _full(c2.shape), _full(s2.shape)],
        out_specs=pl.BlockSpec((1, n2, FFT_P1_BLK, FNET_W), lambda b, i: (b, 0, i, 0)),
        out_shape=jax.ShapeDtypeStruct((bsz, n2, n1, FNET_W), F32),
        compiler_params=_cparams("arbitrary", "arbitrary"), name="fnet_stage2",
    )(z, c2, s2)
    return y.reshape(bsz, t, FNET_W)


def _fnet_short_kernel(u_ref, c_ref, s_ref, y_ref):
    u = u_ref[0]
    y_ref[0] = _dot(c_ref[...], u[:, 0:FNET_W]) - _dot(s_ref[...], u[:, FNET_W:])


def _fnet_short(u):
    bsz, t, w2 = u.shape
    p = np.arange(t, dtype=np.float64)
    a = 2.0 * np.pi * np.outer(p, p) / t
    norm = 1.0 / math.sqrt(t * FNET_GROUP_CH)
    c, s = jnp.asarray(np.cos(a) * norm, F32).astype(BF16), jnp.asarray(np.sin(a) * norm, F32).astype(BF16)
    return pl.pallas_call(
        _fnet_short_kernel, grid=(bsz,),
        in_specs=[pl.BlockSpec((1, t, w2), lambda b: (b, 0, 0)), _full(c.shape), _full(s.shape)],
        out_specs=pl.BlockSpec((1, t, FNET_W), lambda b: (b, 0, 0)),
        out_shape=jax.ShapeDtypeStruct((bsz, t, FNET_W), F32),
        compiler_params=_cparams("arbitrary"), name="fnet_short",
    )(u, c, s)


def _odd_post_tile(x_ref, fr_ref, sf_ref, ys_ref, gate_ref, wo_ref, lng_ref, lnb_ref):
    yf = (fr_ref[0] * sf_ref[0].astype(F32)).astype(BF16)
    y = _dot(yf, wo_ref[0:FNET_W, :]) + _dot(ys_ref[0], wo_ref[FNET_W:, :])
    return _post_tail(x_ref[0], y, gate_ref[0], lng_ref[...], lnb_ref[...])


def _odd_post_kernel(*refs):
    refs[-1][0] = _odd_post_tile(*refs[:-1])


def _odd_post_even_pre_kernel(*refs):
    post_in, pre_par, x_out_ref, pre_out = refs[:8], refs[8:23], refs[23], refs[24:]
    x_new = _odd_post_tile(*post_in)
    x_out_ref[0] = x_new
    _even_pre_tile(x_new, *pre_par, *pre_out)


def _odd_post_even_pre(x, fr, sf, ys, gate, wo, lng, lnb, li, l, shift, scale, wts, lj, tabs, tm):
    bsz, t, d = x.shape
    hq = MLA_HEADS * MLA_HEAD_PAD
    tok = lambda w: pl.BlockSpec((1, tm, w), lambda b, i: (b, i, 0))
    tab = pl.BlockSpec((tm, MLA_HEAD_PAD), lambda b, i: (i, 0))
    pre_shapes, pre_specs = _even_pre_outputs(bsz, t, tm)
    place = _rotary_key_placement()
    return pl.pallas_call(
        _odd_post_even_pre_kernel, grid=(bsz, t // tm),
        in_specs=[tok(d), tok(FNET_W), tok(FNET_W), tok(SGU_W), gate[1],
                  _layer(wo, li), _layer(lng, l), _layer(lnb, l),
                  shift[1], scale[1]] + [_layer(w, lj) for w in wts] + [tab] * 2 + [_full(place.shape)],
        out_specs=(tok(d),) + pre_specs,
        out_shape=(jax.ShapeDtypeStruct((bsz, t, d), F32),) + pre_shapes,
        compiler_params=_cparams("arbitrary", "arbitrary"), name="odd_post_even_pre",
    )(x, fr, sf, ys, gate[0], wo, lng, lnb, shift[0], scale[0], *wts, *tabs, place)


def _odd_post(x, fr, sf, ys, gate, wo, lng, lnb, li, l, tm):
    bsz, t, d = x.shape
    tok = lambda w: pl.BlockSpec((1, tm, w), lambda b, i: (b, i, 0))
    return pl.pallas_call(
        _odd_post_kernel, grid=(bsz, t // tm),
        in_specs=[tok(d), tok(FNET_W), tok(FNET_W), tok(SGU_W),
                  gate[1],
                  _layer(wo, li), _layer(lng, l), _layer(lnb, l)],
        out_specs=tok(d), out_shape=jax.ShapeDtypeStruct((bsz, t, d), F32),
        compiler_params=_cparams("arbitrary", "arbitrary"), name="odd_post",
    )(x, fr, sf, ys, gate[0], wo, lng, lnb)


def _even_weights(w_in, gla_w2, gla_b, q_norm_g, w_uq, kv_norm_g, w_ukv):
    d = w_in.shape[0]
    idx = np.cumsum(EVEN_IN_SIZES)[:-1].tolist()
    gq, gk, gv, glr, gg, cq, ckv, kr, mg = jnp.split(w_in, idx, axis=1)
    half = MLA_ROPE // 2
    z = lambda n: jnp.zeros((d, n), w_in.dtype)
    small = jnp.concatenate([glr, kr, kr[:, half:], kr[:, :half], z(128 - 2 * GLA_GATE_RANK - 2 * MLA_ROPE)], axis=1)
    w = jnp.concatenate([gq * GLA_DK ** -0.5, gk, gv, gg, cq, ckv, mg, small], axis=1).astype(BF16)
    zw = jnp.zeros((GLA_GATE_RANK, GLA_QK_W), F32)
    w2 = jnp.concatenate([jnp.concatenate([gla_w2[0], zw], axis=1), jnp.concatenate([zw, gla_w2[1]], axis=1),
                          jnp.zeros((128 - 2 * GLA_GATE_RANK, 2 * GLA_QK_W), F32)], axis=0)
    gb = jnp.concatenate([gla_b[0], gla_b[1]])[None, :]
    uq = w_uq.reshape(MLA_Q_RANK, MLA_HEADS, MLA_NOPE + MLA_ROPE)
    pad = MLA_HEAD_PAD - MLA_NOPE - MLA_ROPE
    zq = lambda n: jnp.zeros((MLA_Q_RANK, MLA_HEADS, n), w_uq.dtype)
    wqa = jnp.concatenate([uq, zq(pad)], axis=2).reshape(MLA_Q_RANK, -1)
    wqb = jnp.concatenate([zq(MLA_NOPE), uq[:, :, MLA_NOPE + half:], uq[:, :, MLA_NOPE:MLA_NOPE + half], zq(pad)],
                          axis=2).reshape(MLA_Q_RANK, -1)
    ukv = w_ukv.reshape(MLA_KV_RANK, MLA_HEADS, MLA_NOPE + MLA_V)
    wk = jnp.concatenate([ukv[:, :, :MLA_NOPE], jnp.zeros((MLA_KV_RANK, MLA_HEADS, MLA_HEAD_PAD - MLA_NOPE), w_ukv.dtype)],
                         axis=2).reshape(MLA_KV_RANK, -1)
    vpad = MLA_VT_ROWS - MLA_V
    wv = jnp.concatenate([ukv[:, :, MLA_NOPE:], jnp.zeros((MLA_KV_RANK, MLA_HEADS, vpad), w_ukv.dtype)],
                         axis=2).reshape(MLA_KV_RANK, -1)
    vbias = np.zeros((MLA_HEADS, MLA_VT_ROWS), np.float32)
    vbias[:, MLA_V] = 1.0
    return (w, w2.astype(BF16), gb, q_norm_g[None, :], wqa.astype(BF16), wqb.astype(BF16),
            kv_norm_g[None, :], wk.astype(BF16), wv.astype(BF16), jnp.asarray(vbias.reshape(1, -1)))


def _rope_tables(n):
    row = jnp.repeat(jnp.arange(n // GRID_W, dtype=F32), GRID_W)
    col = (jnp.arange(n) % GRID_W).astype(F32)
    n_freq = MLA_ROPE // 4
    inv = ROPE_BASE ** (-jnp.arange(n_freq, dtype=F32) / n_freq)
    ang = jnp.concatenate([row[:, None] * inv, col[:, None] * inv], -1)
    cos, sin = jnp.cos(ang), jnp.sin(ang)
    pad = jnp.zeros((n, MLA_HEAD_PAD - MLA_NOPE - MLA_ROPE), F32)
    zn = jnp.zeros((n, MLA_NOPE), F32)
    ck = jnp.concatenate([zn, cos, cos, pad], axis=1)
    sk = jnp.concatenate([zn, -sin, sin, pad], axis=1)
    return ck, sk


def _plain_tables(n):
    pad = jnp.zeros((n, MLA_HEAD_PAD - MLA_NOPE - MLA_ROPE), F32)
    zn = jnp.zeros((n, MLA_NOPE), F32)
    ck = jnp.concatenate([zn, jnp.ones((n, MLA_ROPE), F32), pad], axis=1)
    return ck, jnp.zeros((n, MLA_HEAD_PAD), F32)


def _pick(t, pref):
    return pref if t % pref == 0 else t


def kernel(x, c, ctx, c_ctx, ada_w, ada_b, post_ln_g, post_ln_b, even_w_in, gla_w2, gla_b, gla_norm_g,
           mla_q_norm_g, mla_w_uq, mla_kv_norm_g, mla_w_ukv, even_w_out, odd_w_in, sgu_w, sgu_b, odd_w_out):
    bsz, n, d = x.shape
    lc = ctx.shape[1]
    depth = ada_w.shape[0]
    assert bsz + 1 <= 8 and n % (FFT_N1 * FFT_T2_BLK) == 0 and n % 512 == 0 and lc % MLA_TK == 0

    cond = jnp.concatenate([c, c_ctx[None, :], jnp.zeros((8 - bsz - 1, d), F32)], axis=0)
    mods = _mods(cond, ada_w, ada_b)

    mods4 = mods.reshape(depth, 8, 1, 3 * d)

    def lat_mod(l, j):
        return mods4, pl.BlockSpec((None, 1, 1, d), lambda b, *_: (l, b, 0, j))

    def ctx_mod(l, j):
        return mods4, pl.BlockSpec((None, 1, 1, d), lambda b, *_: (l, bsz, 0, j))

    rope_tabs = _rope_tables(n)
    ctx_tabs = _plain_tables(lc)
    dft_c = _channel_dft_matrix()
    tm_lat, tm_ctx = _pick(n, 512), _pick(lc, 256)
    tm_post = _pick(n, 1024)
    zero_state = jnp.zeros((bsz, 2, GLA_QK_W, GLA_DV), F32)

    e_wts = jax.vmap(_even_weights)(even_w_in, gla_w2, gla_b, mla_q_norm_g, mla_w_uq, mla_kv_norm_g, mla_w_ukv)
    e_wo = even_w_out.astype(BF16)
    e_gng = gla_norm_g[:, None, :]
    o_w = odd_w_in.astype(BF16)
    o_wo = odd_w_out.astype(BF16)
    o_sw = sgu_w.astype(BF16)
    o_sb = jnp.broadcast_to(sgu_b[:, :, :, None], sgu_b.shape + (SGU_GROUP_CH,))
    lng, lnb = post_ln_g[:, None, :], post_ln_b[:, None, :]

    odd_in = even_in = None
    for l in range(depth):
        need_ctx_out = any(j % 2 == 0 for j in range(l + 1, depth))
        i = l // 2
        if l % 2 == 0:
            pc = _even_pre(ctx, ctx_mod(l, 0), ctx_mod(l, 1), e_wts, i, ctx_tabs, tm_ctx)
            q_c, k_c, v_c, g_c, sgg_c, smg_c, qt_c, kk_c, vt_c = pc
            if even_in is None:
                even_in = _even_pre(x, lat_mod(l, 0), lat_mod(l, 1), e_wts, i, rope_tabs, tm_lat)
            q_l, k_l, v_l, g_l, sgg_l, smg_l, qt_l, kk_l, vt_l = even_in
            even_in = None
            of_c, ob_c, s_c = _gla(q_c, k_c, v_c, g_c, zero_state, GLA_SUB)
            of_l, ob_l, _ = _gla(q_l, k_l, v_l, g_l, s_c, 2 * GLA_SUB)
            ot_l = _mla(qt_l, [(kk_l, vt_l), (kk_c, vt_c)], _pick(n, 512), MLA_TK)
            if l + 1 < depth:
                x_new, *odd_in = _even_post_odd_pre(
                    x, of_l, ob_l, sgg_l, ot_l, smg_l, lat_mod(l, 2), e_wo, e_gng, lng, lnb, i, l,
                    lat_mod(l + 1, 0), lat_mod(l + 1, 1), o_w, dft_c, o_sw, o_sb, (l + 1) // 2, tm_lat)
            else:
                x_new = _even_post(x, of_l, ob_l, sgg_l, ot_l, smg_l, lat_mod(l, 2), e_wo, e_gng, lng, lnb,
                                   i, l, tm_post)
            if need_ctx_out:
                ot_c = _mla(qt_c, [(kk_c, vt_c)], lc, MLA_TK)
                ctx = _even_post(ctx, of_c, ob_c, sgg_c, ot_c, smg_c, ctx_mod(l, 2), e_wo, e_gng, lng, lnb, i, l, tm_ctx)
            x = x_new
        else:
            if odd_in is None:
                odd_in = _odd_pre(x, lat_mod(l, 0), lat_mod(l, 1), o_w, dft_c, o_sw, o_sb, i, tm_lat)
            u, sf, ys = odd_in
            odd_in = None
            if l + 1 < depth:
                x_new, *even_in = _odd_post_even_pre(
                    x, _fnet_long(u), sf, ys, lat_mod(l, 2), o_wo, lng, lnb, i, l,
                    lat_mod(l + 1, 0), lat_mod(l + 1, 1), e_wts, (l + 1) // 2, rope_tabs, tm_lat)
            else:
                x_new = _odd_post(x, _fnet_long(u), sf, ys, lat_mod(l, 2), o_wo, lng, lnb, i, l, tm_post)
            if need_ctx_out:
                u, sf, ys = _odd_pre(ctx, ctx_mod(l, 0), ctx_mod(l, 1), o_w, dft_c, o_sw, o_sb, i, tm_ctx)
                ctx = _odd_post(ctx, _fnet_short(u), sf, ys, ctx_mod(l, 2), o_wo, lng, lnb, i, l, tm_ctx)
            x = x_new
    return x
```

```python
import functools
import math

import numpy as np
import jax
import jax.numpy as jnp
from jax import lax
from jax.experimental import pallas as pl
from jax.experimental.pallas import tpu as pltpu

F32 = jnp.float32
BF16 = jnp.bfloat16

DEPTH = 4
GRID_W = 64
DEEPNORM_ALPHA = (2 * DEPTH) ** 0.25
LN_EPS = 1e-6

GLA_HEADS = 4
GLA_DK = 64
GLA_DV = 128
GLA_QK_W = GLA_HEADS * GLA_DK
GLA_V_W = GLA_HEADS * GLA_DV
GLA_GATE_RANK = 16
GLA_TAU = 16.0
GLA_CHUNK = 64

MLA_HEADS = 8
MLA_NOPE = 64
MLA_ROPE = 32
MLA_V = 64
MLA_Q_RANK = 256
MLA_KV_RANK = 128
MLA_V_W = MLA_HEADS * MLA_V
MLA_SCALE = (MLA_NOPE + MLA_ROPE) ** -0.5
ROPE_BASE = 10000.0
MLA_HEAD_PAD = 128
MLA_VT_ROWS = MLA_V + 16
MLA_VT_W = MLA_HEADS * MLA_VT_ROWS

FNET_GROUPS = 4
FNET_GROUP_CH = 128
FNET_W = FNET_GROUPS * FNET_GROUP_CH
FFT_N1 = 64

SGU_GROUPS = 4
SGU_GROUP_CH = 128
SGU_W = SGU_GROUPS * SGU_GROUP_CH
SGU_CHUNK = 128

EVEN_IN_SIZES = (GLA_QK_W, GLA_QK_W, GLA_V_W, 2 * GLA_GATE_RANK, GLA_V_W,
                 MLA_Q_RANK, MLA_KV_RANK, MLA_ROPE, MLA_V_W)

E_GQ, E_GK, E_GV, E_GG, E_CQ, E_CKV, E_MG, E_SMALL, E_END = (
    0, 256, 512, 1024, 1536, 1792, 1920, 2432, 2560)
SM_KR = 2 * GLA_GATE_RANK
SM_KR_SWAPPED = MLA_NOPE

VMEM_LIMIT_BYTES = 56 * 1024 * 1024
LOG2E = math.log2(math.e)
NEG_BIG = -1e30


def _cparams(*sem):
    return pltpu.CompilerParams(dimension_semantics=sem, vmem_limit_bytes=VMEM_LIMIT_BYTES)


def _dot(a, b):
    return jnp.dot(a, b, preferred_element_type=F32)


def _dot_nt(a, b):
    return lax.dot_general(a, b, (((1,), (1,)), ((), ())), preferred_element_type=F32)


def _dot_tn(a, b):
    return lax.dot_general(a, b, (((0,), (0,)), ((), ())), preferred_element_type=F32)


def _ln(x):
    xc = x - jnp.mean(x, -1, keepdims=True)
    return xc * lax.rsqrt(jnp.mean(xc * xc, -1, keepdims=True) + LN_EPS)


def _rms(x):
    return x * lax.rsqrt(jnp.mean(x * x, -1, keepdims=True) + LN_EPS)


def _silu(x):
    return x / (1.0 + jnp.exp(-x))


def _gelu(x):
    return 0.5 * x * (1.0 + lax.erf(x * (2.0 ** -0.5)))


def _tile_lanes(x, reps):
    return jnp.concatenate([x] * reps, axis=1)


def _full(shape):
    n = len(shape)
    return pl.BlockSpec(shape, lambda *_: (0,) * n)


def _layer(arr, i):
    n = arr.ndim
    return pl.BlockSpec((None,) + arr.shape[1:], lambda *_: (i,) + (0,) * (n - 1))


def _mods_kernel(cond_ref, w_ref, b_ref, o_ref):
    s = _silu(cond_ref[...])
    o_ref[0] = _dot(s.astype(BF16), w_ref[0].astype(BF16)) + b_ref[0]


def _mods(cond, ada_w, ada_b):
    depth, d, d3 = ada_w.shape
    return pl.pallas_call(
        _mods_kernel,
        grid=(depth,),
        in_specs=[pl.BlockSpec((8, d), lambda l: (0, 0)),
                  pl.BlockSpec((1, d, d3), lambda l: (l, 0, 0)),
                  pl.BlockSpec((1, 1, d3), lambda l: (l, 0, 0))],
        out_specs=pl.BlockSpec((1, 8, d3), lambda l: (l, 0, 0)),
        out_shape=jax.ShapeDtypeStruct((depth, 8, d3), F32),
        compiler_params=_cparams("arbitrary"),
        name="ada_mod",
    )(cond, ada_w, ada_b.reshape(depth, 1, d3))


def _even_pre_kernel(x_ref, *refs):
    _even_pre_tile(x_ref[0], *refs)


def _even_pre_tile(x, shift_ref, scale_ref, w_ref, w2_ref, gb_ref, qg_ref, wqa_ref, wqb_ref,
                   kvg_ref, wk_ref, wv_ref, vb_ref, ck_ref, sk_ref, place_ref,
                   q_ref, k_ref, v_ref, g_ref, sgg_ref, smg_ref, qt_ref, kk_ref, vt_ref):
    h = _ln(x) * (1.0 + scale_ref[0]) + shift_ref[0]
    z = _dot(h.astype(BF16), w_ref[...])
    q_ref[0] = z[:, E_GQ:E_GK].astype(BF16)
    k_ref[0] = z[:, E_GK:E_GV].astype(BF16)
    v_ref[0] = z[:, E_GV:E_GG].astype(BF16)
    sgg_ref[0] = _silu(z[:, E_GG:E_CQ]).astype(BF16)
    smg_ref[0] = _silu(z[:, E_MG:E_SMALL]).astype(BF16)
    small = z[:, E_SMALL:E_END]
    small16 = small.astype(BF16)
    pre = _dot(small16, w2_ref[...]) + gb_ref[...]
    g_ref[0] = jax.nn.log_sigmoid(pre) * (1.0 / GLA_TAU)
    cqn = (_rms(z[:, E_CQ:E_CKV]) * qg_ref[...]).astype(BF16)
    lane = lax.broadcasted_iota(jnp.int32, (1, MLA_HEAD_PAD), 1)
    q_scale = MLA_SCALE * LOG2E
    cq = ck_ref[...] * q_scale + jnp.where(lane < MLA_NOPE, q_scale, 0.0)
    sq = sk_ref[...] * q_scale
    qfull = (_dot(cqn, wqa_ref[...]) * _tile_lanes(cq, MLA_HEADS)
             + _dot(cqn, wqb_ref[...]) * _tile_lanes(sq, MLA_HEADS))
    qt_ref[0] = qfull.T.astype(BF16)
    ckvn = (_rms(z[:, E_CKV:E_MG]) * kvg_ref[...]).astype(BF16)
    kr = _dot(small16, place_ref[...]) * ck_ref[...] + small * sk_ref[...]
    kk_ref[0] = (_dot(ckvn, wk_ref[...]) + _tile_lanes(kr, MLA_HEADS)).astype(BF16)
    vt_ref[0] = (_dot(ckvn, wv_ref[...]) + vb_ref[...]).T.astype(BF16)


def _even_pre_outputs(bsz, t, tm):
    hq = MLA_HEADS * MLA_HEAD_PAD
    tok = lambda w: pl.BlockSpec((1, tm, w), lambda b, i: (b, i, 0))
    out_shape = (
        jax.ShapeDtypeStruct((bsz, t, GLA_QK_W), BF16),
        jax.ShapeDtypeStruct((bsz, t, GLA_QK_W), BF16),
        jax.ShapeDtypeStruct((bsz, t, GLA_V_W), BF16),
        jax.ShapeDtypeStruct((bsz, t, 2 * GLA_QK_W), F32),
        jax.ShapeDtypeStruct((bsz, t, GLA_V_W), BF16),
        jax.ShapeDtypeStruct((bsz, t, MLA_V_W), BF16),
        jax.ShapeDtypeStruct((bsz, hq, t), BF16),
        jax.ShapeDtypeStruct((bsz, t, hq), BF16),
        jax.ShapeDtypeStruct((bsz, MLA_VT_W, t), BF16),
    )
    out_specs = (tok(GLA_QK_W), tok(GLA_QK_W), tok(GLA_V_W), tok(2 * GLA_QK_W), tok(GLA_V_W), tok(MLA_V_W),
                 pl.BlockSpec((1, hq, tm), lambda b, i: (b, 0, i)),
                 tok(hq),
                 pl.BlockSpec((1, MLA_VT_W, tm), lambda b, i: (b, 0, i)))
    return out_shape, out_specs


def _even_pre(x, shift, scale, wts, li, tabs, tm):
    bsz, t, d = x.shape
    tok = lambda w: pl.BlockSpec((1, tm, w), lambda b, i: (b, i, 0))
    tab = pl.BlockSpec((tm, MLA_HEAD_PAD), lambda b, i: (i, 0))
    out_shape, out_specs = _even_pre_outputs(bsz, t, tm)
    place = _rotary_key_placement()
    in_specs = [tok(d), shift[1], scale[1]] + [_layer(w, li) for w in wts] + [tab] * 2 + [_full(place.shape)]
    return pl.pallas_call(
        _even_pre_kernel, grid=(bsz, t // tm), in_specs=in_specs, out_specs=out_specs, out_shape=out_shape,
        compiler_params=_cparams("arbitrary", "arbitrary"), name="even_pre",
    )(x, shift[0], scale[0], *wts, *tabs, place)


def _rotary_key_placement():
    m = np.zeros((128, MLA_HEAD_PAD), np.float32)
    m[SM_KR + np.arange(MLA_ROPE), MLA_NOPE + np.arange(MLA_ROPE)] = 1.0
    return jnp.asarray(m, BF16)


def _gla_block(q, k, v, g, tmat, same_tri, mid_off, end_off, head_masks, chunk_masks):
    r = q.shape[0]
    c = GLA_CHUNK
    nc = r // c
    g_hi = g.astype(BF16)
    g_lo = (g - g_hi.astype(F32)).astype(BF16)
    b = _dot(tmat, g_hi) + _dot(tmat, g_lo)
    mid_rows = [b[j * c + mid_off:j * c + mid_off + 1] for j in range(nc)]
    end_rows = [b[j * c + end_off:j * c + end_off + 1] for j in range(nc)]
    per_chunk = lambda rows: jnp.concatenate([jnp.broadcast_to(x, (c, GLA_QK_W)) for x in rows], axis=0)
    b_mid = per_chunk(mid_rows)
    e_up = jnp.exp(b - b_mid)
    e_dn = jnp.exp(b_mid - b)
    qe = q * e_up
    ke = (k * e_dn).astype(BF16)
    kd_t = (k * (e_dn * per_chunk([jnp.exp(e - m) for e, m in zip(end_rows, mid_rows)]))).T
    qb = q * (e_up * per_chunk([jnp.exp(m) for m in mid_rows]))
    stack = lambda a: jnp.concatenate([jnp.where(m, a, 0.0) for m in head_masks], axis=0).astype(BF16)
    att = _dot_nt(stack(qe), ke)
    qb_st = stack(qb)
    dec_t = jnp.exp(jnp.concatenate(end_rows + [jnp.zeros((8 - nc, GLA_QK_W), F32)], axis=0)).T
    intra, ds = [], []
    for h in range(GLA_HEADS):
        a_h = jnp.where(same_tri, att[h * r:(h + 1) * r], 0.0).astype(BF16)
        v_h = v[:, h * GLA_DV:(h + 1) * GLA_DV]
        intra.append(_dot(a_h, v_h))
        kd_h = kd_t[h * GLA_DK:(h + 1) * GLA_DK]
        lhs = jnp.concatenate([jnp.where(cm, kd_h, 0.0) for cm in chunk_masks], axis=0).astype(BF16)
        ds.append(_dot(lhs, v_h))
    return intra, ds, qb_st, dec_t


def _gla_recur(par, s, o_ref, row0, reverse):
    intra, ds, qb_st, dec_t = par
    c = GLA_CHUNK
    r = intra[0].shape[0]
    nc = r // c
    for j in (reversed(range(nc)) if reverse else range(nc)):
        rows = slice(j * c, (j + 1) * c)
        st = jnp.concatenate([qb_st[h * r + j * c:h * r + (j + 1) * c] for h in range(GLA_HEADS)], axis=0)
        o_inter = _dot(st, s.astype(BF16))
        for h in range(GLA_HEADS):
            o_ref[0, row0 + j * c:row0 + (j + 1) * c, h * GLA_DV:(h + 1) * GLA_DV] = (
                intra[h][rows] + o_inter[h * c:(h + 1) * c]).astype(o_ref.dtype)
        dec = jnp.broadcast_to(dec_t[:, j:j + 1], (GLA_QK_W, GLA_DV))
        s = dec * s + jnp.concatenate([ds[h][rows] for h in range(GLA_HEADS)], axis=0)
    return s


GLA_SUB = 256


def _gla_kernel(qf_ref, kf_ref, vf_ref, gf_ref, qb_ref, kb_ref, vb_ref, gb_ref, s0_ref, tl_ref, tu_ref,
                of_ref, ob_ref, sfin_ref, s_scr, *, tb):
    i = pl.program_id(1)

    @pl.when(i == 0)
    def _():
        s_scr[...] = s0_ref[0]

    c, r = GLA_CHUNK, GLA_SUB
    row = lax.broadcasted_iota(jnp.int32, (r, r), 0)
    col = lax.broadcasted_iota(jnp.int32, (r, r), 1)
    same = (row // c) == (col // c)
    lane_head = lax.broadcasted_iota(jnp.int32, (r, GLA_QK_W), 1) // GLA_DK
    head_masks = [lane_head == h for h in range(GLA_HEADS)]
    lane_chunk = lax.broadcasted_iota(jnp.int32, (GLA_DK, r), 1) // c
    chunk_masks = [lane_chunk == j for j in range(r // c)]

    par_f, par_b = [], []
    for u in range(tb // r):
        rs = slice(u * r, (u + 1) * r)
        par_f.append(_gla_block(qf_ref[0, rs].astype(F32), kf_ref[0, rs].astype(F32), vf_ref[0, rs], gf_ref[0, rs],
                                tl_ref[...], same & (col <= row), c // 2 - 1, c - 1, head_masks, chunk_masks))
        par_b.append(_gla_block(qb_ref[0, rs].astype(F32), kb_ref[0, rs].astype(F32), vb_ref[0, rs], gb_ref[0, rs],
                                tu_ref[...], same & (col >= row), c // 2, 0, head_masks, chunk_masks))
    s = s_scr[0]
    for u in range(tb // r):
        s = _gla_recur(par_f[u], s, of_ref, u * r, False)
    s_scr[0] = s
    s = s_scr[1]
    for u in reversed(range(tb // r)):
        s = _gla_recur(par_b[u], s, ob_ref, u * r, True)
    s_scr[1] = s

    @pl.when(i == pl.num_programs(1) - 1)
    def _():
        sfin_ref[0] = s_scr[...]


def _gla(q, k, v, g, s0, tb):
    bsz, t, _ = q.shape
    nblk = t // tb
    assert tb % GLA_SUB == 0
    idx = np.arange(GLA_SUB)
    same = (idx[:, None] // GLA_CHUNK) == (idx[None, :] // GLA_CHUNK)
    tl = jnp.asarray(same & (idx[None, :] <= idx[:, None]), BF16)
    tu = jnp.asarray(same & (idx[None, :] >= idx[:, None]), BF16)
    fwd = lambda w: pl.BlockSpec((1, tb, w), lambda b, i: (b, i, 0))
    bwd = lambda w: pl.BlockSpec((1, tb, w), lambda b, i: (b, nblk - 1 - i, 0))
    st = pl.BlockSpec((1, 2, GLA_QK_W, GLA_DV), lambda b, i: (b, 0, 0, 0))
    return pl.pallas_call(
        functools.partial(_gla_kernel, tb=tb),
        grid=(bsz, nblk),
        in_specs=[fwd(GLA_QK_W), fwd(GLA_QK_W), fwd(GLA_V_W),
                  pl.BlockSpec((1, tb, GLA_QK_W), lambda b, i: (b, i, 0)),
                  bwd(GLA_QK_W), bwd(GLA_QK_W), bwd(GLA_V_W),
                  pl.BlockSpec((1, tb, GLA_QK_W), lambda b, i: (b, nblk - 1 - i, 1)),
                  st, _full((GLA_SUB, GLA_SUB)), _full((GLA_SUB, GLA_SUB))],
        out_specs=(fwd(GLA_V_W), bwd(GLA_V_W), st),
        out_shape=(jax.ShapeDtypeStruct((bsz, t, GLA_V_W), BF16),
                   jax.ShapeDtypeStruct((bsz, t, GLA_V_W), BF16),
                   jax.ShapeDtypeStruct((bsz, 2, GLA_QK_W, GLA_DV), F32)),
        scratch_shapes=[pltpu.VMEM((2, GLA_QK_W, GLA_DV), F32)],
        compiler_params=_cparams("arbitrary", "arbitrary"), name="gla_scan",
    )(q, k, v, g, q, k, v, g, s0, tl, tu)


MLA_SLOTS = 3
MLA_TK = 256


def _mla_kernel(qt_ref, *refs, chunks, tk, n_sub):
    n_src = max(src for src, _ in chunks) + 1
    kv_refs, o_ref, scratch = refs[:2 * n_src], refs[2 * n_src], refs[2 * n_src + 1:]
    tq = qt_ref.shape[2] // n_sub
    for sub in range(n_sub):
        cols = slice(sub * tq, (sub + 1) * tq)
        _mla_pipeline(qt_ref[0, :, cols], kv_refs, o_ref, cols,
                      scratch[2 * MLA_SLOTS * sub:2 * MLA_SLOTS * (sub + 1)], chunks, tk)


def _mla_pipeline(q_t, kv_refs, o_ref, cols, scratch, chunks, tk):
    n_chunks = len(chunks)
    tq = q_t.shape[1]
    s_bufs, p_bufs = scratch[:MLA_SLOTS], scratch[MLA_SLOTS:]
    cmax = None
    m = jnp.full((1, tq), NEG_BIG, F32)
    alpha = None
    acc = jnp.zeros((MLA_VT_ROWS, tq), F32)
    for t in range(n_chunks + 2):
        alpha_prev = alpha
        if 1 <= t <= n_chunks:
            e = (t - 1) % MLA_SLOTS
            m_new = jnp.maximum(m, jnp.max(cmax, axis=0, keepdims=True))
            alpha = jnp.exp2(m - m_new)
            m = m_new
            p_bufs[e][...] = jnp.exp2(s_bufs[e][...] - m).astype(BF16)
        if t < n_chunks:
            src, lo = chunks[t]
            s_t = _dot(kv_refs[2 * src][0, lo:lo + tk, :], q_t)
            s_bufs[t % MLA_SLOTS][...] = s_t
            cmax = jnp.max(s_t.reshape(tk // 8, 8, tq), axis=0)
        if t >= 2:
            src, lo = chunks[t - 2]
            vt = kv_refs[2 * src + 1][0, :, lo:lo + tk]
            acc = alpha_prev * acc + _dot(vt, p_bufs[(t - 2) % MLA_SLOTS][...])
    o_ref[0, :, cols] = (acc[0:MLA_V] / acc[MLA_V:MLA_V + 1]).astype(o_ref.dtype)


def _mla(qt, kv, tq, tk, n_sub=1):
    bsz, _, t = qt.shape
    tq_step = tq * n_sub
    assert t % tq_step == 0 and all(k.shape[1] % tk == 0 for k, _ in kv)
    chunks = [(src, lo) for src, (k, _) in enumerate(kv) for lo in range(0, k.shape[1], tk)]
    in_specs = [pl.BlockSpec((1, MLA_HEAD_PAD, tq_step), lambda b, h, i: (b, h, i))]
    for k, _ in kv:
        in_specs += [pl.BlockSpec((1, k.shape[1], MLA_HEAD_PAD), lambda b, h, i: (b, 0, h)),
                     pl.BlockSpec((1, MLA_VT_ROWS, k.shape[1]), lambda b, h, i: (b, h, 0))]
    return pl.pallas_call(
        functools.partial(_mla_kernel, chunks=chunks, tk=tk, n_sub=n_sub),
        grid=(bsz, MLA_HEADS, t // tq_step),
        in_specs=in_specs,
        out_specs=pl.BlockSpec((1, MLA_V, tq_step), lambda b, h, i: (b, h, i)),
        out_shape=jax.ShapeDtypeStruct((bsz, MLA_V_W, t), BF16),
        scratch_shapes=([pltpu.VMEM((tk, tq), F32)] * MLA_SLOTS + [pltpu.VMEM((tk, tq), BF16)] * MLA_SLOTS) * n_sub,
        compiler_params=_cparams("arbitrary", "arbitrary", "arbitrary"), name="mla_attn",
    )(qt, *[a for pair in kv for a in pair])


def _post_tail(x, y, gate, lng, lnb):
    return _ln(DEEPNORM_ALPHA * x + gate * y) * lng + lnb


def _even_post_tile(x_ref, of_ref, ob_ref, sgg_ref, ot_ref, smg_ref, gate_ref, wo_ref, gng_ref, lng_ref, lnb_ref):
    o = of_ref[0].astype(F32) + ob_ref[0].astype(F32)
    parts = [_rms(o[:, h * GLA_DV:(h + 1) * GLA_DV]) * gng_ref[...] for h in range(GLA_HEADS)]
    yg = jnp.concatenate(parts, axis=1) * sgg_ref[0].astype(F32)
    ym = ot_ref[0].astype(F32).T * smg_ref[0].astype(F32)
    y = _dot(yg.astype(BF16), wo_ref[0:GLA_V_W, :]) + _dot(ym.astype(BF16), wo_ref[GLA_V_W:, :])
    return _post_tail(x_ref[0], y, gate_ref[0], lng_ref[...], lnb_ref[...])


def _even_post_kernel(*refs):
    refs[-1][0] = _even_post_tile(*refs[:-1])


def _even_post_odd_pre_kernel(*refs):
    post_in, pre_par, (x_out_ref, u_ref, sf_ref, ys_ref) = refs[:11], refs[11:17], refs[17:]
    x_new = _even_post_tile(*post_in)
    x_out_ref[0] = x_new
    _odd_pre_tile(x_new, *pre_par, u_ref, sf_ref, ys_ref)


def _even_post(x, o_f, o_b, sgg, o_t, smg, gate, wo, gng, lng, lnb, li, l, tm):
    bsz, t, d = x.shape
    tok = lambda w: pl.BlockSpec((1, tm, w), lambda b, i: (b, i, 0))
    return pl.pallas_call(
        _even_post_kernel, grid=(bsz, t // tm),
        in_specs=[tok(d), tok(GLA_V_W), tok(GLA_V_W), tok(GLA_V_W),
                  pl.BlockSpec((1, MLA_V_W, tm), lambda b, i: (b, 0, i)), tok(MLA_V_W),
                  gate[1],
                  _layer(wo, li), _layer(gng, li), _layer(lng, l), _layer(lnb, l)],
        out_specs=tok(d), out_shape=jax.ShapeDtypeStruct((bsz, t, d), F32),
        compiler_params=_cparams("arbitrary", "arbitrary"), name="even_post",
    )(x, o_f, o_b, sgg, o_t, smg, gate[0], wo, gng, lng, lnb)


def _odd_pre_kernel(x_ref, *refs):
    _odd_pre_tile(x_ref[0], *refs)


def _odd_pre_tile(x, shift_ref, scale_ref, w_ref, dft_ref, sw_ref, sb_ref, u_ref, sf_ref, ys_ref):
    h = _ln(x) * (1.0 + scale_ref[0]) + shift_ref[0]
    z = _dot(h.astype(BF16), w_ref[...])
    for g in range(FNET_GROUPS):
        cs = slice(g * FNET_GROUP_CH, (g + 1) * FNET_GROUP_CH)
        ab = _dot(z[:, cs].astype(BF16), dft_ref[...])
        u_ref[0, :, cs] = ab[:, 0:FNET_GROUP_CH].astype(BF16)
        u_ref[0, :, FNET_W + g * FNET_GROUP_CH:FNET_W + (g + 1) * FNET_GROUP_CH] = ab[:, FNET_GROUP_CH:].astype(BF16)
    sf_ref[0] = _silu(z[:, FNET_W:2 * FNET_W]).astype(BF16)
    o0 = 2 * FNET_W
    tm = z.shape[0]
    for g in range(SGU_GROUPS):
        cs = slice(g * SGU_GROUP_CH, (g + 1) * SGU_GROUP_CH)
        ug = _gelu(z[:, o0 + g * SGU_GROUP_CH:o0 + (g + 1) * SGU_GROUP_CH])
        vg = _ln(_gelu(z[:, o0 + SGU_W + g * SGU_GROUP_CH:o0 + SGU_W + (g + 1) * SGU_GROUP_CH])).astype(BF16)
        sg = _silu(z[:, o0 + 2 * SGU_W + g * SGU_GROUP_CH:o0 + 2 * SGU_W + (g + 1) * SGU_GROUP_CH])
        for c in range(tm // SGU_CHUNK):
            rs = slice(c * SGU_CHUNK, (c + 1) * SGU_CHUNK)
            sv = _dot(sw_ref[g], vg[rs]) + sb_ref[g]
            ys_ref[0, rs, cs] = (ug[rs] * sv * sg[rs]).astype(BF16)


def _even_post_odd_pre(x, o_f, o_b, sgg, o_t, smg, gate, wo, gng, lng, lnb, li, l,
                       shift, scale, w, dft, sw, sb, lj, tm):
    bsz, t, d = x.shape
    tok = lambda w_: pl.BlockSpec((1, tm, w_), lambda b, i: (b, i, 0))
    return pl.pallas_call(
        _even_post_odd_pre_kernel, grid=(bsz, t // tm),
        in_specs=[tok(d), tok(GLA_V_W), tok(GLA_V_W), tok(GLA_V_W),
                  pl.BlockSpec((1, MLA_V_W, tm), lambda b, i: (b, 0, i)), tok(MLA_V_W),
                  gate[1],
                  _layer(wo, li), _layer(gng, li), _layer(lng, l), _layer(lnb, l),
                  shift[1], scale[1], _layer(w, lj), _full(dft.shape), _layer(sw, lj), _layer(sb, lj)],
        out_specs=(tok(d), tok(2 * FNET_W), tok(FNET_W), tok(SGU_W)),
        out_shape=(jax.ShapeDtypeStruct((bsz, t, d), F32),
                   jax.ShapeDtypeStruct((bsz, t, 2 * FNET_W), BF16),
                   jax.ShapeDtypeStruct((bsz, t, FNET_W), BF16),
                   jax.ShapeDtypeStruct((bsz, t, SGU_W), BF16)),
        compiler_params=_cparams("arbitrary", "arbitrary"), name="even_post_odd_pre",
    )(x, o_f, o_b, sgg, o_t, smg, gate[0], wo, gng, lng, lnb, shift[0], scale[0], w, dft, sw, sb)


def _odd_pre(x, shift, scale, w, dft, sw, sb, li, tm):
    bsz, t, d = x.shape
    tok = lambda w_: pl.BlockSpec((1, tm, w_), lambda b, i: (b, i, 0))
    return pl.pallas_call(
        _odd_pre_kernel, grid=(bsz, t // tm),
        in_specs=[tok(d), shift[1], scale[1], _layer(w, li), _full(dft.shape), _layer(sw, li), _layer(sb, li)],
        out_specs=(tok(2 * FNET_W), tok(FNET_W), tok(SGU_W)),
        out_shape=(jax.ShapeDtypeStruct((bsz, t, 2 * FNET_W), BF16),
                   jax.ShapeDtypeStruct((bsz, t, FNET_W), BF16),
                   jax.ShapeDtypeStruct((bsz, t, SGU_W), BF16)),
        compiler_params=_cparams("arbitrary", "arbitrary"), name="odd_pre",
    )(x, shift[0], scale[0], w, dft, sw, sb)


FFT_T2_BLK = 16
FFT_P1_BLK = 8


def _fft1_kernel(u_ref, w1_ref, tc_ref, ts_ref, z_ref):
    n1, tb = FFT_N1, FFT_T2_BLK
    r = n1 * tb
    pq = _dot(w1_ref[...], u_ref[0].reshape(r, 2 * FNET_W))
    zr = pq[0:r, 0:FNET_W] - pq[r:, FNET_W:]
    zi = -pq[0:r, FNET_W:] - pq[r:, 0:FNET_W]
    tc = _tile_lanes(tc_ref[0], FNET_W // 128)
    ts = _tile_lanes(ts_ref[0], FNET_W // 128)
    z_ref[0, :, :, 0:FNET_W] = (zr * tc + zi * ts).astype(BF16).reshape(n1, tb, FNET_W)
    z_ref[0, :, :, FNET_W:] = (zi * tc - zr * ts).astype(BF16).reshape(n1, tb, FNET_W)


def _fft2_kernel(z_ref, c2_ref, s2_ref, y_ref):
    for j in range(FFT_P1_BLK):
        zp = z_ref[0, j]
        y_ref[0, :, j, :] = _dot(c2_ref[...], zp[:, 0:FNET_W]) + _dot(s2_ref[...], zp[:, FNET_W:])


def _dft_tables(t):
    n1, n2 = FFT_N1, t // FFT_N1
    p1 = np.arange(n1, dtype=np.float64)
    a1 = 2.0 * np.pi * np.outer(p1, p1) / n1
    eye = np.eye(FFT_T2_BLK)
    w1 = np.concatenate([np.kron(np.cos(a1), eye), np.kron(np.sin(a1), eye)], axis=0)
    at = 2.0 * np.pi * np.outer(p1, np.arange(n2, dtype=np.float64)) / t

    def twiddle(a):
        a = a.reshape(n1, n2 // FFT_T2_BLK, FFT_T2_BLK).transpose(1, 0, 2).reshape(n2 // FFT_T2_BLK, -1)
        return np.repeat(a[:, :, None], 128, axis=2)
    tc, ts = twiddle(np.cos(at)), twiddle(np.sin(at))
    p2 = np.arange(n2, dtype=np.float64)
    a2 = 2.0 * np.pi * np.outer(p2, p2) / n2
    norm = 1.0 / math.sqrt(t * FNET_GROUP_CH)
    return (jnp.asarray(w1, F32).astype(BF16), jnp.asarray(tc, F32), jnp.asarray(ts, F32),
            jnp.asarray(np.cos(a2) * norm, F32).astype(BF16), jnp.asarray(np.sin(a2) * norm, F32).astype(BF16))


def _channel_dft_matrix():
    d = np.arange(FNET_GROUP_CH, dtype=np.float64)
    a = 2.0 * np.pi * np.outer(d, d) / FNET_GROUP_CH
    return jnp.asarray(np.concatenate([np.cos(a), np.sin(a)], axis=1), F32).astype(BF16)


def _fnet_long(u):
    bsz, t, w2 = u.shape
    n1, n2 = FFT_N1, t // FFT_N1
    w1, tc, ts, c2, s2 = _dft_tables(t)
    blk = pl.BlockSpec((1, n1, FFT_T2_BLK, w2), lambda b, i: (b, 0, i, 0))
    tw = pl.BlockSpec((1, n1 * FFT_T2_BLK, 128), lambda b, i: (i, 0, 0))
    z = pl.pallas_call(
        _fft1_kernel, grid=(bsz, n2 // FFT_T2_BLK),
        in_specs=[blk, _full(w1.shape), tw, tw],
        out_specs=blk,
        out_shape=jax.ShapeDtypeStruct((bsz, n1, n2, w2), BF16),
        compiler_params=_cparams("arbitrary", "arbitrary"), name="fnet_stage1",
    )(u.reshape(bsz, n1, n2, w2), w1, tc, ts)
    y = pl.pallas_call(
        _fft2_kernel, grid=(bsz, n1 // FFT_P1_BLK),
        in_specs=[pl.BlockSpec((1, FFT_P1_BLK, n2, w2), lambda b, i: (b, i, 0, 0)),
                  _full(c2.shape), _full(s2.shape)],
        out_specs=pl.BlockSpec((1, n2, FFT_P1_BLK, FNET_W), lambda b, i: (b, 0, i, 0)),
        out_shape=jax.ShapeDtypeStruct((bsz, n2, n1, FNET_W), F32),
        compiler_params=_cparams("arbitrary", "arbitrary"), name="fnet_stage2",
    )(z, c2, s2)
    return y.reshape(bsz, t, FNET_W)


def _fnet_short_kernel(u_ref, c_ref, s_ref, y_ref):
    u = u_ref[0]
    y_ref[0] = _dot(c_ref[...], u[:, 0:FNET_W]) - _dot(s_ref[...], u[:, FNET_W:])


def _fnet_short(u):
    bsz, t, w2 = u.shape
    p = np.arange(t, dtype=np.float64)
    a = 2.0 * np.pi * np.outer(p, p) / t
    norm = 1.0 / math.sqrt(t * FNET_GROUP_CH)
    c, s = jnp.asarray(np.cos(a) * norm, F32).astype(BF16), jnp.asarray(np.sin(a) * norm, F32).astype(BF16)
    return pl.pallas_call(
        _fnet_short_kernel, grid=(bsz,),
        in_specs=[pl.BlockSpec((1, t, w2), lambda b: (b, 0, 0)), _full(c.shape), _full(s.shape)],
        out_specs=pl.BlockSpec((1, t, FNET_W), lambda b: (b, 0, 0)),
        out_shape=jax.ShapeDtypeStruct((bsz, t, FNET_W), F32),
        compiler_params=_cparams("arbitrary"), name="fnet_short",
    )(u, c, s)


def _odd_post_tile(x_ref, fr_ref, sf_ref, ys_ref, gate_ref, wo_ref, lng_ref, lnb_ref):
    yf = (fr_ref[0] * sf_ref[0].astype(F32)).astype(BF16)
    y = _dot(yf, wo_ref[0:FNET_W, :]) + _dot(ys_ref[0], wo_ref[FNET_W:, :])
    return _post_tail(x_ref[0], y, gate_ref[0], lng_ref[...], lnb_ref[...])


def _odd_post_kernel(*refs):
    refs[-1][0] = _odd_post_tile(*refs[:-1])


def _odd_post_even_pre_kernel(*refs):
    post_in, pre_par, x_out_ref, pre_out = refs[:8], refs[8:23], refs[23], refs[24:]
    x_new = _odd_post_tile(*post_in)
    x_out_ref[0] = x_new
    _even_pre_tile(x_new, *pre_par, *pre_out)


def _odd_post_even_pre(x, fr, sf, ys, gate, wo, lng, lnb, li, l, shift, scale, wts, lj, tabs, tm):
    bsz, t, d = x.shape
    hq = MLA_HEADS * MLA_HEAD_PAD
    tok = lambda w: pl.BlockSpec((1, tm, w), lambda b, i: (b, i, 0))
    tab = pl.BlockSpec((tm, MLA_HEAD_PAD), lambda b, i: (i, 0))
    pre_shapes, pre_specs = _even_pre_outputs(bsz, t, tm)
    place = _rotary_key_placement()
    return pl.pallas_call(
        _odd_post_even_pre_kernel, grid=(bsz, t // tm),
        in_specs=[tok(d), tok(FNET_W), tok(FNET_W), tok(SGU_W), gate[1],
                  _layer(wo, li), _layer(lng, l), _layer(lnb, l),
                  shift[1], scale[1]] + [_layer(w, lj) for w in wts] + [tab] * 2 + [_full(place.shape)],
        out_specs=(tok(d),) + pre_specs,
        out_shape=(jax.ShapeDtypeStruct((bsz, t, d), F32),) + pre_shapes,
        compiler_params=_cparams("arbitrary", "arbitrary"), name="odd_post_even_pre",
    )(x, fr, sf, ys, gate[0], wo, lng, lnb, shift[0], scale[0], *wts, *tabs, place)


def _odd_post(x, fr, sf, ys, gate, wo, lng, lnb, li, l, tm):
    bsz, t, d = x.shape
    tok = lambda w: pl.BlockSpec((1, tm, w), lambda b, i: (b, i, 0))
    return pl.pallas_call(
        _odd_post_kernel, grid=(bsz, t // tm),
        in_specs=[tok(d), tok(FNET_W), tok(FNET_W), tok(SGU_W),
                  gate[1],
                  _layer(wo, li), _layer(lng, l), _layer(lnb, l)],
        out_specs=tok(d), out_shape=jax.ShapeDtypeStruct((bsz, t, d), F32),
        compiler_params=_cparams("arbitrary", "arbitrary"), name="odd_post",
    )(x, fr, sf, ys, gate[0], wo, lng, lnb)


def _even_weights(w_in, gla_w2, gla_b, q_norm_g, w_uq, kv_norm_g, w_ukv):
    d = w_in.shape[0]
    idx = np.cumsum(EVEN_IN_SIZES)[:-1].tolist()
    gq, gk, gv, glr, gg, cq, ckv, kr, mg = jnp.split(w_in, idx, axis=1)
    half = MLA_ROPE // 2
    z = lambda n: jnp.zeros((d, n), w_in.dtype)
    small = jnp.concatenate([glr, kr, kr[:, half:], kr[:, :half], z(128 - 2 * GLA_GATE_RANK - 2 * MLA_ROPE)], axis=1)
    w = jnp.concatenate([gq * GLA_DK ** -0.5, gk, gv, gg, cq, ckv, mg, small], axis=1).astype(BF16)
    zw = jnp.zeros((GLA_GATE_RANK, GLA_QK_W), F32)
    w2 = jnp.concatenate([jnp.concatenate([gla_w2[0], zw], axis=1), jnp.concatenate([zw, gla_w2[1]], axis=1),
                          jnp.zeros((128 - 2 * GLA_GATE_RANK, 2 * GLA_QK_W), F32)], axis=0)
    gb = jnp.concatenate([gla_b[0], gla_b[1]])[None, :]
    uq = w_uq.reshape(MLA_Q_RANK, MLA_HEADS, MLA_NOPE + MLA_ROPE)
    pad = MLA_HEAD_PAD - MLA_NOPE - MLA_ROPE
    zq = lambda n: jnp.zeros((MLA_Q_RANK, MLA_HEADS, n), w_uq.dtype)
    wqa = jnp.concatenate([uq, zq(pad)], axis=2).reshape(MLA_Q_RANK, -1)
    wqb = jnp.concatenate([zq(MLA_NOPE), uq[:, :, MLA_NOPE + half:], uq[:, :, MLA_NOPE:MLA_NOPE + half], zq(pad)],
                          axis=2).reshape(MLA_Q_RANK, -1)
    ukv = w_ukv.reshape(MLA_KV_RANK, MLA_HEADS, MLA_NOPE + MLA_V)
    wk = jnp.concatenate([ukv[:, :, :MLA_NOPE], jnp.zeros((MLA_KV_RANK, MLA_HEADS, MLA_HEAD_PAD - MLA_NOPE), w_ukv.dtype)],
                         axis=2).reshape(MLA_KV_RANK, -1)
    vpad = MLA_VT_ROWS - MLA_V
    wv = jnp.concatenate([ukv[:, :, MLA_NOPE:], jnp.zeros((MLA_KV_RANK, MLA_HEADS, vpad), w_ukv.dtype)],
                         axis=2).reshape(MLA_KV_RANK, -1)
    vbias = np.zeros((MLA_HEADS, MLA_VT_ROWS), np.float32)
    vbias[:, MLA_V] = 1.0
    return (w, w2.astype(BF16), gb, q_norm_g[None, :], wqa.astype(BF16), wqb.astype(BF16),
            kv_norm_g[None, :], wk.astype(BF16), wv.astype(BF16), jnp.asarray(vbias.reshape(1, -1)))


def _rope_tables(n):
    row = jnp.repeat(jnp.arange(n // GRID_W, dtype=F32), GRID_W)
    col = (jnp.arange(n) % GRID_W).astype(F32)
    n_freq = MLA_ROPE // 4
    inv = ROPE_BASE ** (-jnp.arange(n_freq, dtype=F32) / n_freq)
    ang = jnp.concatenate([row[:, None] * inv, col[:, None] * inv], -1)
    cos, sin = jnp.cos(ang), jnp.sin(ang)
    pad = jnp.zeros((n, MLA_HEAD_PAD - MLA_NOPE - MLA_ROPE), F32)
    zn = jnp.zeros((n, MLA_NOPE), F32)
    ck = jnp.concatenate([zn, cos, cos, pad], axis=1)
    sk = jnp.concatenate([zn, -sin, sin, pad], axis=1)
    return ck, sk


def _plain_tables(n):
    pad = jnp.zeros((n, MLA_HEAD_PAD - MLA_NOPE - MLA_ROPE), F32)
    zn = jnp.zeros((n, MLA_NOPE), F32)
    ck = jnp.concatenate([zn, jnp.ones((n, MLA_ROPE), F32), pad], axis=1)
    return ck, jnp.zeros((n, MLA_HEAD_PAD), F32)


def _pick(t, pref):
    return pref if t % pref == 0 else t


def kernel(x, c, ctx, c_ctx, ada_w, ada_b, post_ln_g, post_ln_b, even_w_in, gla_w2, gla_b, gla_norm_g,
           mla_q_norm_g, mla_w_uq, mla_kv_norm_g, mla_w_ukv, even_w_out, odd_w_in, sgu_w, sgu_b, odd_w_out):
    bsz, n, d = x.shape
    lc = ctx.shape[1]
    depth = ada_w.shape[0]
    assert bsz + 1 <= 8 and n % (FFT_N1 * FFT_T2_BLK) == 0 and n % 512 == 0 and lc % MLA_TK == 0

    cond = jnp.concatenate([c, c_ctx[None, :], jnp.zeros((8 - bsz - 1, d), F32)], axis=0)
    mods = _mods(cond, ada_w, ada_b)

    mods4 = mods.reshape(depth, 8, 1, 3 * d)

    def lat_mod(l, j):
        return mods4, pl.BlockSpec((None, 1, 1, d), lambda b, *_: (l, b, 0, j))

    def ctx_mod(l, j):
        return mods4, pl.BlockSpec((None, 1, 1, d), lambda b, *_: (l, bsz, 0, j))

    rope_tabs = _rope_tables(n)
    ctx_tabs = _plain_tables(lc)
    dft_c = _channel_dft_matrix()
    tm_lat, tm_ctx = _pick(n, 512), _pick(lc, 256)
    tm_post = _pick(n, 1024)
    zero_state = jnp.zeros((bsz, 2, GLA_QK_W, GLA_DV), F32)

    e_wts = jax.vmap(_even_weights)(even_w_in, gla_w2, gla_b, mla_q_norm_g, mla_w_uq, mla_kv_norm_g, mla_w_ukv)
    e_wo = even_w_out.astype(BF16)
    e_gng = gla_norm_g[:, None, :]
    o_w = odd_w_in.astype(BF16)
    o_wo = odd_w_out.astype(BF16)
    o_sw = sgu_w.astype(BF16)
    o_sb = jnp.broadcast_to(sgu_b[:, :, :, None], sgu_b.shape + (SGU_GROUP_CH,))
    lng, lnb = post_ln_g[:, None, :], post_ln_b[:, None, :]

    odd_in = even_in = None
    for l in range(depth):
        need_ctx_out = any(j % 2 == 0 for j in range(l + 1, depth))
        i = l // 2
        if l % 2 == 0:
            pc = _even_pre(ctx, ctx_mod(l, 0), ctx_mod(l, 1), e_wts, i, ctx_tabs, tm_ctx)
            q_c, k_c, v_c, g_c, sgg_c, smg_c, qt_c, kk_c, vt_c = pc
            if even_in is None:
                even_in = _even_pre(x, lat_mod(l, 0), lat_mod(l, 1), e_wts, i, rope_tabs, tm_lat)
            q_l, k_l, v_l, g_l, sgg_l, smg_l, qt_l, kk_l, vt_l = even_in
            even_in = None
            of_c, ob_c, s_c = _gla(q_c, k_c, v_c, g_c, zero_state, GLA_SUB)
            of_l, ob_l, _ = _gla(q_l, k_l, v_l, g_l, s_c, 2 * GLA_SUB)
            ot_l = _mla(qt_l, [(kk_l, vt_l), (kk_c, vt_c)], 512, MLA_TK, n_sub=2 if n % 1024 == 0 else 1)
            if l + 1 < depth:
                x_new, *odd_in = _even_post_odd_pre(
                    x, of_l, ob_l, sgg_l, ot_l, smg_l, lat_mod(l, 2), e_wo, e_gng, lng, lnb, i, l,
                    lat_mod(l + 1, 0), lat_mod(l + 1, 1), o_w, dft_c, o_sw, o_sb, (l + 1) // 2, tm_lat)
            else:
                x_new = _even_post(x, of_l, ob_l, sgg_l, ot_l, smg_l, lat_mod(l, 2), e_wo, e_gng, lng, lnb,
                                   i, l, tm_post)
            if need_ctx_out:
                ot_c = _mla(qt_c, [(kk_c, vt_c)], lc, MLA_TK)
                ctx = _even_post(ctx, of_c, ob_c, sgg_c, ot_c, smg_c, ctx_mod(l, 2), e_wo, e_gng, lng, lnb, i, l, tm_ctx)
            x = x_new
        else:
            if odd_in is None:
                odd_in = _odd_pre(x, lat_mod(l, 0), lat_mod(l, 1), o_w, dft_c, o_sw, o_sb, i, tm_lat)
            u, sf, ys = odd_in
            odd_in = None
            if l + 1 < depth:
                x_new, *even_in = _odd_post_even_pre(
                    x, _fnet_long(u), sf, ys, lat_mod(l, 2), o_wo, lng, lnb, i, l,
                    lat_mod(l + 1, 0), lat_mod(l + 1, 1), e_wts, (l + 1) // 2, rope_tabs, tm_lat)
            else:
                x_new = _odd_post(x, _fnet_long(u), sf, ys, lat_mod(l, 2), o_wo, lng, lnb, i, l, tm_post)
            if need_ctx_out:
                u, sf, ys = _odd_pre(ctx, ctx_mod(l, 0), ctx_mod(l, 1), o_w, dft_c, o_sw, o_sb, i, tm_ctx)
                ctx = _odd_post(ctx, _fnet_short(u), sf, ys, ctx_mod(l, 2), o_wo, lng, lnb, i, l, tm_ctx)
            x = x_new
    return x
```

```python
import functools
import math

import numpy as np
import jax
import jax.numpy as jnp
from jax import lax
from jax.experimental import pallas as pl
from jax.experimental.pallas import tpu as pltpu

F32 = jnp.float32
BF16 = jnp.bfloat16

DEPTH = 4
GRID_W = 64
DEEPNORM_ALPHA = (2 * DEPTH) ** 0.25
LN_EPS = 1e-6

GLA_HEADS = 4
GLA_DK = 64
GLA_DV = 128
GLA_QK_W = GLA_HEADS * GLA_DK
GLA_V_W = GLA_HEADS * GLA_DV
GLA_GATE_RANK = 16
GLA_TAU = 16.0
GLA_CHUNK = 64

MLA_HEADS = 8
MLA_NOPE = 64
MLA_ROPE = 32
MLA_V = 64
MLA_Q_RANK = 256
MLA_KV_RANK = 128
MLA_V_W = MLA_HEADS * MLA_V
MLA_SCALE = (MLA_NOPE + MLA_ROPE) ** -0.5
ROPE_BASE = 10000.0
MLA_HEAD_PAD = 128
MLA_VT_ROWS = MLA_V + 16
MLA_VT_W = MLA_HEADS * MLA_VT_ROWS

FNET_GROUPS = 4
FNET_GROUP_CH = 128
FNET_W = FNET_GROUPS * FNET_GROUP_CH
FFT_N1 = 64

SGU_GROUPS = 4
SGU_GROUP_CH = 128
SGU_W = SGU_GROUPS * SGU_GROUP_CH
SGU_CHUNK = 128

EVEN_IN_SIZES = (GLA_QK_W, GLA_QK_W, GLA_V_W, 2 * GLA_GATE_RANK, GLA_V_W,
                 MLA_Q_RANK, MLA_KV_RANK, MLA_ROPE, MLA_V_W)

E_GQ, E_GK, E_GV, E_GG, E_CQ, E_CKV, E_MG, E_SMALL, E_END = (
    0, 256, 512, 1024, 1536, 1792, 1920, 2432, 2560)
SM_KR = 2 * GLA_GATE_RANK
SM_KR_SWAPPED = MLA_NOPE

VMEM_LIMIT_BYTES = 56 * 1024 * 1024
LOG2E = math.log2(math.e)
NEG_BIG = -1e30


def _cparams(*sem):
    return pltpu.CompilerParams(dimension_semantics=sem, vmem_limit_bytes=VMEM_LIMIT_BYTES)


def _dot(a, b):
    return jnp.dot(a, b, preferred_element_type=F32)


def _dot_nt(a, b):
    return lax.dot_general(a, b, (((1,), (1,)), ((), ())), preferred_element_type=F32)


def _dot_tn(a, b):
    return lax.dot_general(a, b, (((0,), (0,)), ((), ())), preferred_element_type=F32)


def _ln(x):
    xc = x - jnp.mean(x, -1, keepdims=True)
    return xc * lax.rsqrt(jnp.mean(xc * xc, -1, keepdims=True) + LN_EPS)


def _rms(x):
    return x * lax.rsqrt(jnp.mean(x * x, -1, keepdims=True) + LN_EPS)


def _silu(x):
    return x / (1.0 + jnp.exp(-x))


def _gelu(x):
    return 0.5 * x * (1.0 + lax.erf(x * (2.0 ** -0.5)))


def _tile_lanes(x, reps):
    return jnp.concatenate([x] * reps, axis=1)


def _full(shape):
    n = len(shape)
    return pl.BlockSpec(shape, lambda *_: (0,) * n)


def _layer(arr, i):
    n = arr.ndim
    return pl.BlockSpec((None,) + arr.shape[1:], lambda *_: (i,) + (0,) * (n - 1))


def _mods_kernel(cond_ref, w_ref, b_ref, o_ref):
    s = _silu(cond_ref[...])
    o_ref[0] = _dot(s.astype(BF16), w_ref[0].astype(BF16)) + b_ref[0]


def _mods(cond, ada_w, ada_b):
    depth, d, d3 = ada_w.shape
    return pl.pallas_call(
        _mods_kernel,
        grid=(depth,),
        in_specs=[pl.BlockSpec((8, d), lambda l: (0, 0)),
                  pl.BlockSpec((1, d, d3), lambda l: (l, 0, 0)),
                  pl.BlockSpec((1, 1, d3), lambda l: (l, 0, 0))],
        out_specs=pl.BlockSpec((1, 8, d3), lambda l: (l, 0, 0)),
        out_shape=jax.ShapeDtypeStruct((depth, 8, d3), F32),
        compiler_params=_cparams("arbitrary"),
        name="ada_mod",
    )(cond, ada_w, ada_b.reshape(depth, 1, d3))


def _even_pre_kernel(x_ref, *refs):
    _even_pre_tile(x_ref[0], *refs)


def _even_pre_tile(x, shift_ref, scale_ref, w_ref, w2_ref, gb_ref, qg_ref, wqa_ref, wqb_ref,
                   kvg_ref, wk_ref, wv_ref, vb_ref, ck_ref, sk_ref, place_ref,
                   q_ref, k_ref, v_ref, g_ref, sgg_ref, smg_ref, qt_ref, kk_ref, vt_ref):
    h = _ln(x) * (1.0 + scale_ref[0]) + shift_ref[0]
    z = _dot(h.astype(BF16), w_ref[...])
    q_ref[0] = z[:, E_GQ:E_GK].astype(BF16)
    k_ref[0] = z[:, E_GK:E_GV].astype(BF16)
    v_ref[0] = z[:, E_GV:E_GG].astype(BF16)
    sgg_ref[0] = _silu(z[:, E_GG:E_CQ]).astype(BF16)
    smg_ref[0] = _silu(z[:, E_MG:E_SMALL]).astype(BF16)
    small = z[:, E_SMALL:E_END]
    small16 = small.astype(BF16)
    pre = _dot(small16, w2_ref[...]) + gb_ref[...]
    g_ref[0] = jax.nn.log_sigmoid(pre) * (1.0 / GLA_TAU)
    cqn = (_rms(z[:, E_CQ:E_CKV]) * qg_ref[...]).astype(BF16)
    lane = lax.broadcasted_iota(jnp.int32, (1, MLA_HEAD_PAD), 1)
    q_scale = MLA_SCALE * LOG2E
    cq = ck_ref[...] * q_scale + jnp.where(lane < MLA_NOPE, q_scale, 0.0)
    sq = sk_ref[...] * q_scale
    qfull = (_dot(cqn, wqa_ref[...]) * _tile_lanes(cq, MLA_HEADS)
             + _dot(cqn, wqb_ref[...]) * _tile_lanes(sq, MLA_HEADS))
    qt_ref[0] = qfull.T.astype(BF16)
    ckvn = (_rms(z[:, E_CKV:E_MG]) * kvg_ref[...]).astype(BF16)
    kr = _dot(small16, place_ref[...]) * ck_ref[...] + small * sk_ref[...]
    kk_ref[0] = (_dot(ckvn, wk_ref[...]) + _tile_lanes(kr, MLA_HEADS)).astype(BF16)
    vt_ref[0] = (_dot(ckvn, wv_ref[...]) + vb_ref[...]).T.astype(BF16)


def _even_pre_outputs(bsz, t, tm):
    hq = MLA_HEADS * MLA_HEAD_PAD
    tok = lambda w: pl.BlockSpec((1, tm, w), lambda b, i: (b, i, 0))
    out_shape = (
        jax.ShapeDtypeStruct((bsz, t, GLA_QK_W), BF16),
        jax.ShapeDtypeStruct((bsz, t, GLA_QK_W), BF16),
        jax.ShapeDtypeStruct((bsz, t, GLA_V_W), BF16),
        jax.ShapeDtypeStruct((bsz, t, 2 * GLA_QK_W), F32),
        jax.ShapeDtypeStruct((bsz, t, GLA_V_W), BF16),
        jax.ShapeDtypeStruct((bsz, t, MLA_V_W), BF16),
        jax.ShapeDtypeStruct((bsz, hq, t), BF16),
        jax.ShapeDtypeStruct((bsz, t, hq), BF16),
        jax.ShapeDtypeStruct((bsz, MLA_VT_W, t), BF16),
    )
    out_specs = (tok(GLA_QK_W), tok(GLA_QK_W), tok(GLA_V_W), tok(2 * GLA_QK_W), tok(GLA_V_W), tok(MLA_V_W),
                 pl.BlockSpec((1, hq, tm), lambda b, i: (b, 0, i)),
                 tok(hq),
                 pl.BlockSpec((1, MLA_VT_W, tm), lambda b, i: (b, 0, i)))
    return out_shape, out_specs


def _even_pre(x, shift, scale, wts, li, tabs, tm):
    bsz, t, d = x.shape
    tok = lambda w: pl.BlockSpec((1, tm, w), lambda b, i: (b, i, 0))
    tab = pl.BlockSpec((tm, MLA_HEAD_PAD), lambda b, i: (i, 0))
    out_shape, out_specs = _even_pre_outputs(bsz, t, tm)
    place = _rotary_key_placement()
    in_specs = [tok(d), shift[1], scale[1]] + [_layer(w, li) for w in wts] + [tab] * 2 + [_full(place.shape)]
    return pl.pallas_call(
        _even_pre_kernel, grid=(bsz, t // tm), in_specs=in_specs, out_specs=out_specs, out_shape=out_shape,
        compiler_params=_cparams("arbitrary", "arbitrary"), name="even_pre",
    )(x, shift[0], scale[0], *wts, *tabs, place)


def _rotary_key_placement():
    m = np.zeros((128, MLA_HEAD_PAD), np.float32)
    m[SM_KR + np.arange(MLA_ROPE), MLA_NOPE + np.arange(MLA_ROPE)] = 1.0
    return jnp.asarray(m, BF16)


def _gla_block(q, k, v, g, tmat, same_tri, mid_off, end_off, head_masks, chunk_masks):
    r = q.shape[0]
    c = GLA_CHUNK
    nc = r // c
    g_hi = g.astype(BF16)
    g_lo = (g - g_hi.astype(F32)).astype(BF16)
    b = _dot(tmat, g_hi) + _dot(tmat, g_lo)
    mid_rows = [b[j * c + mid_off:j * c + mid_off + 1] for j in range(nc)]
    end_rows = [b[j * c + end_off:j * c + end_off + 1] for j in range(nc)]
    per_chunk = lambda rows: jnp.concatenate([jnp.broadcast_to(x, (c, GLA_QK_W)) for x in rows], axis=0)
    b_mid = per_chunk(mid_rows)
    e_up = jnp.exp(b - b_mid)
    e_dn = jnp.exp(b_mid - b)
    qe = q * e_up
    ke = (k * e_dn).astype(BF16)
    kd_t = (k * (e_dn * per_chunk([jnp.exp(e - m) for e, m in zip(end_rows, mid_rows)]))).T
    qb = q * (e_up * per_chunk([jnp.exp(m) for m in mid_rows]))
    stack = lambda a: jnp.concatenate([jnp.where(m, a, 0.0) for m in head_masks], axis=0).astype(BF16)
    att = _dot_nt(stack(qe), ke)
    qb_st = stack(qb)
    dec_t = jnp.exp(jnp.concatenate(end_rows + [jnp.zeros((8 - nc, GLA_QK_W), F32)], axis=0)).T
    intra, ds = [], []
    for h in range(GLA_HEADS):
        a_h = jnp.where(same_tri, att[h * r:(h + 1) * r], 0.0).astype(BF16)
        v_h = v[:, h * GLA_DV:(h + 1) * GLA_DV]
        intra.append(_dot(a_h, v_h))
        kd_h = kd_t[h * GLA_DK:(h + 1) * GLA_DK]
        lhs = jnp.concatenate([jnp.where(cm, kd_h, 0.0) for cm in chunk_masks], axis=0).astype(BF16)
        ds.append(_dot(lhs, v_h))
    return intra, ds, qb_st, dec_t


def _gla_recur(par, s, o_ref, row0, reverse):
    intra, ds, qb_st, dec_t = par
    c = GLA_CHUNK
    r = intra[0].shape[0]
    nc = r // c
    for j in (reversed(range(nc)) if reverse else range(nc)):
        rows = slice(j * c, (j + 1) * c)
        st = jnp.concatenate([qb_st[h * r + j * c:h * r + (j + 1) * c] for h in range(GLA_HEADS)], axis=0)
        o_inter = _dot(st, s.astype(BF16))
        for h in range(GLA_HEADS):
            o_ref[0, row0 + j * c:row0 + (j + 1) * c, h * GLA_DV:(h + 1) * GLA_DV] = (
                intra[h][rows] + o_inter[h * c:(h + 1) * c]).astype(o_ref.dtype)
        dec = jnp.broadcast_to(dec_t[:, j:j + 1], (GLA_QK_W, GLA_DV))
        s = dec * s + jnp.concatenate([ds[h][rows] for h in range(GLA_HEADS)], axis=0)
    return s


GLA_SUB = 256


def _gla_kernel(qf_ref, kf_ref, vf_ref, gf_ref, qb_ref, kb_ref, vb_ref, gb_ref, s0_ref, tl_ref, tu_ref,
                of_ref, ob_ref, sfin_ref, s_scr, *, tb):
    i = pl.program_id(1)

    @pl.when(i == 0)
    def _():
        s_scr[...] = s0_ref[0]

    c, r = GLA_CHUNK, GLA_SUB
    row = lax.broadcasted_iota(jnp.int32, (r, r), 0)
    col = lax.broadcasted_iota(jnp.int32, (r, r), 1)
    same = (row // c) == (col // c)
    lane_head = lax.broadcasted_iota(jnp.int32, (r, GLA_QK_W), 1) // GLA_DK
    head_masks = [lane_head == h for h in range(GLA_HEADS)]
    lane_chunk = lax.broadcasted_iota(jnp.int32, (GLA_DK, r), 1) // c
    chunk_masks = [lane_chunk == j for j in range(r // c)]

    par_f, par_b = [], []
    for u in range(tb // r):
        rs = slice(u * r, (u + 1) * r)
        par_f.append(_gla_block(qf_ref[0, rs].astype(F32), kf_ref[0, rs].astype(F32), vf_ref[0, rs], gf_ref[0, rs],
                                tl_ref[...], same & (col <= row), c // 2 - 1, c - 1, head_masks, chunk_masks))
        par_b.append(_gla_block(qb_ref[0, rs].astype(F32), kb_ref[0, rs].astype(F32), vb_ref[0, rs], gb_ref[0, rs],
                                tu_ref[...], same & (col >= row), c // 2, 0, head_masks, chunk_masks))
    s = s_scr[0]
    for u in range(tb // r):
        s = _gla_recur(par_f[u], s, of_ref, u * r, False)
    s_scr[0] = s
    s = s_scr[1]
    for u in reversed(range(tb // r)):
        s = _gla_recur(par_b[u], s, ob_ref, u * r, True)
    s_scr[1] = s

    @pl.when(i == pl.num_programs(1) - 1)
    def _():
        sfin_ref[0] = s_scr[...]


def _gla(q, k, v, g, s0, tb):
    bsz, t, _ = q.shape
    nblk = t // tb
    assert tb % GLA_SUB == 0
    idx = np.arange(GLA_SUB)
    same = (idx[:, None] // GLA_CHUNK) == (idx[None, :] // GLA_CHUNK)
    tl = jnp.asarray(same & (idx[None, :] <= idx[:, None]), BF16)
    tu = jnp.asarray(same & (idx[None, :] >= idx[:, None]), BF16)
    fwd = lambda w: pl.BlockSpec((1, tb, w), lambda b, i: (b, i, 0))
    bwd = lambda w: pl.BlockSpec((1, tb, w), lambda b, i: (b, nblk - 1 - i, 0))
    st = pl.BlockSpec((1, 2, GLA_QK_W, GLA_DV), lambda b, i: (b, 0, 0, 0))
    return pl.pallas_call(
        functools.partial(_gla_kernel, tb=tb),
        grid=(bsz, nblk),
        in_specs=[fwd(GLA_QK_W), fwd(GLA_QK_W), fwd(GLA_V_W),
                  pl.BlockSpec((1, tb, GLA_QK_W), lambda b, i: (b, i, 0)),
                  bwd(GLA_QK_W), bwd(GLA_QK_W), bwd(GLA_V_W),
                  pl.BlockSpec((1, tb, GLA_QK_W), lambda b, i: (b, nblk - 1 - i, 1)),
                  st, _full((GLA_SUB, GLA_SUB)), _full((GLA_SUB, GLA_SUB))],
        out_specs=(fwd(GLA_V_W), bwd(GLA_V_W), st),
        out_shape=(jax.ShapeDtypeStruct((bsz, t, GLA_V_W), BF16),
                   jax.ShapeDtypeStruct((bsz, t, GLA_V_W), BF16),
                   jax.ShapeDtypeStruct((bsz, 2, GLA_QK_W, GLA_DV), F32)),
        scratch_shapes=[pltpu.VMEM((2, GLA_QK_W, GLA_DV), F32)],
        compiler_params=_cparams("arbitrary", "arbitrary"), name="gla_scan",
    )(q, k, v, g, q, k, v, g, s0, tl, tu)


MLA_SLOTS = 3
MLA_TK = 256


def _mla_kernel(qt_ref, *refs, chunks, tk, n_sub):
    n_src = max(src for src, _ in chunks) + 1
    kv_refs, o_ref, scratch = refs[:2 * n_src], refs[2 * n_src], refs[2 * n_src + 1:]
    tq = qt_ref.shape[2] // n_sub
    for sub in range(n_sub):
        cols = slice(sub * tq, (sub + 1) * tq)
        _mla_pipeline(qt_ref[0, :, cols], kv_refs, o_ref, cols,
                      scratch[2 * MLA_SLOTS * sub:2 * MLA_SLOTS * (sub + 1)], chunks, tk)


def _mla_pipeline(q_t, kv_refs, o_ref, cols, scratch, chunks, tk):
    n_chunks = len(chunks)
    tq = q_t.shape[1]
    s_bufs, p_bufs = scratch[:MLA_SLOTS], scratch[MLA_SLOTS:]
    cmax = None
    m = jnp.full((1, tq), NEG_BIG, F32)
    alpha = None
    acc = jnp.zeros((MLA_VT_ROWS, tq), F32)
    for t in range(n_chunks + 2):
        alpha_prev = alpha
        if 1 <= t <= n_chunks:
            e = (t - 1) % MLA_SLOTS
            m_new = jnp.maximum(m, jnp.max(cmax, axis=0, keepdims=True))
            alpha = jnp.exp2(m - m_new)
            m = m_new
            p_bufs[e][...] = jnp.exp2(s_bufs[e][...] - m).astype(BF16)
        if t < n_chunks:
            src, lo = chunks[t]
            s_t = _dot(kv_refs[2 * src][0, lo:lo + tk, :], q_t)
            s_bufs[t % MLA_SLOTS][...] = s_t
            cmax = jnp.max(s_t.reshape(tk // 8, 8, tq), axis=0)
        if t >= 2:
            src, lo = chunks[t - 2]
            vt = kv_refs[2 * src + 1][0, :, lo:lo + tk]
            acc = alpha_prev * acc + _dot(vt, p_bufs[(t - 2) % MLA_SLOTS][...])
    o_ref[0, :, cols] = (acc[0:MLA_V] / acc[MLA_V:MLA_V + 1]).astype(o_ref.dtype)


def _mla(qt, kv, tq, tk, n_sub=1):
    bsz, _, t = qt.shape
    tq_step = tq * n_sub
    assert t % tq_step == 0 and all(k.shape[1] % tk == 0 for k, _ in kv)
    chunks = [(src, lo) for src, (k, _) in enumerate(kv) for lo in range(0, k.shape[1], tk)]
    in_specs = [pl.BlockSpec((1, MLA_HEAD_PAD, tq_step), lambda b, h, i: (b, h, i))]
    for k, _ in kv:
        in_specs += [pl.BlockSpec((1, k.shape[1], MLA_HEAD_PAD), lambda b, h, i: (b, 0, h)),
                     pl.BlockSpec((1, MLA_VT_ROWS, k.shape[1]), lambda b, h, i: (b, h, 0))]
    return pl.pallas_call(
        functools.partial(_mla_kernel, chunks=chunks, tk=tk, n_sub=n_sub),
        grid=(bsz, MLA_HEADS, t // tq_step),
        in_specs=in_specs,
        out_specs=pl.BlockSpec((1, MLA_V, tq_step), lambda b, h, i: (b, h, i)),
        out_shape=jax.ShapeDtypeStruct((bsz, MLA_V_W, t), BF16),
        scratch_shapes=([pltpu.VMEM((tk, tq), F32)] * MLA_SLOTS + [pltpu.VMEM((tk, tq), BF16)] * MLA_SLOTS) * n_sub,
        compiler_params=_cparams("arbitrary", "arbitrary", "arbitrary"), name="mla_attn",
    )(qt, *[a for pair in kv for a in pair])


def _post_tail(x, y, gate, lng, lnb):
    return _ln(DEEPNORM_ALPHA * x + gate * y) * lng + lnb


def _even_post_tile(x_ref, of_ref, ob_ref, sgg_ref, ot_ref, smg_ref, gate_ref, wo_ref, gng_ref, lng_ref, lnb_ref):
    o = of_ref[0].astype(F32) + ob_ref[0].astype(F32)
    parts = [_rms(o[:, h * GLA_DV:(h + 1) * GLA_DV]) * gng_ref[...] for h in range(GLA_HEADS)]
    yg = jnp.concatenate(parts, axis=1) * sgg_ref[0].astype(F32)
    ym = ot_ref[0].astype(F32).T * smg_ref[0].astype(F32)
    y = _dot(yg.astype(BF16), wo_ref[0:GLA_V_W, :]) + _dot(ym.astype(BF16), wo_ref[GLA_V_W:, :])
    return _post_tail(x_ref[0], y, gate_ref[0], lng_ref[...], lnb_ref[...])


def _even_post_kernel(*refs):
    refs[-1][0] = _even_post_tile(*refs[:-1])


def _even_post_odd_pre_kernel(*refs):
    post_in, pre_par, (x_out_ref, u_ref, sf_ref, ys_ref) = refs[:11], refs[11:17], refs[17:]
    x_new = _even_post_tile(*post_in)
    x_out_ref[0] = x_new
    _odd_pre_tile(x_new, *pre_par, u_ref, sf_ref, ys_ref)


def _even_post(x, o_f, o_b, sgg, o_t, smg, gate, wo, gng, lng, lnb, li, l, tm):
    bsz, t, d = x.shape
    tok = lambda w: pl.BlockSpec((1, tm, w), lambda b, i: (b, i, 0))
    return pl.pallas_call(
        _even_post_kernel, grid=(bsz, t // tm),
        in_specs=[tok(d), tok(GLA_V_W), tok(GLA_V_W), tok(GLA_V_W),
                  pl.BlockSpec((1, MLA_V_W, tm), lambda b, i: (b, 0, i)), tok(MLA_V_W),
                  gate[1],
                  _layer(wo, li), _layer(gng, li), _layer(lng, l), _layer(lnb, l)],
        out_specs=tok(d), out_shape=jax.ShapeDtypeStruct((bsz, t, d), F32),
        compiler_params=_cparams("arbitrary", "arbitrary"), name="even_post",
    )(x, o_f, o_b, sgg, o_t, smg, gate[0], wo, gng, lng, lnb)


def _odd_pre_kernel(x_ref, *refs):
    _odd_pre_tile(x_ref[0], *refs)


def _odd_pre_tile(x, shift_ref, scale_ref, w_ref, dft_ref, sw_ref, sb_ref, u_ref, sf_ref, ys_ref):
    h = _ln(x) * (1.0 + scale_ref[0]) + shift_ref[0]
    z = _dot(h.astype(BF16), w_ref[...])
    for g in range(FNET_GROUPS):
        cs = slice(g * FNET_GROUP_CH, (g + 1) * FNET_GROUP_CH)
        ab = _dot(z[:, cs].astype(BF16), dft_ref[...])
        u_ref[0, :, cs] = ab[:, 0:FNET_GROUP_CH].astype(BF16)
        u_ref[0, :, FNET_W + g * FNET_GROUP_CH:FNET_W + (g + 1) * FNET_GROUP_CH] = ab[:, FNET_GROUP_CH:].astype(BF16)
    sf_ref[0] = _silu(z[:, FNET_W:2 * FNET_W]).astype(BF16)
    o0 = 2 * FNET_W
    tm = z.shape[0]
    for g in range(SGU_GROUPS):
        cs = slice(g * SGU_GROUP_CH, (g + 1) * SGU_GROUP_CH)
        ug = _gelu(z[:, o0 + g * SGU_GROUP_CH:o0 + (g + 1) * SGU_GROUP_CH])
        vg = _ln(_gelu(z[:, o0 + SGU_W + g * SGU_GROUP_CH:o0 + SGU_W + (g + 1) * SGU_GROUP_CH])).astype(BF16)
        sg = _silu(z[:, o0 + 2 * SGU_W + g * SGU_GROUP_CH:o0 + 2 * SGU_W + (g + 1) * SGU_GROUP_CH])
        for c in range(tm // SGU_CHUNK):
            rs = slice(c * SGU_CHUNK, (c + 1) * SGU_CHUNK)
            sv = _dot(sw_ref[g], vg[rs]) + sb_ref[g]
            ys_ref[0, rs, cs] = (ug[rs] * sv * sg[rs]).astype(BF16)


def _even_post_odd_pre(x, o_f, o_b, sgg, o_t, smg, gate, wo, gng, lng, lnb, li, l,
                       shift, scale, w, dft, sw, sb, lj, tm):
    bsz, t, d = x.shape
    tok = lambda w_: pl.BlockSpec((1, tm, w_), lambda b, i: (b, i, 0))
    return pl.pallas_call(
        _even_post_odd_pre_kernel, grid=(bsz, t // tm),
        in_specs=[tok(d), tok(GLA_V_W), tok(GLA_V_W), tok(GLA_V_W),
                  pl.BlockSpec((1, MLA_V_W, tm), lambda b, i: (b, 0, i)), tok(MLA_V_W),
                  gate[1],
                  _layer(wo, li), _layer(gng, li), _layer(lng, l), _layer(lnb, l),
                  shift[1], scale[1], _layer(w, lj), _full(dft.shape), _layer(sw, lj), _layer(sb, lj)],
        out_specs=(tok(d), tok(2 * FNET_W), tok(FNET_W), tok(SGU_W)),
        out_shape=(jax.ShapeDtypeStruct((bsz, t, d), F32),
                   jax.ShapeDtypeStruct((bsz, t, 2 * FNET_W), BF16),
                   jax.ShapeDtypeStruct((bsz, t, FNET_W), BF16),
                   jax.ShapeDtypeStruct((bsz, t, SGU_W), BF16)),
        compiler_params=_cparams("arbitrary", "arbitrary"), name="even_post_odd_pre",
    )(x, o_f, o_b, sgg, o_t, smg, gate[0], wo, gng, lng, lnb, shift[0], scale[0], w, dft, sw, sb)


def _odd_pre(x, shift, scale, w, dft, sw, sb, li, tm):
    bsz, t, d = x.shape
    tok = lambda w_: pl.BlockSpec((1, tm, w_), lambda b, i: (b, i, 0))
    return pl.pallas_call(
        _odd_pre_kernel, grid=(bsz, t // tm),
        in_specs=[tok(d), shift[1], scale[1], _layer(w, li), _full(dft.shape), _layer(sw, li), _layer(sb, li)],
        out_specs=(tok(2 * FNET_W), tok(FNET_W), tok(SGU_W)),
        out_shape=(jax.ShapeDtypeStruct((bsz, t, 2 * FNET_W), BF16),
                   jax.ShapeDtypeStruct((bsz, t, FNET_W), BF16),
                   jax.ShapeDtypeStruct((bsz, t, SGU_W), BF16)),
        compiler_params=_cparams("arbitrary", "arbitrary"), name="odd_pre",
    )(x, shift[0], scale[0], w, dft, sw, sb)


FFT_T2_BLK = 16
FFT_P1_BLK = 8


def _fft1_kernel(u_ref, w1_ref, tc_ref, ts_ref, z_ref):
    n1, tb = FFT_N1, FFT_T2_BLK
    r = n1 * tb
    pq = _dot(w1_ref[...], u_ref[0].reshape(r, 2 * FNET_W))
    zr = pq[0:r, 0:FNET_W] - pq[r:, FNET_W:]
    zi = -pq[0:r, FNET_W:] - pq[r:, 0:FNET_W]
    tc = _tile_lanes(tc_ref[0], FNET_W // 128)
    ts = _tile_lanes(ts_ref[0], FNET_W // 128)
    z_ref[0, :, :, 0:FNET_W] = (zr * tc + zi * ts).astype(BF16).reshape(n1, tb, FNET_W)
    z_ref[0, :, :, FNET_W:] = (zi * tc - zr * ts).astype(BF16).reshape(n1, tb, FNET_W)


def _fft2_kernel(z_ref, c2_ref, s2_ref, y_ref):
    for j in range(FFT_P1_BLK):
        zp = z_ref[0, j]
        y_ref[0, :, j, :] = _dot(c2_ref[...], zp[:, 0:FNET_W]) + _dot(s2_ref[...], zp[:, FNET_W:])


def _dft_tables(t):
    n1, n2 = FFT_N1, t // FFT_N1
    p1 = np.arange(n1, dtype=np.float64)
    a1 = 2.0 * np.pi * np.outer(p1, p1) / n1
    eye = np.eye(FFT_T2_BLK)
    w1 = np.concatenate([np.kron(np.cos(a1), eye), np.kron(np.sin(a1), eye)], axis=0)
    at = 2.0 * np.pi * np.outer(p1, np.arange(n2, dtype=np.float64)) / t

    def twiddle(a):
        a = a.reshape(n1, n2 // FFT_T2_BLK, FFT_T2_BLK).transpose(1, 0, 2).reshape(n2 // FFT_T2_BLK, -1)
        return np.repeat(a[:, :, None], 128, axis=2)
    tc, ts = twiddle(np.cos(at)), twiddle(np.sin(at))
    p2 = np.arange(n2, dtype=np.float64)
    a2 = 2.0 * np.pi * np.outer(p2, p2) / n2
    norm = 1.0 / math.sqrt(t * FNET_GROUP_CH)
    return (jnp.asarray(w1, F32).astype(BF16), jnp.asarray(tc, F32), jnp.asarray(ts, F32),
            jnp.asarray(np.cos(a2) * norm, F32).astype(BF16), jnp.asarray(np.sin(a2) * norm, F32).astype(BF16))


def _channel_dft_matrix():
    d = np.arange(FNET_GROUP_CH, dtype=np.float64)
    a = 2.0 * np.pi * np.outer(d, d) / FNET_GROUP_CH
    return jnp.asarray(np.concatenate([np.cos(a), np.sin(a)], axis=1), F32).astype(BF16)


def _fnet_long(u):
    bsz, t, w2 = u.shape
    n1, n2 = FFT_N1, t // FFT_N1
    w1, tc, ts, c2, s2 = _dft_tables(t)
    blk = pl.BlockSpec((1, n1, FFT_T2_BLK, w2), lambda b, i: (b, 0, i, 0))
    tw = pl.BlockSpec((1, n1 * FFT_T2_BLK, 128), lambda b, i: (i, 0, 0))
    z = pl.pallas_call(
        _fft1_kernel, grid=(bsz, n2 // FFT_T2_BLK),
        in_specs=[blk, _full(w1.shape), tw, tw],
        out_specs=blk,
        out_shape=jax.ShapeDtypeStruct((bsz, n1, n2, w2), BF16),
        compiler_params=_cparams("arbitrary", "arbitrary"), name="fnet_stage1",
    )(u.reshape(bsz, n1, n2, w2), w1, tc, ts)
    y = pl.pallas_call(
        _fft2_kernel, grid=(bsz, n1 // FFT_P1_BLK),
        in_specs=[pl.BlockSpec((1, FFT_P1_BLK, n2, w2), lambda b, i: (b, i, 0, 0)),
                  _full(c2.shape), _full(s2.shape)],
        out_specs=pl.BlockSpec((1, n2, FFT_P1_BLK, FNET_W), lambda b, i: (b, 0, i, 0)),
        out_shape=jax.ShapeDtypeStruct((bsz, n2, n1, FNET_W), F32),
        compiler_params=_cparams("arbitrary", "arbitrary"), name="fnet_stage2",
    )(z, c2, s2)
    return y.reshape(bsz, t, FNET_W)


def _fnet_short_kernel(u_ref, c_ref, s_ref, y_ref):
    u = u_ref[0]
    y_ref[0] = _dot(c_ref[...], u[:, 0:FNET_W]) - _dot(s_ref[...], u[:, FNET_W:])


def _fnet_short(u):
    bsz, t, w2 = u.shape
    p = np.arange(t, dtype=np.float64)
    a = 2.0 * np.pi * np.outer(p, p) / t
    norm = 1.0 / math.sqrt(t * FNET_GROUP_CH)
    c, s = jnp.asarray(np.cos(a) * norm, F32).astype(BF16), jnp.asarray(np.sin(a) * norm, F32).astype(BF16)
    return pl.pallas_call(
        _fnet_short_kernel, grid=(bsz,),
        in_specs=[pl.BlockSpec((1, t, w2), lambda b: (b, 0, 0)), _full(c.shape), _full(s.shape)],
        out_specs=pl.BlockSpec((1, t, FNET_W), lambda b: (b, 0, 0)),
        out_shape=jax.ShapeDtypeStruct((bsz, t, FNET_W), F32),
        compiler_params=_cparams("arbitrary"), name="fnet_short",
    )(u, c, s)


def _odd_post_tile(x_ref, fr_ref, sf_ref, ys_ref, gate_ref, wo_ref, lng_ref, lnb_ref):
    yf = (fr_ref[0] * sf_ref[0].astype(F32)).astype(BF16)
    y = _dot(yf, wo_ref[0:FNET_W, :]) + _dot(ys_ref[0], wo_ref[FNET_W:, :])
    return _post_tail(x_ref[0], y, gate_ref[0], lng_ref[...], lnb_ref[...])


def _odd_post_kernel(*refs):
    refs[-1][0] = _odd_post_tile(*refs[:-1])


def _odd_post_even_pre_kernel(*refs):
    post_in, pre_par, x_out_ref, pre_out = refs[:8], refs[8:23], refs[23], refs[24:]
    x_new = _odd_post_tile(*post_in)
    x_out_ref[0] = x_new
    _even_pre_tile(x_new, *pre_par, *pre_out)


def _odd_post_even_pre(x, fr, sf, ys, gate, wo, lng, lnb, li, l, shift, scale, wts, lj, tabs, tm):
    bsz, t, d = x.shape
    hq = MLA_HEADS * MLA_HEAD_PAD
    tok = lambda w: pl.BlockSpec((1, tm, w), lambda b, i: (b, i, 0))
    tab = pl.BlockSpec((tm, MLA_HEAD_PAD), lambda b, i: (i, 0))
    pre_shapes, pre_specs = _even_pre_outputs(bsz, t, tm)
    place = _rotary_key_placement()
    return pl.pallas_call(
        _odd_post_even_pre_kernel, grid=(bsz, t // tm),
        in_specs=[tok(d), tok(FNET_W), tok(FNET_W), tok(SGU_W), gate[1],
                  _layer(wo, li), _layer(lng, l), _layer(lnb, l),
                  shift[1], scale[1]] + [_layer(w, lj) for w in wts] + [tab] * 2 + [_full(place.shape)],
        out_specs=(tok(d),) + pre_specs,
        out_shape=(jax.ShapeDtypeStruct((bsz, t, d), F32),) + pre_shapes,
        compiler_params=_cparams("arbitrary", "arbitrary"), name="odd_post_even_pre",
    )(x, fr, sf, ys, gate[0], wo, lng, lnb, shift[0], scale[0], *wts, *tabs, place)


def _odd_post(x, fr, sf, ys, gate, wo, lng, lnb, li, l, tm):
    bsz, t, d = x.shape
    tok = lambda w: pl.BlockSpec((1, tm, w), lambda b, i: (b, i, 0))
    return pl.pallas_call(
        _odd_post_kernel, grid=(bsz, t // tm),
        in_specs=[tok(d), tok(FNET_W), tok(FNET_W), tok(SGU_W),
                  gate[1],
                  _layer(wo, li), _layer(lng, l), _layer(lnb, l)],
        out_specs=tok(d), out_shape=jax.ShapeDtypeStruct((bsz, t, d), F32),
        compiler_params=_cparams("arbitrary", "arbitrary"), name="odd_post",
    )(x, fr, sf, ys, gate[0], wo, lng, lnb)


def _even_weights(w_in, gla_w2, gla_b, q_norm_g, w_uq, kv_norm_g, w_ukv):
    d = w_in.shape[0]
    idx = np.cumsum(EVEN_IN_SIZES)[:-1].tolist()
    gq, gk, gv, glr, gg, cq, ckv, kr, mg = jnp.split(w_in, idx, axis=1)
    half = MLA_ROPE // 2
    z = lambda n: jnp.zeros((d, n), w_in.dtype)
    small = jnp.concatenate([glr, kr, kr[:, half:], kr[:, :half], z(128 - 2 * GLA_GATE_RANK - 2 * MLA_ROPE)], axis=1)
    w = jnp.concatenate([gq * GLA_DK ** -0.5, gk, gv, gg, cq, ckv, mg, small], axis=1).astype(BF16)
    zw = jnp.zeros((GLA_GATE_RANK, GLA_QK_W), F32)
    w2 = jnp.concatenate([jnp.concatenate([gla_w2[0], zw], axis=1), jnp.concatenate([zw, gla_w2[1]], axis=1),
                          jnp.zeros((128 - 2 * GLA_GATE_RANK, 2 * GLA_QK_W), F32)], axis=0)
    gb = jnp.concatenate([gla_b[0], gla_b[1]])[None, :]
    uq = w_uq.reshape(MLA_Q_RANK, MLA_HEADS, MLA_NOPE + MLA_ROPE)
    pad = MLA_HEAD_PAD - MLA_NOPE - MLA_ROPE
    zq = lambda n: jnp.zeros((MLA_Q_RANK, MLA_HEADS, n), w_uq.dtype)
    wqa = jnp.concatenate([uq, zq(pad)], axis=2).reshape(MLA_Q_RANK, -1)
    wqb = jnp.concatenate([zq(MLA_NOPE), uq[:, :, MLA_NOPE + half:], uq[:, :, MLA_NOPE:MLA_NOPE + half], zq(pad)],
                          axis=2).reshape(MLA_Q_RANK, -1)
    ukv = w_ukv.reshape(MLA_KV_RANK, MLA_HEADS, MLA_NOPE + MLA_V)
    wk = jnp.concatenate([ukv[:, :, :MLA_NOPE], jnp.zeros((MLA_KV_RANK, MLA_HEADS, MLA_HEAD_PAD - MLA_NOPE), w_ukv.dtype)],
                         axis=2).reshape(MLA_KV_RANK, -1)
    vpad = MLA_VT_ROWS - MLA_V
    wv = jnp.concatenate([ukv[:, :, MLA_NOPE:], jnp.zeros((MLA_KV_RANK, MLA_HEADS, vpad), w_ukv.dtype)],
                         axis=2).reshape(MLA_KV_RANK, -1)
    vbias = np.zeros((MLA_HEADS, MLA_VT_ROWS), np.float32)
    vbias[:, MLA_V] = 1.0
    return (w, w2.astype(BF16), gb, q_norm_g[None, :], wqa.astype(BF16), wqb.astype(BF16),
            kv_norm_g[None, :], wk.astype(BF16), wv.astype(BF16), jnp.asarray(vbias.reshape(1, -1)))


def _rope_tables(n):
    row = jnp.repeat(jnp.arange(n // GRID_W, dtype=F32), GRID_W)
    col = (jnp.arange(n) % GRID_W).astype(F32)
    n_freq = MLA_ROPE // 4
    inv = ROPE_BASE ** (-jnp.arange(n_freq, dtype=F32) / n_freq)
    ang = jnp.concatenate([row[:, None] * inv, col[:, None] * inv], -1)
    cos, sin = jnp.cos(ang), jnp.sin(ang)
    pad = jnp.zeros((n, MLA_HEAD_PAD - MLA_NOPE - MLA_ROPE), F32)
    zn = jnp.zeros((n, MLA_NOPE), F32)
    ck = jnp.concatenate([zn, cos, cos, pad], axis=1)
    sk = jnp.concatenate([zn, -sin, sin, pad], axis=1)
    return ck, sk


def _plain_tables(n):
    pad = jnp.zeros((n, MLA_HEAD_PAD - MLA_NOPE - MLA_ROPE), F32)
    zn = jnp.zeros((n, MLA_NOPE), F32)
    ck = jnp.concatenate([zn, jnp.ones((n, MLA_ROPE), F32), pad], axis=1)
    return ck, jnp.zeros((n, MLA_HEAD_PAD), F32)


def _pick(t, pref):
    return pref if t % pref == 0 else t


def kernel(x, c, ctx, c_ctx, ada_w, ada_b, post_ln_g, post_ln_b, even_w_in, gla_w2, gla_b, gla_norm_g,
           mla_q_norm_g, mla_w_uq, mla_kv_norm_g, mla_w_ukv, even_w_out, odd_w_in, sgu_w, sgu_b, odd_w_out):
    bsz, n, d = x.shape
    lc = ctx.shape[1]
    depth = ada_w.shape[0]
    assert bsz + 1 <= 8 and n % (FFT_N1 * FFT_T2_BLK) == 0 and n % 512 == 0 and lc % MLA_TK == 0

    cond = jnp.concatenate([c, c_ctx[None, :], jnp.zeros((8 - bsz - 1, d), F32)], axis=0)
    mods = _mods(cond, ada_w, ada_b)

    mods4 = mods.reshape(depth, 8, 1, 3 * d)

    def lat_mod(l, j):
        return mods4, pl.BlockSpec((None, 1, 1, d), lambda b, *_: (l, b, 0, j))

    def ctx_mod(l, j):
        return mods4, pl.BlockSpec((None, 1, 1, d), lambda b, *_: (l, bsz, 0, j))

    rope_tabs = _rope_tables(n)
    ctx_tabs = _plain_tables(lc)
    dft_c = _channel_dft_matrix()
    tm_lat, tm_ctx = _pick(n, 512), _pick(lc, 256)
    tm_post = _pick(n, 1024)
    zero_state = jnp.zeros((bsz, 2, GLA_QK_W, GLA_DV), F32)

    e_wts = jax.vmap(_even_weights)(even_w_in, gla_w2, gla_b, mla_q_norm_g, mla_w_uq, mla_kv_norm_g, mla_w_ukv)
    e_wo = even_w_out.astype(BF16)
    e_gng = gla_norm_g[:, None, :]
    o_w = odd_w_in.astype(BF16)
    o_wo = odd_w_out.astype(BF16)
    o_sw = sgu_w.astype(BF16)
    o_sb = jnp.broadcast_to(sgu_b[:, :, :, None], sgu_b.shape + (SGU_GROUP_CH,))
    lng, lnb = post_ln_g[:, None, :], post_ln_b[:, None, :]

    odd_in = even_in = None
    for l in range(depth):
        need_ctx_out = any(j % 2 == 0 for j in range(l + 1, depth))
        i = l // 2
        if l % 2 == 0:
            pc = _even_pre(ctx, ctx_mod(l, 0), ctx_mod(l, 1), e_wts, i, ctx_tabs, tm_ctx)
            q_c, k_c, v_c, g_c, sgg_c, smg_c, qt_c, kk_c, vt_c = pc
            if even_in is None:
                even_in = _even_pre(x, lat_mod(l, 0), lat_mod(l, 1), e_wts, i, rope_tabs, tm_lat)
            q_l, k_l, v_l, g_l, sgg_l, smg_l, qt_l, kk_l, vt_l = even_in
            even_in = None
            of_c, ob_c, s_c = _gla(q_c, k_c, v_c, g_c, zero_state, GLA_SUB)
            of_l, ob_l, _ = _gla(q_l, k_l, v_l, g_l, s_c, 2 * GLA_SUB)
            ot_l = _mla(qt_l, [(kk_l, vt_l), (kk_c, vt_c)], 512, MLA_TK, n_sub=next(k for k in (4, 2, 1) if n % (512 * k) == 0))
            if l + 1 < depth:
                x_new, *odd_in = _even_post_odd_pre(
                    x, of_l, ob_l, sgg_l, ot_l, smg_l, lat_mod(l, 2), e_wo, e_gng, lng, lnb, i, l,
                    lat_mod(l + 1, 0), lat_mod(l + 1, 1), o_w, dft_c, o_sw, o_sb, (l + 1) // 2, tm_lat)
            else:
                x_new = _even_post(x, of_l, ob_l, sgg_l, ot_l, smg_l, lat_mod(l, 2), e_wo, e_gng, lng, lnb,
                                   i, l, tm_post)
            if need_ctx_out:
                ot_c = _mla(qt_c, [(kk_c, vt_c)], lc, MLA_TK)
                ctx = _even_post(ctx, of_c, ob_c, sgg_c, ot_c, smg_c, ctx_mod(l, 2), e_wo, e_gng, lng, lnb, i, l, tm_ctx)
            x = x_new
        else:
            if odd_in is None:
                odd_in = _odd_pre(x, lat_mod(l, 0), lat_mod(l, 1), o_w, dft_c, o_sw, o_sb, i, tm_lat)
            u, sf, ys = odd_in
            odd_in = None
            if l + 1 < depth:
                x_new, *even_in = _odd_post_even_pre(
                    x, _fnet_long(u), sf, ys, lat_mod(l, 2), o_wo, lng, lnb, i, l,
                    lat_mod(l + 1, 0), lat_mod(l + 1, 1), e_wts, (l + 1) // 2, rope_tabs, tm_lat)
            else:
                x_new = _odd_post(x, _fnet_long(u), sf, ys, lat_mod(l, 2), o_wo, lng, lnb, i, l, tm_post)
            if need_ctx_out:
                u, sf, ys = _odd_pre(ctx, ctx_mod(l, 0), ctx_mod(l, 1), o_w, dft_c, o_sw, o_sb, i, tm_ctx)
                ctx = _odd_post(ctx, _fnet_short(u), sf, ys, ctx_mod(l, 2), o_wo, lng, lnb, i, l, tm_ctx)
            x = x_new
    return x
```
